```python
import math
import jax, jax.numpy as jnp
from jax import lax
import numpy as np

D_MODEL = 1024
BATCH = 4
SEQ = 4096
DEPTH = 2
DEC_BATCH = 32
DEC_SEQ = 4
PAST_LEN = 16384
PAGE_SIZE = 128

N_BRANCH = 4
BR_WIDTH = D_MODEL // N_BRANCH
A_HEADS = 4
A_DK = BR_WIDTH // A_HEADS
A_DV = BR_WIDTH // A_HEADS
B_HEADS = 4
B_KV_HEADS = 2
B_GROUP = B_HEADS // B_KV_HEADS
B_HD = BR_WIDTH // B_HEADS
NSA_BLOCK = 64
NSA_TOPK = 16
NSA_WINDOW = 512
NSA_QBLOCK = 64
C_HEADS = 4
C_D = BR_WIDTH // (2 * C_HEADS)
D_HEADS = 4
D_DK = BR_WIDTH // (2 * D_HEADS)
D_DV = BR_WIDTH // D_HEADS
GLA_RANK = 16
GLA_TAU = 16.0
LIN_CHUNK = 64
ATTN_QBLOCK = 128
D_FF = 2816
NORM_EPS = 1e-6
N_ALIBI = B_HEADS + C_HEADS

IN_SPLITS = (
    BR_WIDTH, BR_WIDTH, BR_WIDTH, BR_WIDTH,
    BR_WIDTH, 6 * B_KV_HEADS * B_HD, 3 * B_HEADS,
    BR_WIDTH, BR_WIDTH, BR_WIDTH,
    D_HEADS * D_DK, D_HEADS * D_DK, BR_WIDTH, GLA_RANK, BR_WIDTH,
    N_BRANCH * D_MODEL,
)
N_IN = sum(IN_SPLITS)

kernel_name = 'hybrid_hgrn2_nsa_diff_gla_decode_step'


def _rmsnorm(x, g):
    xf = x.astype(jnp.float32)
    y = xf * lax.rsqrt(jnp.mean(xf * xf, axis=-1, keepdims=True) + NORM_EPS)
    return (y * g.astype(jnp.float32)).astype(x.dtype)


def _swiglu(x, w_in, w_out):
    gate, up = jnp.split(x @ w_in, 2, axis=-1)
    return (jax.nn.silu(gate) * up) @ w_out


def _masked_softmax(s, mask):
    s = jnp.where(mask, s.astype(jnp.float32), -jnp.inf)
    m = jnp.max(s, axis=-1, keepdims=True)
    m = jnp.where(jnp.isfinite(m), m, 0.0)
    e = jnp.exp(s - m)
    d = jnp.sum(e, axis=-1, keepdims=True)
    return e / jnp.where(d > 0, d, 1.0)


def _alibi_slopes():
    i = jnp.arange(1, N_ALIBI + 1, dtype=jnp.float32)
    return jnp.exp2(-8.0 * i / N_ALIBI)


def _gather_pages(pool, layer, page_table):
    rows = pool[layer, page_table]
    return rows.reshape((page_table.shape[0], -1) + rows.shape[3:])


def _chunk_gated_linear(q, k, v, log_f, state0):
    bsz, t, nh, _ = q.shape
    dv = v.shape[-1]
    c = min(LIN_CHUNK, t)
    pad = (-t) % c
    n = (t + pad) // c

    def prep(a):
        a = jnp.pad(a.astype(jnp.float32), ((0, 0), (0, pad), (0, 0), (0, 0)))
        return a.reshape(bsz, n, c, nh, a.shape[-1]).transpose(1, 0, 3, 2, 4)

    causal = jnp.tril(jnp.ones((c, c), dtype=bool))[:, :, None]

    def step(s, inp):
        qc, kc, vc, gc = inp
        b = jnp.cumsum(gc, axis=2)
        o = jnp.einsum('bhtk,bhkv->bhtv', qc * jnp.exp(b), s)
        decay = jnp.exp(jnp.where(causal, b[:, :, :, None, :] - b[:, :, None, :, :], -jnp.inf))
        att = jnp.einsum('bhtk,bhsk,bhtsk->bhts', qc, kc, decay)
        o = o + jnp.einsum('bhts,bhsv->bhtv', att, vc)
        b_last = b[:, :, -1]
        s = jnp.exp(b_last)[..., None] * s + jnp.einsum('bhsk,bhsv->bhkv', kc * jnp.exp(b_last[:, :, None] - b), vc)
        return s, o

    s, o = lax.scan(step, state0.astype(jnp.float32), (prep(q), prep(k), prep(v), prep(log_f)))
    o = o.transpose(1, 0, 3, 2, 4).reshape(bsz, n * c, nh, dv)[:, :t]
    return o, s


def _diff_attend(q, q_pos, k, v, k_pos, lam, slopes):
    dist = q_pos[:, None] - k_pos[None, :]
    s = jnp.einsum('bqhmd,bkhmd->bhmqk', q, k).astype(jnp.float32) * (C_D ** -0.5) - slopes[None, :, None, None, None] * dist
    p = _masked_softmax(s, dist >= 0)
    w = p[:, :, 0] - lam * p[:, :, 1]
    return jnp.einsum('bhqk,bkhe->bqhe', w, v)


def _nsa_compress(kv_full, cmp_pe, cmp_w):
    bsz, length = kv_full.shape[:2]
    nc = length // NSA_BLOCK
    blocks = kv_full[:, :nc * NSA_BLOCK, :2].reshape(bsz, nc, NSA_BLOCK, 2, B_KV_HEADS, B_HD)
    summ = jnp.einsum('bnjcgd,cjde->bncge', blocks, cmp_w) + jnp.einsum('cjd,cjde->ce', cmp_pe, cmp_w)[None, None, :, None, :]
    return summ[:, :, 0], summ[:, :, 1]


def _nsa_attend(q, q_pos, kv_full, k_cmp, v_cmp, win_kv, win_pos, slopes):
    bsz, tq = q.shape[:2]
    length = kv_full.shape[1]
    nc = k_cmp.shape[1]
    scale = B_HD ** -0.5
    sl = slopes.reshape(1, B_KV_HEADS, B_GROUP, 1, 1)
    cmp_end = jnp.arange(nc) * NSA_BLOCK + (NSA_BLOCK - 1)
    dist = q_pos[:, None] - cmp_end[None, :]
    s = jnp.einsum('bqgnd,bcgd->bgnqc', q, k_cmp).astype(jnp.float32) * scale - sl * dist
    p_cmp = _masked_softmax(s, dist >= 0)
    o_cmp = jnp.einsum('bgnqc,bcgd->bqgnd', p_cmp, v_cmp)
    nb = -(-length // NSA_BLOCK)
    k_sel = min(NSA_TOPK, nb)
    imp = jnp.pad(p_cmp.sum(axis=2), ((0, 0), (0, 0), (0, 0), (0, nb - nc)))
    blk = jnp.arange(nb)[None, :]
    cur = (q_pos // NSA_BLOCK)[:, None]
    forced = (blk == 0) | (blk == cur) | (blk == cur - 1)
    score = jnp.where(blk > cur, -jnp.inf, jnp.where(forced, jnp.inf, imp))
    _, sel = lax.top_k(score, k_sel)
    pos = (sel[..., None] * NSA_BLOCK + jnp.arange(NSA_BLOCK)).reshape(bsz, B_KV_HEADS, tq, k_sel * NSA_BLOCK)
    b_idx = jnp.arange(bsz)[:, None, None, None]
    g_idx = jnp.arange(B_KV_HEADS)[None, :, None, None]
    rows = kv_full[b_idx, jnp.minimum(pos, length - 1), 2:4, g_idx]
    dist = q_pos[None, None, :, None] - pos
    s = jnp.einsum('bqgnd,bgqmd->bgnqm', q, rows[..., 0, :]).astype(jnp.float32) * scale - sl * dist[:, :, None]
    p = _masked_softmax(s, (dist >= 0)[:, :, None])
    o_sel = jnp.einsum('bgnqm,bgqmd->bqgnd', p, rows[..., 1, :])
    dist = q_pos[:, None] - win_pos[None, :]
    mask = (dist >= 0) & (dist < NSA_WINDOW) & (win_pos >= 0)[None, :]
    s = jnp.einsum('bqgnd,bkgd->bgnqk', q, win_kv[:, :, 0]).astype(jnp.float32) * scale - sl * dist
    p = _masked_softmax(s, mask)
    o_win = jnp.einsum('bgnqk,bkgd->bqgnd', p, win_kv[:, :, 1])
    return o_cmp, o_sel, o_win


def _token_mixer(h, l, lw, lower_bound, slopes, past):
    bsz, t, _ = h.shape
    f32 = jnp.float32
    pos0 = 0 if past is None else PAST_LEN
    q_pos = pos0 + jnp.arange(t)
    points = [int(p) for p in np.cumsum(IN_SPLITS)[:-1]]
    (a_q, a_f, a_i, a_g, b_q, b_kv, b_gate, c_q, c_k, c_v,
     d_q, d_k, d_v, d_lr, d_g, merge) = jnp.split(h @ lw['w_in'], points, axis=-1)

    sa = (bsz, t, A_HEADS, A_DK)
    lb = lower_bound.reshape(A_HEADS, A_DK)
    log_f = jnp.logaddexp(jnp.log1p(-lb) + jax.nn.log_sigmoid(a_f.reshape(sa).astype(f32)), jnp.log(lb))
    s0 = jnp.zeros((bsz, A_HEADS, A_DK, A_DV), f32) if past is None else past['state_hgrn'][l]
    o_a, st_a = _chunk_gated_linear(a_q.reshape(sa) * (A_DK ** -0.5), -jnp.expm1(log_f),
                                    a_i.reshape(bsz, t, A_HEADS, A_DV), log_f, s0)
    o_a = (_rmsnorm(o_a, lw['hgrn_norm'].reshape(A_HEADS, A_DV))
           * jax.nn.silu(a_g.reshape(bsz, t, A_HEADS, A_DV))).reshape(bsz, t, BR_WIDTH)

    sd = (bsz, t, D_HEADS, D_DK)
    d_logf = (jax.nn.log_sigmoid((d_lr @ lw['gla_w2'] + lw['gla_b']).astype(f32)) / GLA_TAU).reshape(sd)
    s0 = jnp.zeros((bsz, D_HEADS, D_DK, D_DV), f32) if past is None else past['state_gla'][l]
    o_d, st_d = _chunk_gated_linear(d_q.reshape(sd) * (D_DK ** -0.5), d_k.reshape(sd),
                                    d_v.reshape(bsz, t, D_HEADS, D_DV), d_logf, s0)
    o_d = (_rmsnorm(o_d, lw['gla_norm'].reshape(D_HEADS, D_DV))
           * jax.nn.silu(d_g.reshape(bsz, t, D_HEADS, D_DV))).reshape(bsz, t, BR_WIDTH)

    cq = c_q.reshape(bsz, t, C_HEADS, 2, C_D)
    new_c = jnp.stack([c_k.reshape(bsz, t, C_HEADS, 2 * C_D), c_v.reshape(bsz, t, C_HEADS, 2 * C_D)], axis=2)
    if past is None:
        kv_c = new_c
    else:
        kv_c = jnp.concatenate([_gather_pages(past['cache_diff_kv'], l, past['page_table']), new_c], axis=1)
    lk = kv_c.shape[1]
    k_pos = jnp.arange(lk)
    kc = kv_c[:, :, 0].reshape(bsz, lk, C_HEADS, 2, C_D)
    vc = kv_c[:, :, 1]
    lv = lw['diff_lambda'].astype(f32)
    lam_init = 0.8 - 0.6 * math.exp(-0.3 * l)
    lam = jnp.exp(jnp.sum(lv[0] * lv[1])) - jnp.exp(jnp.sum(lv[2] * lv[3])) + lam_init
    sl_c = slopes[1::2]
    if past is None:
        qb = min(ATTN_QBLOCK, t)
        nq = t // qb
        o_c = lax.map(lambda a: _diff_attend(a[0], a[1], kc, vc, k_pos, lam, sl_c),
                      (cq.reshape(bsz, nq, qb, C_HEADS, 2, C_D).swapaxes(0, 1), q_pos.reshape(nq, qb)))
        o_c = o_c.swapaxes(0, 1).reshape(bsz, t, C_HEADS, 2 * C_D)
    else:
        o_c = _diff_attend(cq, q_pos, kc, vc, k_pos, lam, sl_c)
    o_c = (_rmsnorm(o_c, lw['diff_norm']) * (1.0 - lam_init)).reshape(bsz, t, BR_WIDTH)

    bq = b_q.reshape(bsz, t, B_KV_HEADS, B_GROUP, B_HD)
    new_b = b_kv.reshape(bsz, t, 6, B_KV_HEADS, B_HD)
    new_paged = new_b[:, :, :4]
    new_win = new_b[:, :, 4:]
    sl_b = slopes[0::2]
    if past is None:
        kv_b = new_paged
        k_cmp, v_cmp = _nsa_compress(kv_b, lw['nsa_pe'], lw['nsa_w'])
        qb = min(NSA_QBLOCK, t)
        nq = t // qb
        win_pad = jnp.pad(new_win, ((0, 0), (NSA_WINDOW, 0), (0, 0), (0, 0), (0, 0)))

        def body(args):
            qblk, start = args
            win_kv = lax.dynamic_slice_in_dim(win_pad, start, NSA_WINDOW + qb, axis=1)
            win_pos = start - NSA_WINDOW + jnp.arange(NSA_WINDOW + qb)
            return _nsa_attend(qblk, start + jnp.arange(qb), kv_b, k_cmp, v_cmp, win_kv, win_pos, sl_b)

        outs = lax.map(body, (bq.reshape(bsz, nq, qb, B_KV_HEADS, B_GROUP, B_HD).swapaxes(0, 1), jnp.arange(nq) * qb))
        o_cmp, o_sel, o_win = [a.swapaxes(0, 1).reshape(bsz, t, B_KV_HEADS, B_GROUP, B_HD) for a in outs]
        win_state = new_win[:, -min(NSA_WINDOW, t):]
    else:
        kv_b = jnp.concatenate([_gather_pages(past['cache_nsa_kv'], l, past['page_table']), new_paged], axis=1)
        k_cmp, v_cmp = _nsa_compress(kv_b, lw['nsa_pe'], lw['nsa_w'])
        wc = past['cache_nsa_win'].shape[2]
        win_kv = jnp.concatenate([past['cache_nsa_win'][l], new_win], axis=1)
        win_pos = PAST_LEN - wc + jnp.arange(wc + t)
        o_cmp, o_sel, o_win = _nsa_attend(bq, q_pos, kv_b, k_cmp, v_cmp, win_kv, win_pos, sl_b)
        win_state = win_kv[:, -wc:]
    gate = jax.nn.sigmoid(b_gate + lw['nsa_gate_b']).reshape(bsz, t, B_KV_HEADS, B_GROUP, 3)
    o_b = (gate[..., 0:1] * o_cmp + gate[..., 1:2] * o_sel + gate[..., 2:3] * o_win).reshape(bsz, t, BR_WIDTH)

    branches = jnp.stack([o_a, o_b, o_c, o_d], axis=2).astype(h.dtype)
    up = jnp.einsum('btnr,nrd->btnd', branches, lw['w_branch'])
    gates = jax.nn.sigmoid(merge.reshape(bsz, t, N_BRANCH, D_MODEL))
    y = jnp.sum(gates * up, axis=2) @ lw['w_out']
    return y, (new_paged, win_state, new_c, st_a, st_d)


def _layer(x, l, lw, lower_bound, slopes, past):
    g = lw['norm']
    x = x + 0.5 * _rmsnorm(_swiglu(_rmsnorm(x, g[0]), lw['ffn_in'][0], lw['ffn_out'][0]), g[1])
    y, new = _token_mixer(_rmsnorm(x, g[2]), l, lw, lower_bound, slopes, past)
    x = x + _rmsnorm(y, g[3])
    x = x + 0.5 * _rmsnorm(_swiglu(_rmsnorm(x, g[4]), lw['ffn_in'][1], lw['ffn_out'][1]), g[5])
    return x, new


def setup_inputs(seed: int = 0) -> dict:
    key = jax.random.key(seed)
    ks = jax.random.split(key, 24)
    n_pages = PAST_LEN // PAGE_SIZE
    n_pool = (DEC_BATCH * n_pages * 5) // 4
    win_cache = min(NSA_WINDOW, PAST_LEN)

    def nrm(k, shape, scale):
        return scale * jax.random.normal(k, shape, jnp.float32)

    page_table = jax.random.permutation(ks[7], n_pool)[:DEC_BATCH * n_pages].reshape(DEC_BATCH, n_pages).astype(jnp.int32)
    return {
        'x_prompt': nrm(ks[0], (BATCH, SEQ, D_MODEL), 1.0),
        'x_sample': nrm(ks[1], (DEC_BATCH, DEC_SEQ, D_MODEL), 1.0),
        'cache_nsa_kv': nrm(ks[2], (DEPTH, n_pool, PAGE_SIZE, 4, B_KV_HEADS, B_HD), 1.0),
        'cache_nsa_win': nrm(ks[3], (DEPTH, DEC_BATCH, win_cache, 2, B_KV_HEADS, B_HD), 1.0),
        'cache_diff_kv': nrm(ks[4], (DEPTH, n_pool, PAGE_SIZE, 2, C_HEADS, 2 * C_D), 1.0),
        'state_hgrn': nrm(ks[5], (DEPTH, DEC_BATCH, A_HEADS, A_DK, A_DV), 0.5),
        'state_gla': nrm(ks[6], (DEPTH, DEC_BATCH, D_HEADS, D_DK, D_DV), 0.5),
        'page_table': page_table,
        'norm_gains': 1.0 + nrm(ks[8], (DEPTH, 6, D_MODEL), 0.05),
        'ffn_w_in': nrm(ks[9], (DEPTH, 2, D_MODEL, 2 * D_FF), D_MODEL ** -0.5),
        'ffn_w_out': nrm(ks[10], (DEPTH, 2, D_FF, D_MODEL), D_FF ** -0.5),
        'w_in': nrm(ks[11], (DEPTH, D_MODEL, N_IN), D_MODEL ** -0.5),
        'hgrn_lb_logits': nrm(ks[12], (DEPTH, A_HEADS * A_DK), 1.0),
        'hgrn_norm_gain': 1.0 + nrm(ks[13], (DEPTH, A_HEADS * A_DV), 0.05),
        'nsa_cmp_pe': nrm(ks[14], (DEPTH, 2, NSA_BLOCK, B_HD), 0.1),
        'nsa_cmp_w': nrm(ks[15], (DEPTH, 2, NSA_BLOCK, B_HD, B_HD), (NSA_BLOCK * B_HD) ** -0.5),
        'nsa_gate_b': nrm(ks[16], (DEPTH, 3 * B_HEADS), 0.1),
        'diff_lambda': nrm(ks[17], (DEPTH, 4, C_D), 0.1),
        'diff_norm_gain': 1.0 + nrm(ks[18], (DEPTH, 2 * C_D), 0.05),
        'gla_gate_w2': nrm(ks[19], (DEPTH, GLA_RANK, D_HEADS * D_DK), GLA_RANK ** -0.5),
        'gla_gate_b': nrm(ks[20], (DEPTH, D_HEADS * D_DK), 0.1),
        'gla_norm_gain': 1.0 + nrm(ks[21], (DEPTH, D_HEADS * D_DV), 0.05),
        'w_branch': nrm(ks[22], (DEPTH, N_BRANCH, BR_WIDTH, D_MODEL), BR_WIDTH ** -0.5),
        'w_out': nrm(ks[23], (DEPTH, D_MODEL, D_MODEL), D_MODEL ** -0.5),
    }


def reference(x_prompt, x_sample, cache_nsa_kv, cache_nsa_win, cache_diff_kv, state_hgrn, state_gla, page_table,
              norm_gains, ffn_w_in, ffn_w_out, w_in, hgrn_lb_logits, hgrn_norm_gain, nsa_cmp_pe, nsa_cmp_w,
              nsa_gate_b, diff_lambda, diff_norm_gain, gla_gate_w2, gla_gate_b, gla_norm_gain, w_branch, w_out):
    lb_cum = jnp.cumsum(jax.nn.softmax(hgrn_lb_logits.astype(jnp.float32), axis=0), axis=0)
    lower_bounds = lb_cum - lb_cum[0]
    slopes = _alibi_slopes()
    past = {'cache_nsa_kv': cache_nsa_kv, 'cache_nsa_win': cache_nsa_win, 'cache_diff_kv': cache_diff_kv,
            'state_hgrn': state_hgrn, 'state_gla': state_gla, 'page_table': page_table}
    y_prompt, y_sample = x_prompt, x_sample
    new_p, new_s = [], []
    for l in range(DEPTH):
        lw = {'norm': norm_gains[l], 'ffn_in': ffn_w_in[l], 'ffn_out': ffn_w_out[l], 'w_in': w_in[l],
              'hgrn_norm': hgrn_norm_gain[l], 'nsa_pe': nsa_cmp_pe[l], 'nsa_w': nsa_cmp_w[l],
              'nsa_gate_b': nsa_gate_b[l], 'diff_lambda': diff_lambda[l], 'diff_norm': diff_norm_gain[l],
              'gla_w2': gla_gate_w2[l], 'gla_b': gla_gate_b[l], 'gla_norm': gla_norm_gain[l],
              'w_branch': w_branch[l], 'w_out': w_out[l]}
        y_prompt, e_p = _layer(y_prompt, l, lw, lower_bounds[l], slopes, None)
        new_p.append(e_p)
        y_sample, e_s = _layer(y_sample, l, lw, lower_bounds[l], slopes, past)
        new_s.append(e_s)
    nsa_kv_prompt = jnp.stack([e[0] for e in new_p])
    nsa_kv_sample = jnp.stack([e[0] for e in new_s])
    nsa_win_prompt = jnp.stack([e[1] for e in new_p])
    nsa_win_sample = jnp.stack([e[1] for e in new_s])
    diff_kv_prompt = jnp.stack([e[2] for e in new_p])
    diff_kv_sample = jnp.stack([e[2] for e in new_s])
    hgrn_prompt = jnp.stack([e[3] for e in new_p])
    hgrn_sample = jnp.stack([e[3] for e in new_s])
    gla_prompt = jnp.stack([e[4] for e in new_p])
    gla_sample = jnp.stack([e[4] for e in new_s])
    return (y_prompt, y_sample, nsa_kv_prompt, nsa_kv_sample, nsa_win_prompt, nsa_win_sample,
            diff_kv_prompt, diff_kv_sample, hgrn_prompt, hgrn_sample, gla_prompt, gla_sample)
```

```python
import functools
import math

import numpy as np
import jax
import jax.numpy as jnp
from jax import lax
from jax.experimental import pallas as pl
from jax.experimental.pallas import tpu as pltpu

F32 = jnp.float32
BF16 = jnp.bfloat16
HI = lax.Precision.HIGHEST

N_BRANCH = 4
A_HEADS = 4
B_HEADS = 4
B_KV_HEADS = 2
B_GROUP = 2
C_HEADS = 4
D_HEADS = 4
NSA_BLOCK = 64
NSA_TOPK = 16
NSA_WINDOW = 512
GLA_RANK = 16
GLA_TAU = 16.0
NORM_EPS = 1e-6
LANE = 128
LIN_GROUP = 16
NEW_PAD = 16
VMEM_LIMIT = 56 * 1024 * 1024

NT_DIMS = (((1,), (1,)), ((), ()))
TN_DIMS = (((0,), (0,)), ((), ()))


def _params(semantics):
    return pltpu.CompilerParams(dimension_semantics=semantics, vmem_limit_bytes=VMEM_LIMIT)


def _rms(x, g):
    return x * lax.rsqrt(jnp.mean(x * x, axis=-1, keepdims=True) + NORM_EPS) * g


def _dot(a, b):
    return jnp.dot(a, b, preferred_element_type=F32)


def _dot_nt(a, b):
    return lax.dot_general(a, b, NT_DIMS, preferred_element_type=F32)


def _row_tile(n, cap):
    t = min(n, cap)
    while n % t or t % 8:
        t -= 1
    return t


def _ffn_body(x_ref, gpre_ref, gpost_ref, wg_ref, wu_ref, wo_ref, o_ref, xn_ref, acc_ref):
    f = pl.program_id(1)

    @pl.when(f == 0)
    def _():
        xn_ref[...] = _rms(x_ref[...], gpre_ref[...]).astype(BF16)
        acc_ref[...] = jnp.zeros_like(acc_ref)

    xn = xn_ref[...]
    gate = _dot(xn, wg_ref[...])
    up = _dot(xn, wu_ref[...])
    act = (gate * jax.nn.sigmoid(gate) * up).astype(BF16)
    acc_ref[...] += _dot(act, wo_ref[...])

    @pl.when(f == pl.num_programs(1) - 1)
    def _():
        o_ref[...] = x_ref[...] + 0.5 * _rms(acc_ref[...], gpost_ref[...])


def _ffn(x, g_pre, g_post, w_in, w_out):
    n, d = x.shape
    dff = w_out.shape[0]
    tf = 256
    nf = dff // tf
    tm = _row_tile(n, 1024)
    return pl.pallas_call(
        _ffn_body,
        grid=(n // tm, nf),
        in_specs=[
            pl.BlockSpec((tm, d), lambda i, f: (i, 0)),
            pl.BlockSpec((1, d), lambda i, f: (0, 0)),
            pl.BlockSpec((1, d), lambda i, f: (0, 0)),
            pl.BlockSpec((d, tf), lambda i, f: (0, f)),
            pl.BlockSpec((d, tf), lambda i, f: (0, nf + f)),
            pl.BlockSpec((tf, d), lambda i, f: (f, 0)),
        ],
        out_specs=pl.BlockSpec((tm, d), lambda i, f: (i, 0)),
        out_shape=jax.ShapeDtypeStruct((n, d), F32),
        scratch_shapes=[pltpu.VMEM((tm, d), BF16), pltpu.VMEM((tm, d), F32)],
        compiler_params=_params(("parallel", "arbitrary")),
        name="ffn",
    )(x, g_pre.reshape(1, d), g_post.reshape(1, d), w_in, w_in, w_out)


def _proj_body(x_ref, g_ref, wa, wb, wc, wd, oa, ob, oc, od):
    xn = _rms(x_ref[...], g_ref[...]).astype(BF16)
    for w, o in ((wa, oa), (wb, ob), (wc, oc), (wd, od)):
        o[...] = _dot(xn, w[...])


def _proj(x, g, ws):
    n, d = x.shape
    tm = _row_tile(n, 512)
    return pl.pallas_call(
        _proj_body,
        grid=(n // tm,),
        in_specs=[pl.BlockSpec((tm, d), lambda i: (i, 0)), pl.BlockSpec((1, d), lambda i: (0, 0))]
        + [pl.BlockSpec(w.shape, lambda i: (0, 0)) for w in ws],
        out_specs=[pl.BlockSpec((tm, w.shape[1]), lambda i: (i, 0)) for w in ws],
        out_shape=[jax.ShapeDtypeStruct((n, w.shape[1]), F32) for w in ws],
        compiler_params=_params(("parallel",)),
        name="mixer_in_proj",
    )(x, g.reshape(1, d), *ws)


def _merge_body(x_ref, g2_ref, g3_ref, oa, ob, oc, od, wm_ref, wb_ref, wo_ref, out_ref):
    x = x_ref[...]
    d = x.shape[1]
    h = _rms(x, g2_ref[...]).astype(BF16)
    s = None
    for n, br in enumerate((oa, ob, oc, od)):
        gate = jax.nn.sigmoid(_dot(h, wm_ref[:, n * d:(n + 1) * d]))
        term = gate * _dot(br[...].astype(BF16), wb_ref[n])
        s = term if s is None else s + term
    y = _dot(s.astype(BF16), wo_ref[...])
    out_ref[...] = x + _rms(y, g3_ref[...])


def _merge(x, g2, g3, branches, w_merge, w_branch, w_out):
    n, d = x.shape
    br = branches[0].shape[1]
    tm = _row_tile(n, 256)
    row = lambda i: (i, 0)
    fix2 = lambda i: (0, 0)
    return pl.pallas_call(
        _merge_body,
        grid=(n // tm,),
        in_specs=[pl.BlockSpec((tm, d), row), pl.BlockSpec((1, d), fix2), pl.BlockSpec((1, d), fix2)]
        + [pl.BlockSpec((tm, br), row)] * N_BRANCH
        + [pl.BlockSpec(w_merge.shape, fix2), pl.BlockSpec(w_branch.shape, lambda i: (0, 0, 0)),
           pl.BlockSpec(w_out.shape, fix2)],
        out_specs=pl.BlockSpec((tm, d), row),
        out_shape=jax.ShapeDtypeStruct((n, d), F32),
        compiler_params=_params(("parallel",)),
        name="merge",
    )(x, g2.reshape(1, d), g3.reshape(1, d), *branches, w_merge, w_branch, w_out)


def _log_sigmoid(z):
    return jnp.minimum(z, 0.0) - jnp.log1p(jnp.exp(-jnp.abs(z)))


def _lin_core(q, k, v, lf, gg, gain, s0_ref, o_ref, st_ref, qt_s, kt_s, vt_s, dec_s, oi_s, st_s,
              *, dk, dv, t_valid):
    tt, wk = q.shape
    wv = v.shape[1]
    grp = LIN_GROUP
    t_idx = pl.program_id(1)

    @pl.when(t_idx == 0)
    def _():
        st_s[...] = s0_ref[...]

    if t_valid is not None:
        tok = t_idx * tt + lax.broadcasted_iota(jnp.int32, (tt, 1), 0)
        lf = jnp.where(tok < t_valid, lf, 0.0)

    r = lax.broadcasted_iota(jnp.int32, (tt, tt), 0)
    c = lax.broadcasted_iota(jnp.int32, (tt, tt), 1)
    same = (r // grp) == (c // grp)
    tri = jnp.where(same & (c <= r), 1.0, 0.0).astype(F32)
    ones_g = jnp.where(same, 1.0, 0.0).astype(F32)
    b = jnp.dot(tri, lf, precision=HI, preferred_element_type=F32)
    bl = jnp.dot(ones_g, lf, precision=HI, preferred_element_type=F32)

    qt_s[...] = (q * jnp.exp(b)).astype(BF16)
    kt_s[...] = (k * jnp.exp(bl - b)).astype(BF16)
    vt_s[...] = v.astype(BF16)
    dec_s[...] = jnp.exp(bl)

    hk = lax.broadcasted_iota(jnp.int32, (wk, wv), 0) // dk
    hv = lax.broadcasted_iota(jnp.int32, (wk, wv), 1) // dv
    ones_hd = jnp.where(hk == hv, 1.0, 0.0).astype(BF16)

    rowm = lax.broadcasted_iota(jnp.int32, (tt, 1), 0) % grp
    od = jnp.zeros((tt, wv), F32)
    for d in range(grp):
        ks = k if d == 0 else pltpu.roll(k, d, 0)
        bs = b if d == 0 else pltpu.roll(b, d, 0)
        vs = v if d == 0 else pltpu.roll(v, d, 0)
        e = jnp.exp(jnp.minimum(b - bs, 0.0))
        z = jnp.where(rowm >= d, q * ks * e, 0.0).astype(BF16)
        od = od + _dot(z, ones_hd) * vs

    mv = lax.broadcasted_iota(jnp.int32, (wv, wk), 0) // dv
    mk = lax.broadcasted_iota(jnp.int32, (wv, wk), 1) // dk
    mbd = jnp.where(mv == mk, 1.0, 0.0).astype(F32)

    def step(i, carry):
        r0 = pl.multiple_of(i * grp, grp)
        qg = qt_s[pl.ds(r0, grp), :]
        kg = kt_s[pl.ds(r0, grp), :]
        vg = vt_s[pl.ds(r0, grp), :]
        s = st_s[...]
        oi_s[pl.ds(r0, grp), :] = _dot_nt(qg, s.astype(BF16))
        upd = lax.dot_general(vg, kg, TN_DIMS, preferred_element_type=F32)
        st_s[...] = dec_s[pl.ds(r0, 1), :] * s + mbd * upd
        return carry

    lax.fori_loop(0, tt // grp, step, 0)

    o = oi_s[...] + od
    pv = lax.broadcasted_iota(jnp.int32, (wv, wv), 0) // dv
    pw = lax.broadcasted_iota(jnp.int32, (wv, wv), 1) // dv
    avg = jnp.where(pv == pw, 1.0 / dv, 0.0).astype(F32)
    ms = jnp.dot(o * o, avg, precision=HI, preferred_element_type=F32)
    o_ref[...] = o * lax.rsqrt(ms + NORM_EPS) * gain * (gg * jax.nn.sigmoid(gg))

    @pl.when(t_idx == pl.num_programs(1) - 1)
    def _():
        st_ref[...] = st_s[...]


def _hgrn_body(y_ref, la_ref, lc_ref, gain_ref, s0_ref, o_ref, st_ref, *scratch, w, dk, t_valid):
    y = y_ref[...]
    q = y[:, 0:w] * (dk ** -0.5)
    z = y[:, w:2 * w]
    v = y[:, 2 * w:3 * w]
    gg = y[:, 3 * w:4 * w]
    a = la_ref[...] + _log_sigmoid(z)
    cc = lc_ref[...]
    lf = jnp.maximum(a, cc) + jnp.log1p(jnp.exp(-jnp.abs(a - cc)))
    k = jnp.exp(la_ref[...] + _log_sigmoid(-z))
    _lin_core(q, k, v, lf, gg, gain_ref[...], s0_ref, o_ref, st_ref, *scratch, dk=dk, dv=dk, t_valid=t_valid)


def _gla_body(y_ref, w2_ref, b2_ref, gain_ref, s0_ref, o_ref, st_ref, *scratch, wk, wv, dk, dv, t_valid):
    y = y_ref[...]
    q = y[:, 0:wk] * (dk ** -0.5)
    k = y[:, wk:2 * wk]
    v = y[:, 2 * wk:2 * wk + wv]
    lr = y[:, 2 * wk + wv:2 * wk + wv + LANE]
    gg = y[:, 2 * wk + wv + LANE:2 * wk + 2 * wv + LANE]
    u = jnp.dot(lr, w2_ref[...], precision=HI, preferred_element_type=F32) + b2_ref[...]
    lf = _log_sigmoid(u) * (1.0 / GLA_TAU)
    _lin_core(q, k, v, lf, gg, gain_ref[...], s0_ref, o_ref, st_ref, *scratch, dk=dk, dv=dv, t_valid=t_valid)


def _lin_call(body, y, bsz, t, tt, vecs, s0, wk, wv):
    nt = t // tt
    cw = y.shape[1]
    return pl.pallas_call(
        body,
        grid=(bsz, nt),
        in_specs=[pl.BlockSpec((tt, cw), lambda b, i: (b * nt + i, 0))]
        + [pl.BlockSpec(a.shape, lambda b, i: (0, 0)) for a in vecs]
        + [pl.BlockSpec((None, wv, wk), lambda b, i: (b, 0, 0))],
        out_specs=[pl.BlockSpec((tt, wv), lambda b, i: (b * nt + i, 0)),
                   pl.BlockSpec((None, wv, wk), lambda b, i: (b, 0, 0))],
        out_shape=[jax.ShapeDtypeStruct((bsz * t, wv), F32), jax.ShapeDtypeStruct((bsz, wv, wk), F32)],
        scratch_shapes=[pltpu.VMEM((tt, wk), BF16), pltpu.VMEM((tt, wk), BF16), pltpu.VMEM((tt, wv), BF16),
                        pltpu.VMEM((tt, wk), F32), pltpu.VMEM((tt, wv), F32), pltpu.VMEM((wv, wk), F32)],
        compiler_params=_params(("parallel", "arbitrary")),
        name="gated_linear",
    )(y, *vecs, s0)


def _state_to_bd(state):
    bsz, nh, dk, dv = state.shape
    eye = jnp.eye(nh, dtype=state.dtype)
    return jnp.einsum('bhkv,hg->bhvgk', state, eye).reshape(bsz, nh * dv, nh * dk)


def _bd_to_state(st, nh):
    bsz, wv, wk = st.shape
    dv, dk = wv // nh, wk // nh
    blocks = st.reshape(bsz, nh, dv, nh, dk)
    idx = jnp.arange(nh)
    return blocks[:, idx, :, idx, :].transpose(1, 0, 3, 2)


def _pad_group(y, bsz, t):
    tp = -(-t // LIN_GROUP) * LIN_GROUP
    if tp == t:
        return y, tp
    y3 = jnp.pad(y.reshape(bsz, t, -1), ((0, 0), (0, tp - t), (0, 0)))
    return y3.reshape(bsz * tp, -1), tp


def _linear_mixer(body, y, bsz, t, vecs, state0, nh, dk, dv):
    yp, tp = _pad_group(y, bsz, t)
    tt = min(tp, LANE)
    wk, wv = nh * dk, nh * dv
    s0 = jnp.zeros((bsz, wv, wk), F32) if state0 is None else _state_to_bd(state0)
    body = functools.partial(body, t_valid=None if tp == t else t)
    o, st = _lin_call(body, yp, bsz, tp, tt, vecs, s0, wk, wv)
    if tp != t:
        o = o.reshape(bsz, tp, wv)[:, :t].reshape(bsz * t, wv)
    return o, _bd_to_state(st, nh)


def _online(s, v, m, l, a):
    m_new = jnp.maximum(m, jnp.max(s, axis=1, keepdims=True))
    m_safe = jnp.where(m_new == -jnp.inf, 0.0, m_new)
    p = jnp.exp(s - m_safe)
    alpha = jnp.exp(m - m_safe)
    l = alpha * l + jnp.sum(p, axis=1, keepdims=True)
    a = alpha * a + _dot(p.astype(BF16), v)
    return m_new, l, a


def _finish(l, a):
    return a / jnp.where(l > 0, l, 1.0)


def _softmax_init(rows, width):
    return (jnp.full((rows, 1), -jnp.inf, F32), jnp.zeros((rows, 1), F32), jnp.zeros((rows, width), F32))


def _diff_prompt_body(sl_ref, lam_ref, q_ref, k_ref, v_ref, gain_ref, o_ref, *, tq, tk, cd, lam_init):
    h = pl.program_id(1)
    i = pl.program_id(2)
    slope = sl_ref[h]
    lam = lam_ref[0]
    q = q_ref[...] * (cd ** -0.5)
    lane = lax.broadcasted_iota(jnp.int32, q.shape, 1)
    q0 = jnp.where(lane < cd, q, 0.0).astype(BF16)
    q1 = jnp.where(lane >= cd, q, 0.0).astype(BF16)
    rc = (lax.broadcasted_iota(jnp.int32, (tq, tk), 0) - lax.broadcasted_iota(jnp.int32, (tq, tk), 1))

    def body(j, carry):
        c0, c1 = carry
        k0 = pl.multiple_of(j * tk, tk)
        kt = k_ref[pl.ds(k0, tk), :]
        vt = v_ref[pl.ds(k0, tk), :]
        dist = rc + (i * tq - j * tk)
        bias = jnp.where(dist >= 0, dist.astype(F32) * (-slope), -jnp.inf)
        c0 = _online(_dot_nt(q0, kt) + bias, vt, *c0)
        c1 = _online(_dot_nt(q1, kt) + bias, vt, *c1)
        return c0, c1

    init = _softmax_init(tq, 2 * cd)
    n_kv = (i * tq + tq - 1) // tk + 1
    (m0, l0, a0), (m1, l1, a1) = lax.fori_loop(0, n_kv, body, (init, init))
    o = _finish(l0, a0) - lam * _finish(l1, a1)
    o_ref[...] = _rms(o, gain_ref[...]) * (1.0 - lam_init)


def _diff_prompt(q, k, v, gain, lam, slopes, lam_init):
    bsz, nh, t, w = q.shape
    tq = tk = min(256, t)
    smem = pl.BlockSpec(memory_space=pltpu.SMEM)
    return pl.pallas_call(
        functools.partial(_diff_prompt_body, tq=tq, tk=tk, cd=w // 2, lam_init=lam_init),
        grid=(bsz, nh, t // tq),
        in_specs=[smem, smem,
                  pl.BlockSpec((None, None, tq, w), lambda b, h, i: (b, h, i, 0)),
                  pl.BlockSpec((None, None, t, w), lambda b, h, i: (b, h, 0, 0)),
                  pl.BlockSpec((None, None, t, w), lambda b, h, i: (b, h, 0, 0)),
                  pl.BlockSpec((1, w), lambda b, h, i: (0, 0))],
        out_specs=pl.BlockSpec((None, None, tq, w), lambda b, h, i: (b, h, i, 0)),
        out_shape=jax.ShapeDtypeStruct((bsz, nh, t, w), F32),
        compiler_params=_params(("parallel", "parallel", "arbitrary")),
        name="diff_attn_prompt",
    )(slopes, lam, q, k, v, gain.reshape(1, w))


def _compress(xk_ref, xv_ref, w_ref, nb):
    def body(j, acc):
        ak, av = acc
        xk = xk_ref[pl.ds(j, nb, stride=NSA_BLOCK), :].astype(BF16)
        xv = xv_ref[pl.ds(j, nb, stride=NSA_BLOCK), :].astype(BF16)
        return ak + _dot(xk, w_ref[0, j]), av + _dot(xv, w_ref[1, j])

    zero = jnp.zeros((nb, xk_ref.shape[1]), F32)
    ak, av = lax.fori_loop(0, NSA_BLOCK, body, (zero, zero))
    return jnp.concatenate([ak, av], axis=1)


def _compress_body(xk_ref, xv_ref, w_ref, c_ref, o_ref, *, nb, hd):
    acc = _compress(xk_ref, xv_ref, w_ref, nb) + c_ref[...]
    for p in range(4):
        o_ref[p] = acc[:, p * hd:(p + 1) * hd]


def _compress_prompt(yb, bsz, t, w_bd, cconst, hd):
    nb = t // NSA_BLOCK
    cw = 4 * hd
    hw = cw // 2
    return pl.pallas_call(
        functools.partial(_compress_body, nb=nb, hd=hd),
        grid=(bsz,),
        in_specs=[pl.BlockSpec((t, hw), lambda b: (b, 2)),
                  pl.BlockSpec((t, hw), lambda b: (b, 3)),
                  pl.BlockSpec(w_bd.shape, lambda b: (0, 0, 0, 0)),
                  pl.BlockSpec((1, cw), lambda b: (0, 0))],
        out_specs=pl.BlockSpec((None, 4, nb, hd), lambda b: (b, 0, 0, 0)),
        out_shape=jax.ShapeDtypeStruct((bsz, 4, nb, hd), F32),
        compiler_params=_params(("parallel",)),
        name="nsa_compress_prompt",
    )(yb, yb, w_bd, cconst)


def _topk_mask(score, blk, k_sel):
    nb = score.shape[1]
    rank = jnp.zeros(score.shape, F32)
    for i in range(nb):
        col = score[:, i:i + 1]
        ahead = jnp.where(col > score, 1.0, jnp.where(col == score, jnp.where(blk > i, 1.0, 0.0), 0.0))
        rank = rank + ahead
    return rank


def _nsa_prompt_body(sl_ref, q_ref, kc_ref, vc_ref, ks_ref, vs_ref, kw_ref, vw_ref, gl_ref, gb_ref, o_ref,
                     *, tq, nb, hd):
    g = pl.program_id(1)
    i = pl.program_id(2)
    scale = hd ** -0.5
    qs = [(q_ref[n] * scale).astype(BF16) for n in range(B_GROUP)]
    slopes = [sl_ref[g * B_GROUP + n] for n in range(B_GROUP)]
    qpos = i * tq + lax.broadcasted_iota(jnp.int32, (tq, 1), 0)

    blk = lax.broadcasted_iota(jnp.int32, (1, nb), 1)
    distc = qpos - (blk * NSA_BLOCK + (NSA_BLOCK - 1))
    kc = kc_ref[...].astype(BF16)
    vc = vc_ref[...].astype(BF16)
    o_cmp = []
    imp = None
    for n in range(B_GROUP):
        s = jnp.where(distc >= 0, _dot_nt(qs[n], kc) - slopes[n] * distc.astype(F32), -jnp.inf)
        m = jnp.max(s, axis=1, keepdims=True)
        e = jnp.exp(s - jnp.where(m == -jnp.inf, 0.0, m))
        den = jnp.sum(e, axis=1, keepdims=True)
        p = e / jnp.where(den > 0, den, 1.0)
        o_cmp.append(_dot(p.astype(BF16), vc))
        imp = p if imp is None else imp + p

    cur = qpos // NSA_BLOCK
    forced = (blk == 0) | (blk == cur) | (blk == cur - 1)
    score = jnp.where(blk > cur, -jnp.inf, jnp.where(forced, jnp.inf, imp))
    rank = _topk_mask(score, blk, NSA_TOPK)
    selmask = jnp.where(rank < min(NSA_TOPK, nb), 1.0, 0.0).astype(BF16)

    tk = tq
    rc = (lax.broadcasted_iota(jnp.int32, (tq, tk), 0) - lax.broadcasted_iota(jnp.int32, (tq, tk), 1))
    eb = lax.broadcasted_iota(jnp.int32, (nb, tk), 0)
    es = lax.broadcasted_iota(jnp.int32, (nb, tk), 1)

    def attend(j, carry, k_ref, v_ref, mask_fn):
        k0 = pl.multiple_of(j * tk, tk)
        kt = k_ref[pl.ds(k0, tk), :]
        vt = v_ref[pl.ds(k0, tk), :]
        dist = rc + (i * tq - j * tk)
        ok = mask_fn(j, dist)
        distf = dist.astype(F32)
        out = []
        for n in range(B_GROUP):
            s = jnp.where(ok, _dot_nt(qs[n], kt) - slopes[n] * distf, -jnp.inf)
            out.append(_online(s, vt, *carry[n]))
        return tuple(out)

    def sel_mask(j, dist):
        expand = jnp.where((es + j * tk) // NSA_BLOCK == eb, 1.0, 0.0).astype(BF16)
        return (_dot(selmask, expand) > 0.5) & (dist >= 0)

    def win_mask(j, dist):
        return (dist >= 0) & (dist < NSA_WINDOW)

    init = tuple(_softmax_init(tq, hd) for _ in range(B_GROUP))
    c_sel = lax.fori_loop(0, i + 1, lambda j, c: attend(j, c, ks_ref, vs_ref, sel_mask), init)
    j_lo = jnp.maximum(i - (NSA_WINDOW + tk - 1) // tk, 0)
    c_win = lax.fori_loop(j_lo, i + 1, lambda j, c: attend(j, c, kw_ref, vw_ref, win_mask), init)

    gates = jax.nn.sigmoid(gl_ref[...] + gb_ref[...])
    glane = lax.broadcasted_iota(jnp.int32, gates.shape, 1)
    for n in range(B_GROUP):
        base = (g * B_GROUP + n) * 3
        gc = [jnp.sum(jnp.where(glane == base + c, gates, 0.0), axis=1, keepdims=True) for c in range(3)]
        o_ref[n] = (gc[0] * o_cmp[n] + gc[1] * _finish(c_sel[n][1], c_sel[n][2])
                    + gc[2] * _finish(c_win[n][1], c_win[n][2]))


def _nsa_prompt(q, cmp_kv, kv_t, gate_logits, gate_b, slopes):
    bsz, ng, _, t, hd = q.shape
    nb = cmp_kv.shape[2]
    tq = min(256, t)
    nq = t // tq
    kv_spec = lambda c: pl.BlockSpec((None, None, None, t, hd), lambda b, g, i: (b, c, g, 0, 0))
    return pl.pallas_call(
        functools.partial(_nsa_prompt_body, tq=tq, nb=nb, hd=hd),
        grid=(bsz, ng, nq),
        in_specs=[pl.BlockSpec(memory_space=pltpu.SMEM),
                  pl.BlockSpec((None, None, B_GROUP, tq, hd), lambda b, g, i: (b, g, 0, i, 0)),
                  pl.BlockSpec((None, None, nb, hd), lambda b, g, i: (b, g, 0, 0)),
                  pl.BlockSpec((None, None, nb, hd), lambda b, g, i: (b, B_KV_HEADS + g, 0, 0)),
                  kv_spec(2), kv_spec(3), kv_spec(4), kv_spec(5),
                  pl.BlockSpec((tq, LANE), lambda b, g, i: (b * nq + i, gate_logits.shape[1] // LANE - 1)),
                  pl.BlockSpec((1, LANE), lambda b, g, i: (0, 0))],
        out_specs=pl.BlockSpec((None, None, B_GROUP, tq, hd), lambda b, g, i: (b, g, 0, i, 0)),
        out_shape=jax.ShapeDtypeStruct((bsz, ng, B_GROUP, t, hd), F32),
        compiler_params=_params(("parallel", "parallel", "arbitrary")),
        name="nsa_attn_prompt",
    )(slopes, q, cmp_kv, cmp_kv, kv_t, kv_t, kv_t, kv_t, gate_logits, gate_b)


def _diff_decode_body(pt_ref, lam_ref, q_ref, new_ref, gain_ref, *refs, n_pp, page, past, nh, cd, lam_init, dec_t):
    pages = refs[:n_pp]
    o_ref = refs[n_pp]
    m_s, l_s, a_s = refs[n_pp + 1:]
    c = pl.program_id(1)
    w = 2 * cd * nh
    rows = 2 * dec_t * nh
    row = lax.broadcasted_iota(jnp.int32, (rows, 1), 0)
    rh = row % nh
    rt = (row % (dec_t * nh)) // nh
    slope = jnp.zeros((rows, 1), F32)
    for h in range(nh):
        slope = jnp.where(rh == h, 2.0 ** (-2 * (h + 1)), slope)
    qb = (q_ref[...] * (cd ** -0.5)).astype(BF16)

    @pl.when(c == 0)
    def _():
        new = new_ref[...]
        kn = new[:, :w].astype(BF16)
        vn = new[:, w:].astype(BF16)
        col = lax.broadcasted_iota(jnp.int32, (1, new.shape[0]), 1)
        dist = rt - col
        s = jnp.where((dist >= 0) & (col < dec_t), _dot_nt(qb, kn) - slope * dist.astype(F32), -jnp.inf)
        m, l, a = _online(s, vn, *_softmax_init(rows, w))
        m_s[...] = m
        l_s[...] = l
        a_s[...] = a

    carry = (m_s[...], l_s[...], a_s[...])
    col = lax.broadcasted_iota(jnp.int32, (1, page), 1)
    for p in range(n_pp):
        pg = pages[p][...]
        dist = (past + rt) - ((c * n_pp + p) * page + col)
        s = _dot_nt(qb, pg[:, :w].astype(BF16)) - slope * dist.astype(F32)
        carry = _online(s, pg[:, w:].astype(BF16), *carry)
    m_s[...], l_s[...], a_s[...] = carry

    @pl.when(c == pl.num_programs(1) - 1)
    def _():
        full = _finish(carry[1], carry[2])
        own = jnp.zeros((rows, 2 * cd), F32)
        for h in range(nh):
            own = jnp.where(rh == h, full[:, h * 2 * cd:(h + 1) * 2 * cd], own)
        half = rows // 2
        o = own[:half] - lam_ref[0] * own[half:]
        o_ref[...] = _rms(o, gain_ref[...]) * (1.0 - lam_init)


def _diff_decode(q_rows, new_kv, cache, layer, page_table, gain, lam, lam_init, dec_t, past):
    bsz, rows, w = q_rows.shape
    n_pages = page_table.shape[1]
    page = cache.shape[2]
    n_pp = math.gcd(16, n_pages)
    nh = C_HEADS
    cd = w // (2 * nh)
    pt_flat = page_table.reshape(-1)

    def page_spec(p):
        return pl.BlockSpec((None, None, page, 2 * w),
                            lambda b, c, pt: (layer, pt[b * n_pages + c * n_pp + p], 0, 0))

    grid_spec = pltpu.PrefetchScalarGridSpec(
        num_scalar_prefetch=1,
        grid=(bsz, n_pages // n_pp),
        in_specs=[pl.BlockSpec(memory_space=pltpu.SMEM),
                  pl.BlockSpec((None, rows, w), lambda b, c, pt: (b, 0, 0)),
                  pl.BlockSpec((None, NEW_PAD, 2 * w), lambda b, c, pt: (b, 0, 0)),
                  pl.BlockSpec((1, 2 * cd), lambda b, c, pt: (0, 0))]
        + [page_spec(p) for p in range(n_pp)],
        out_specs=pl.BlockSpec((None, rows // 2, 2 * cd), lambda b, c, pt: (b, 0, 0)),
        scratch_shapes=[pltpu.VMEM((rows, 1), F32), pltpu.VMEM((rows, 1), F32), pltpu.VMEM((rows, w), F32)],
    )
    return pl.pallas_call(
        functools.partial(_diff_decode_body, n_pp=n_pp, page=page, past=past, nh=nh, cd=cd,
                          lam_init=lam_init, dec_t=dec_t),
        grid_spec=grid_spec,
        out_shape=jax.ShapeDtypeStruct((bsz, rows // 2, 2 * cd), F32),
        compiler_params=_params(("parallel", "arbitrary")),
        name="diff_attn_decode",
    )(pt_flat, lam, q_rows, new_kv, gain.reshape(1, 2 * cd), *([cache] * n_pp))


def _two_part_softmax(s1, v1, s2, v2):
    m = jnp.maximum(jnp.max(s1, axis=1, keepdims=True), jnp.max(s2, axis=1, keepdims=True))
    m = jnp.where(m == -jnp.inf, 0.0, m)
    e1 = jnp.exp(s1 - m)
    e2 = jnp.exp(s2 - m)
    den = jnp.sum(e1, axis=1, keepdims=True) + jnp.sum(e2, axis=1, keepdims=True)
    return (_dot(e1.astype(BF16), v1) + _dot(e2.astype(BF16), v2)) / jnp.where(den > 0, den, 1.0)


def _nsa_decode1_body(pt_ref, q_ref, wcache_ref, wnew_ref, wbd_ref, cc_ref, *refs,
                      n_pp, page, past, hd, dec_t, nc, k_past):
    pages = refs[:n_pp]
    ocmp_ref, owin_ref, ids_ref = refs[n_pp:n_pp + 3]
    xk_s, xv_s = refs[n_pp + 3:]
    c = pl.program_id(1)
    hw = 2 * hd
    for p in range(n_pp):
        r0 = pl.multiple_of((c * n_pp + p) * page, page)
        xk_s[pl.ds(r0, page), :] = pages[p][:, :hw]
        xv_s[pl.ds(r0, page), :] = pages[p][:, hw:]

    @pl.when(c == pl.num_programs(1) - 1)
    def _():
        cw = 4 * hd
        rows = B_GROUP * dec_t * B_KV_HEADS
        row = lax.broadcasted_iota(jnp.int32, (rows, 1), 0)
        rg = row % B_KV_HEADS
        rt = (row // B_KV_HEADS) % dec_t
        rn = row // (B_KV_HEADS * dec_t)
        slope = jnp.zeros((rows, 1), F32)
        for g in range(B_KV_HEADS):
            for n in range(B_GROUP):
                slope = jnp.where((rg == g) & (rn == n), 2.0 ** (-(2 * (g * B_GROUP + n) + 1)), slope)
        qb = (q_ref[...] * (hd ** -0.5)).astype(BF16)

        def own_v(full):
            out = jnp.zeros((rows, hd), F32)
            for g in range(B_KV_HEADS):
                out = jnp.where(rg == g, full[:, (B_KV_HEADS + g) * hd:(B_KV_HEADS + g + 1) * hd], out)
            return out

        kc = (_compress(xk_s, xv_s, wbd_ref, nc) + cc_ref[...]).astype(BF16)
        blk = lax.broadcasted_iota(jnp.int32, (1, nc), 1)
        dist = (past + rt) - (blk * NSA_BLOCK + (NSA_BLOCK - 1))
        s = jnp.where(dist >= 0, _dot_nt(qb, kc) - slope * dist.astype(F32), -jnp.inf)
        m = jnp.max(s, axis=1, keepdims=True)
        e = jnp.exp(s - jnp.where(m == -jnp.inf, 0.0, m))
        den = jnp.sum(e, axis=1, keepdims=True)
        p = e / jnp.where(den > 0, den, 1.0)
        ocmp_ref[...] = own_v(_dot(p.astype(BF16), kc))

        half = rows // B_GROUP
        imp = p[:half] + p[half:]
        forced = (blk == 0) | (blk == nc - 1)
        score = jnp.where(forced, jnp.inf, imp)
        rank = _topk_mask(score, blk, k_past)
        lane = lax.broadcasted_iota(jnp.int32, (half, LANE), 1)
        ids = jnp.zeros((half, LANE), jnp.int32)
        blk_f = blk.astype(F32)
        for r in range(k_past):
            idx = jnp.sum(jnp.where(rank == r, blk_f, 0.0), axis=1, keepdims=True).astype(jnp.int32)
            ids = jnp.where(lane == r, idx, ids)
        ids_ref[...] = ids

        wc = wcache_ref[...].astype(BF16)
        wn = wnew_ref[...].astype(BF16)
        nwin = wc.shape[0]
        idx = lax.broadcasted_iota(jnp.int32, (1, nwin), 1)
        d1 = (nwin + rt) - idx
        s1 = jnp.where((d1 >= 0) & (d1 < NSA_WINDOW), _dot_nt(qb, wc) - slope * d1.astype(F32), -jnp.inf)
        col = lax.broadcasted_iota(jnp.int32, (1, wn.shape[0]), 1)
        d2 = rt - col
        s2 = jnp.where((d2 >= 0) & (col < dec_t), _dot_nt(qb, wn) - slope * d2.astype(F32), -jnp.inf)
        owin_ref[...] = own_v(_two_part_softmax(s1, wc, s2, wn))


def _nsa_decode1(q_rows, win_cache, win_new, cache, layer, page_table, w_bd, cconst, dec_t, past, k_past):
    bsz, rows, cw = q_rows.shape
    hd = cw // 4
    n_pages = page_table.shape[1]
    page = cache.shape[2]
    n_pp = math.gcd(16, n_pages)
    nc = past // NSA_BLOCK
    nwin = win_cache.shape[2]
    half = rows // B_GROUP

    def page_spec(p):
        return pl.BlockSpec((None, None, page, cw),
                            lambda b, c, pt: (layer, pt[b * n_pages + c * n_pp + p], 0, 0))

    grid_spec = pltpu.PrefetchScalarGridSpec(
        num_scalar_prefetch=1,
        grid=(bsz, n_pages // n_pp),
        in_specs=[pl.BlockSpec((None, rows, cw), lambda b, c, pt: (b, 0, 0)),
                  pl.BlockSpec((None, None, nwin, cw), lambda b, c, pt: (layer, b, 0, 0)),
                  pl.BlockSpec((None, NEW_PAD, cw), lambda b, c, pt: (b, 0, 0)),
                  pl.BlockSpec(w_bd.shape, lambda b, c, pt: (0, 0, 0, 0)),
                  pl.BlockSpec((1, cw), lambda b, c, pt: (0, 0))]
        + [page_spec(p) for p in range(n_pp)],
        out_specs=[pl.BlockSpec((None, rows, hd), lambda b, c, pt: (b, 0, 0)),
                   pl.BlockSpec((None, rows, hd), lambda b, c, pt: (b, 0, 0)),
                   pl.BlockSpec((None, half, LANE), lambda b, c, pt: (b, 0, 0))],
        scratch_shapes=[pltpu.VMEM((past, cw // 2), F32), pltpu.VMEM((past, cw // 2), F32)],
    )
    return pl.pallas_call(
        functools.partial(_nsa_decode1_body, n_pp=n_pp, page=page, past=past, hd=hd, dec_t=dec_t, nc=nc,
                          k_past=k_past),
        grid_spec=grid_spec,
        out_shape=[jax.ShapeDtypeStruct((bsz, rows, hd), F32), jax.ShapeDtypeStruct((bsz, rows, hd), F32),
                   jax.ShapeDtypeStruct((bsz, half, LANE), jnp.int32)],
        compiler_params=_params(("parallel", "arbitrary")),
        name="nsa_decode_compress_window",
    )(page_table.reshape(-1), q_rows, win_cache, win_new, w_bd, cconst, *([cache] * n_pp))


def _nsa_decode2_body(pt_ref, ids_ref, q_ref, new_ref, *refs, k_past, past, hd, dec_t):
    blocks = refs[:k_past]
    o_ref = refs[k_past]
    b = pl.program_id(0)
    g = pl.program_id(1)
    t = pl.program_id(2)
    cw = 4 * hd
    rows = q_ref.shape[0]
    row = lax.broadcasted_iota(jnp.int32, (rows, 1), 0)
    slope = jnp.zeros((rows, 1), F32)
    for n in range(B_GROUP):
        s_n = jnp.where(g == 0, 2.0 ** (-(2 * n + 1)), 2.0 ** (-(2 * (B_GROUP + n) + 1)))
        slope = jnp.where(row == n, s_n, slope)
    qb = (q_ref[...] * (hd ** -0.5)).astype(BF16)

    new = new_ref[...].astype(BF16)
    col = lax.broadcasted_iota(jnp.int32, (1, new.shape[0]), 1)
    d = t - col
    s = jnp.where((d >= 0) & (col < dec_t), _dot_nt(qb, new) - slope * d.astype(F32), -jnp.inf)
    carry = _online(s, new, *_softmax_init(rows, cw))
    base = ((b * dec_t + t) * B_KV_HEADS + g) * k_past
    pos = lax.broadcasted_iota(jnp.int32, (1, NSA_BLOCK), 1)
    for r in range(k_past):
        blk = blocks[r][...].astype(BF16)
        dist = (past + t) - (ids_ref[base + r] * NSA_BLOCK + pos)
        s = _dot_nt(qb, blk) - slope * dist.astype(F32)
        carry = _online(s, blk, *carry)
    full = _finish(carry[1], carry[2])
    lo = (B_KV_HEADS) * hd
    o_ref[...] = jnp.where(g == 0, full[:, lo:lo + hd], full[:, lo + hd:lo + 2 * hd])


def _nsa_decode2(q_rows, sel_new, ids, cache, layer, page_table, dec_t, past, k_past):
    bsz = q_rows.shape[0]
    cw = q_rows.shape[-1]
    hd = cw // 4
    n_pages = page_table.shape[1]
    page = cache.shape[2]
    per_page = page // NSA_BLOCK

    def blk_spec(r):
        def imap(b, g, t, pt, ids):
            bid = ids[((b * dec_t + t) * B_KV_HEADS + g) * k_past + r]
            return (layer, pt[b * n_pages + bid // per_page], bid % per_page, 1)
        return pl.BlockSpec((None, None, NSA_BLOCK, cw), imap)

    grid_spec = pltpu.PrefetchScalarGridSpec(
        num_scalar_prefetch=2,
        grid=(bsz, B_KV_HEADS, dec_t),
        in_specs=[pl.BlockSpec((None, None, None, NEW_PAD, cw), lambda b, g, t, pt, ids: (b, t, g, 0, 0)),
                  pl.BlockSpec((None, NEW_PAD, cw), lambda b, g, t, pt, ids: (b, 0, 0))]
        + [blk_spec(r) for r in range(k_past)],
        out_specs=pl.BlockSpec((None, None, None, NEW_PAD, hd), lambda b, g, t, pt, ids: (b, t, g, 0, 0)),
    )
    return pl.pallas_call(
        functools.partial(_nsa_decode2_body, k_past=k_past, past=past, hd=hd, dec_t=dec_t),
        grid_spec=grid_spec,
        out_shape=jax.ShapeDtypeStruct((bsz, dec_t, B_KV_HEADS, NEW_PAD, hd), F32),
        compiler_params=_params(("parallel", "parallel", "arbitrary")),
        name="nsa_decode_selected",
    )(page_table.reshape(-1), ids, q_rows, sel_new, *([cache] * k_past))


def _nsa_combine_body(gl_ref, gb_ref, oc_ref, os_ref, ow_ref, o_ref, *, hd):
    gates = jax.nn.sigmoid(gl_ref[...] + gb_ref[...])
    lane = lax.broadcasted_iota(jnp.int32, gates.shape, 1)
    head = lax.broadcasted_iota(jnp.int32, o_ref.shape, 1) // hd
    out = jnp.zeros(o_ref.shape, F32)
    for c, br_ref in enumerate((oc_ref, os_ref, ow_ref)):
        gmap = jnp.zeros(o_ref.shape, F32)
        for h in range(B_HEADS):
            gc = jnp.sum(jnp.where(lane == 3 * h + c, gates, 0.0), axis=1, keepdims=True)
            gmap = jnp.where(head == h, gc, gmap)
        out = out + gmap * br_ref[...]
    o_ref[...] = out


def _nsa_combine(gate_logits, gate_b, o_cmp, o_sel, o_win):
    n, w = o_cmp.shape
    full = lambda a: pl.BlockSpec(a.shape, lambda i: (0, 0))
    return pl.pallas_call(
        functools.partial(_nsa_combine_body, hd=w // B_HEADS),
        grid=(1,),
        in_specs=[pl.BlockSpec((n, LANE), lambda i: (0, gate_logits.shape[1] // LANE - 1)), full(gate_b),
                  full(o_cmp), full(o_sel), full(o_win)],
        out_specs=pl.BlockSpec((n, w), lambda i: (0, 0)),
        out_shape=jax.ShapeDtypeStruct((n, w), F32),
        name="nsa_decode_combine",
    )(gate_logits, gate_b, o_cmp, o_sel, o_win)


def _split_points(d_model):
    br = d_model // N_BRANCH
    hd_b = br // B_HEADS
    d_dk = br // (2 * D_HEADS)
    splits = (br, br, br, br, br, 6 * B_KV_HEADS * hd_b, 3 * B_HEADS, br, br, br,
              D_HEADS * d_dk, D_HEADS * d_dk, br, GLA_RANK, br, N_BRANCH * d_model)
    return [0] + [int(p) for p in np.cumsum(splits)]


def _layer_weights(l, d_model, w_in, ffn_w_in, ffn_w_out, w_branch, w_out, nsa_cmp_pe, nsa_cmp_w,
                   gla_gate_w2, gla_gate_b, nsa_gate_b):
    pts = _split_points(d_model)
    w = w_in[l]
    seg = lambda i, j: w[:, pts[i]:pts[j]]
    padl = lambda a: jnp.pad(a, ((0, 0), (0, LANE - a.shape[1])))
    wa = seg(0, 4)
    wb = jnp.concatenate([seg(4, 6), padl(seg(6, 7))], axis=1)
    wc = seg(7, 10)
    wd = jnp.concatenate([seg(10, 13), padl(seg(13, 14)), seg(14, 15)], axis=1)
    hd = nsa_cmp_w.shape[-1]
    eye = jnp.eye(B_KV_HEADS, dtype=F32)
    w_bd = jnp.einsum('cjde,gh->cjgdhe', nsa_cmp_w[l], eye).reshape(
        2, NSA_BLOCK, B_KV_HEADS * hd, B_KV_HEADS * hd)
    cvec = jnp.einsum('cjd,cjde->ce', nsa_cmp_pe[l], nsa_cmp_w[l])
    cconst = jnp.repeat(cvec, B_KV_HEADS, axis=0).reshape(1, 4 * hd)
    w2 = jnp.pad(gla_gate_w2[l], ((0, LANE - GLA_RANK), (0, 0)))
    return dict(
        proj=[a.astype(BF16) for a in (wa, wb, wc, wd)],
        w_merge=seg(15, 16).astype(BF16), w_branch=w_branch[l].astype(BF16), w_out=w_out[l].astype(BF16),
        ffn_in=ffn_w_in[l].astype(BF16), ffn_out=ffn_w_out[l].astype(BF16),
        w_bd=w_bd.astype(BF16), cconst=cconst, gla_w2=w2, gla_b=gla_gate_b[l].reshape(1, -1),
        gate_b=jnp.pad(nsa_gate_b[l].reshape(1, -1), ((0, 0), (0, LANE - 3 * B_HEADS))),
    )


def _mixer_common(x, lw, g2, vec_a, vec_d, bsz, t, state_a, state_d, br):
    a_dk = br // A_HEADS
    d_dk = br // (2 * D_HEADS)
    d_dv = br // D_HEADS
    ya, yb, yc, yd = _proj(x, g2, lw['proj'])
    o_a, st_a = _linear_mixer(functools.partial(_hgrn_body, w=br, dk=a_dk), ya, bsz, t, vec_a, state_a,
                              A_HEADS, a_dk, a_dk)
    o_d, st_d = _linear_mixer(functools.partial(_gla_body, wk=D_HEADS * d_dk, wv=br, dk=d_dk, dv=d_dv),
                              yd, bsz, t, vec_d, state_d, D_HEADS, d_dk, d_dv)
    return yb, yc, o_a, st_a, o_d, st_d


def kernel(x_prompt, x_sample, cache_nsa_kv, cache_nsa_win, cache_diff_kv, state_hgrn, state_gla, page_table,
           norm_gains, ffn_w_in, ffn_w_out, w_in, hgrn_lb_logits, hgrn_norm_gain, nsa_cmp_pe, nsa_cmp_w,
           nsa_gate_b, diff_lambda, diff_norm_gain, gla_gate_w2, gla_gate_b, gla_norm_gain, w_branch, w_out):
    bp, tp, d_model = x_prompt.shape
    bs, ts, _ = x_sample.shape
    depth = w_in.shape[0]
    br = d_model // N_BRANCH
    hd = br // B_HEADS
    cd2 = br // C_HEADS
    n_pool, page = cache_nsa_kv.shape[1], cache_nsa_kv.shape[2]
    n_pages = page_table.shape[1]
    past = n_pages * page
    nwin = cache_nsa_win.shape[2]
    assert past % NSA_BLOCK == 0 and ts <= NEW_PAD and ts <= NSA_BLOCK and tp % NSA_BLOCK == 0
    assert past // NSA_BLOCK >= NSA_TOPK and page % NSA_BLOCK == 0 and nwin == NSA_WINDOW
    k_past = NSA_TOPK - 1

    lb_cum = jnp.cumsum(jax.nn.softmax(hgrn_lb_logits.astype(F32), axis=0), axis=0)
    lower = lb_cum - lb_cum[0]
    slopes = 2.0 ** (-np.arange(1, B_HEADS + C_HEADS + 1, dtype=np.float64))
    sl_b = jnp.asarray(slopes[0::2], F32)
    sl_c = jnp.asarray(slopes[1::2], F32)

    nsa_pool = cache_nsa_kv.reshape(depth, n_pool, page, 4 * B_KV_HEADS * hd)
    diff_pool = cache_diff_kv.reshape(depth, n_pool, page, 2 * br)
    win_pool = cache_nsa_win.reshape(depth, bs, nwin, 2 * B_KV_HEADS * hd)

    xp = x_prompt.reshape(bp * tp, d_model)
    xs = x_sample.reshape(bs * ts, d_model)
    outs = {k: [] for k in ('kvp', 'kvs', 'winp', 'wins', 'dkp', 'dks', 'hp', 'hs', 'gp', 'gs')}

    for l in range(depth):
        lw = _layer_weights(l, d_model, w_in, ffn_w_in, ffn_w_out, w_branch, w_out, nsa_cmp_pe, nsa_cmp_w,
                            gla_gate_w2, gla_gate_b, nsa_gate_b)
        g = norm_gains[l]
        lb = lower[l].reshape(1, br)
        vec_a = (jnp.log1p(-lb), jnp.log(lb), hgrn_norm_gain[l].reshape(1, br))
        vec_d = (lw['gla_w2'], lw['gla_b'], gla_norm_gain[l].reshape(1, br))
        lv = diff_lambda[l].astype(F32)
        lam_init = 0.8 - 0.6 * math.exp(-0.3 * l)
        lam = (jnp.exp(jnp.sum(lv[0] * lv[1])) - jnp.exp(jnp.sum(lv[2] * lv[3])) + lam_init).reshape(1)

        xp = _ffn(xp, g[0], g[1], lw['ffn_in'][0], lw['ffn_out'][0])
        yb, yc, o_a, st_a, o_d, st_d = _mixer_common(xp, lw, g[2], vec_a, vec_d, bp, tp, None, None, br)
        outs['hp'].append(st_a)
        outs['gp'].append(st_d)
        outs['kvp'].append(yb[:, br:br + 4 * B_KV_HEADS * hd].reshape(bp, tp, 4, B_KV_HEADS, hd))
        outs['winp'].append(yb[:, br + 4 * B_KV_HEADS * hd:br + 6 * B_KV_HEADS * hd]
                            .reshape(bp, tp, 2, B_KV_HEADS, hd)[:, -min(NSA_WINDOW, tp):])
        outs['dkp'].append(yc[:, br:3 * br].reshape(bp, tp, 2, C_HEADS, cd2))

        heads = lambda a: a.reshape(bp, tp, C_HEADS, cd2).transpose(0, 2, 1, 3)
        o_c = _diff_prompt(heads(yc[:, 0:br]), heads(yc[:, br:2 * br]).astype(BF16),
                           heads(yc[:, 2 * br:3 * br]).astype(BF16), diff_norm_gain[l], lam, sl_c, lam_init)
        o_c = o_c.transpose(0, 2, 1, 3).reshape(bp * tp, br)

        cmp_kv = _compress_prompt(yb, bp, tp, lw['w_bd'], lw['cconst'], hd)
        q_b = yb[:, 0:br].reshape(bp, tp, B_KV_HEADS, B_GROUP, hd).transpose(0, 2, 3, 1, 4)
        kv_t = yb[:, br:br + 6 * B_KV_HEADS * hd].reshape(bp, tp, 6, B_KV_HEADS, hd).transpose(0, 2, 3, 1, 4)
        o_b = _nsa_prompt(q_b, cmp_kv, kv_t.astype(BF16), yb, lw['gate_b'], sl_b)
        o_b = o_b.transpose(0, 3, 1, 2, 4).reshape(bp * tp, br)

        xp = _merge(xp, g[2], g[3], (o_a, o_b, o_c, o_d), lw['w_merge'], lw['w_branch'], lw['w_out'])
        xp = _ffn(xp, g[4], g[5], lw['ffn_in'][1], lw['ffn_out'][1])

        xs = _ffn(xs, g[0], g[1], lw['ffn_in'][0], lw['ffn_out'][0])
        yb, yc, o_a, st_a, o_d, st_d = _mixer_common(xs, lw, g[2], vec_a, vec_d, bs, ts,
                                                     state_hgrn[l], state_gla[l], br)
        outs['hs'].append(st_a)
        outs['gs'].append(st_d)
        new_kv = yb[:, br:br + 6 * B_KV_HEADS * hd].reshape(bs, ts, 6, B_KV_HEADS * hd)
        outs['kvs'].append(new_kv[:, :, :4].reshape(bs, ts, 4, B_KV_HEADS, hd))
        new_win = new_kv[:, :, 4:].reshape(bs, ts, 2 * B_KV_HEADS * hd)
        outs['wins'].append(jnp.concatenate([win_pool[l], new_win], axis=1)[:, -nwin:]
                            .reshape(bs, nwin, 2, B_KV_HEADS, hd))
        outs['dks'].append(yc[:, br:3 * br].reshape(bs, ts, 2, C_HEADS, cd2))
        pad8 = lambda a: jnp.pad(a, ((0, 0), (0, NEW_PAD - ts), (0, 0)))

        cq = yc[:, 0:br].reshape(bs, ts, C_HEADS, 2, cd2 // 2)
        sel = (jnp.arange(C_HEADS)[:, None, None, None] == jnp.arange(C_HEADS)[None, None, :, None]) & \
              (jnp.arange(2)[None, :, None, None] == jnp.arange(2)[None, None, None, :])
        q_rows = jnp.einsum('bthmd,hmgn->bmthgnd', cq, sel.astype(F32)).reshape(bs, 2 * ts * C_HEADS, br)
        new_c = pad8(yc[:, br:3 * br].reshape(bs, ts, 2 * br))
        o_c = _diff_decode(q_rows, new_c, diff_pool, l, page_table, diff_norm_gain[l], lam, lam_init, ts, past)
        o_c = o_c.reshape(bs * ts, br)

        bq = yb[:, 0:br].reshape(bs, ts, B_KV_HEADS, B_GROUP, hd)
        eye_g = jnp.eye(B_KV_HEADS, dtype=F32)
        placed = jnp.einsum('btgnd,gh->btgnhd', bq, eye_g).reshape(bs, ts, B_KV_HEADS, B_GROUP, B_KV_HEADS * hd)
        placed = jnp.pad(placed, ((0, 0),) * 4 + ((0, B_KV_HEADS * hd),))
        q_rows = placed.transpose(0, 3, 1, 2, 4).reshape(bs, B_GROUP * ts * B_KV_HEADS, 4 * hd)
        win_new = pad8(new_win)
        o_cmp, o_win, ids = _nsa_decode1(q_rows, win_pool, win_new, nsa_pool, l, page_table, lw['w_bd'],
                                         lw['cconst'], ts, past, k_past)
        ids_flat = ids[:, :, :k_past].reshape(-1)
        q2 = jnp.pad(placed.reshape(bs, ts, B_KV_HEADS, B_GROUP, 4 * hd), ((0, 0),) * 3 + ((0, NEW_PAD - B_GROUP), (0, 0)))
        sel_new = pad8(new_kv[:, :, 2:4].reshape(bs, ts, 4 * hd))
        o_sel = _nsa_decode2(q2, sel_new, ids_flat, nsa_pool, l, page_table, ts, past, k_past)
        rows_to_tok = lambda a: a.reshape(bs, B_GROUP, ts, B_KV_HEADS, hd).transpose(0, 2, 3, 1, 4).reshape(bs * ts, br)
        o_sel = o_sel[:, :, :, :B_GROUP].reshape(bs * ts, br)
        o_b = _nsa_combine(yb, lw['gate_b'], rows_to_tok(o_cmp), o_sel, rows_to_tok(o_win))

        xs = _merge(xs, g[2], g[3], (o_a, o_b, o_c, o_d), lw['w_merge'], lw['w_branch'], lw['w_out'])
        xs = _ffn(xs, g[4], g[5], lw['ffn_in'][1], lw['ffn_out'][1])

    st = lambda k: jnp.stack(outs[k])
    return (xp.reshape(bp, tp, d_model), xs.reshape(bs, ts, d_model), st('kvp'), st('kvs'), st('winp'), st('wins'),
            st('dkp'), st('dks'), st('hp'), st('hs'), st('gp'), st('gs'))
```

```python
import functools
import math

import numpy as np
import jax
import jax.numpy as jnp
from jax import lax
from jax.experimental import pallas as pl
from jax.experimental.pallas import tpu as pltpu

F32 = jnp.float32
BF16 = jnp.bfloat16
HI = lax.Precision.HIGHEST

N_BRANCH = 4
A_HEADS = 4
B_HEADS = 4
B_KV_HEADS = 2
B_GROUP = 2
C_HEADS = 4
D_HEADS = 4
NSA_BLOCK = 64
NSA_TOPK = 16
NSA_WINDOW = 512
GLA_RANK = 16
GLA_TAU = 16.0
NORM_EPS = 1e-6
LANE = 128
LIN_GROUP = 16
NEW_PAD = 16
DEC_ROWS = 8
ONES_PAD = 16
PAGE_PITCH_PAD = 8
VMEM_LIMIT = 56 * 1024 * 1024
LOG2E = 1.4426950408889634
NEG_BIG = -1e30

NT_DIMS = (((1,), (1,)), ((), ()))
TN_DIMS = (((0,), (0,)), ((), ()))


def _params(semantics):
    return pltpu.CompilerParams(dimension_semantics=semantics, vmem_limit_bytes=VMEM_LIMIT)


def _rms(x, g):
    return x * lax.rsqrt(jnp.mean(x * x, axis=-1, keepdims=True) + NORM_EPS) * g


def _dot(a, b):
    return jnp.dot(a, b, preferred_element_type=F32)


def _dot_nt(a, b):
    return lax.dot_general(a, b, NT_DIMS, preferred_element_type=F32)


def _row_tile(n, cap):
    t = min(n, cap)
    while n % t or t % 8:
        t -= 1
    return t


def _ffn_body(x_ref, gpre_ref, gpost_ref, wg_ref, wu_ref, wo_ref, o_ref, xn_ref, acc_ref):
    f = pl.program_id(1)

    @pl.when(f == 0)
    def _():
        xn_ref[...] = _rms(x_ref[...], gpre_ref[...]).astype(BF16)
        acc_ref[...] = jnp.zeros_like(acc_ref)

    xn = xn_ref[...]
    gate = _dot(xn, wg_ref[...])
    up = _dot(xn, wu_ref[...])
    act = (gate * jax.nn.sigmoid(gate) * up).astype(BF16)
    acc_ref[...] += _dot(act, wo_ref[...])

    @pl.when(f == pl.num_programs(1) - 1)
    def _():
        o_ref[...] = x_ref[...] + 0.5 * _rms(acc_ref[...], gpost_ref[...])


def _ffn(x, g_pre, g_post, w_in, w_out):
    n, d = x.shape
    dff = w_out.shape[0]
    tf = 256
    nf = dff // tf
    tm = _row_tile(n, 1024)
    return pl.pallas_call(
        _ffn_body,
        grid=(n // tm, nf),
        in_specs=[
            pl.BlockSpec((tm, d), lambda i, f: (i, 0)),
            pl.BlockSpec((1, d), lambda i, f: (0, 0)),
            pl.BlockSpec((1, d), lambda i, f: (0, 0)),
            pl.BlockSpec((d, tf), lambda i, f: (0, f)),
            pl.BlockSpec((d, tf), lambda i, f: (0, nf + f)),
            pl.BlockSpec((tf, d), lambda i, f: (f, 0)),
        ],
        out_specs=pl.BlockSpec((tm, d), lambda i, f: (i, 0)),
        out_shape=jax.ShapeDtypeStruct((n, d), F32),
        scratch_shapes=[pltpu.VMEM((tm, d), BF16), pltpu.VMEM((tm, d), F32)],
        compiler_params=_params(("parallel", "arbitrary")),
        name="ffn",
    )(x, g_pre.reshape(1, d), g_post.reshape(1, d), w_in, w_in, w_out)


def _proj_body(x_ref, g_ref, wa, wb, wc, wd, oa, ob, oc, od):
    xn = _rms(x_ref[...], g_ref[...]).astype(BF16)
    for w, o in ((wa, oa), (wb, ob), (wc, oc), (wd, od)):
        o[...] = _dot(xn, w[...])


def _proj(x, g, ws):
    n, d = x.shape
    tm = _row_tile(n, 512)
    return pl.pallas_call(
        _proj_body,
        grid=(n // tm,),
        in_specs=[pl.BlockSpec((tm, d), lambda i: (i, 0)), pl.BlockSpec((1, d), lambda i: (0, 0))]
        + [pl.BlockSpec(w.shape, lambda i: (0, 0)) for w in ws],
        out_specs=[pl.BlockSpec((tm, w.shape[1]), lambda i: (i, 0)) for w in ws],
        out_shape=[jax.ShapeDtypeStruct((n, w.shape[1]), F32) for w in ws],
        compiler_params=_params(("parallel",)),
        name="mixer_in_proj",
    )(x, g.reshape(1, d), *ws)


def _merge_body(x_ref, g2_ref, g3_ref, oa, ob, oc, od, wm_ref, wb_ref, wo_ref, out_ref):
    x = x_ref[...]
    d = x.shape[1]
    h = _rms(x, g2_ref[...]).astype(BF16)
    s = None
    for n, br in enumerate((oa, ob, oc, od)):
        gate = jax.nn.sigmoid(_dot(h, wm_ref[:, n * d:(n + 1) * d]))
        term = gate * _dot(br[...].astype(BF16), wb_ref[n])
        s = term if s is None else s + term
    y = _dot(s.astype(BF16), wo_ref[...])
    out_ref[...] = x + _rms(y, g3_ref[...])


def _merge(x, g2, g3, branches, w_merge, w_branch, w_out):
    n, d = x.shape
    br = branches[0].shape[1]
    tm = _row_tile(n, 256)
    row = lambda i: (i, 0)
    fix2 = lambda i: (0, 0)
    return pl.pallas_call(
        _merge_body,
        grid=(n // tm,),
        in_specs=[pl.BlockSpec((tm, d), row), pl.BlockSpec((1, d), fix2), pl.BlockSpec((1, d), fix2)]
        + [pl.BlockSpec((tm, br), row)] * N_BRANCH
        + [pl.BlockSpec(w_merge.shape, fix2), pl.BlockSpec(w_branch.shape, lambda i: (0, 0, 0)),
           pl.BlockSpec(w_out.shape, fix2)],
        out_specs=pl.BlockSpec((tm, d), row),
        out_shape=jax.ShapeDtypeStruct((n, d), F32),
        compiler_params=_params(("parallel",)),
        name="merge",
    )(x, g2.reshape(1, d), g3.reshape(1, d), *branches, w_merge, w_branch, w_out)


def _log_sigmoid(z):
    return jnp.minimum(z, 0.0) - jnp.log1p(jnp.exp(-jnp.abs(z)))


def _lin_core(q, k, v, lf, gg, gain, s0_ref, o_ref, st_ref, qt_s, kt_s, vt_s, dec_s, oi_s, st_s,
              *, dk, dv, t_valid):
    tt, wk = q.shape
    wv = v.shape[1]
    grp = LIN_GROUP
    t_idx = pl.program_id(1)

    @pl.when(t_idx == 0)
    def _():
        st_s[...] = s0_ref[...]

    if t_valid is not None:
        tok = t_idx * tt + lax.broadcasted_iota(jnp.int32, (tt, 1), 0)
        lf = jnp.where(tok < t_valid, lf, 0.0)

    r = lax.broadcasted_iota(jnp.int32, (tt, tt), 0)
    c = lax.broadcasted_iota(jnp.int32, (tt, tt), 1)
    same = (r // grp) == (c // grp)
    tri = jnp.where(same & (c <= r), 1.0, 0.0).astype(F32)
    ones_g = jnp.where(same, 1.0, 0.0).astype(F32)
    b = jnp.dot(tri, lf, precision=HI, preferred_element_type=F32)
    bl = jnp.dot(ones_g, lf, precision=HI, preferred_element_type=F32)

    qt_s[...] = (q * jnp.exp(b)).astype(BF16)
    kt_s[...] = (k * jnp.exp(bl - b)).astype(BF16)
    vt_s[...] = v.astype(BF16)
    dec_s[...] = jnp.exp(bl)

    hk = lax.broadcasted_iota(jnp.int32, (wk, wv), 0) // dk
    hv = lax.broadcasted_iota(jnp.int32, (wk, wv), 1) // dv
    ones_hd = jnp.where(hk == hv, 1.0, 0.0).astype(BF16)

    rowm = lax.broadcasted_iota(jnp.int32, (tt, 1), 0) % grp
    od = jnp.zeros((tt, wv), F32)
    for d in range(grp):
        ks = k if d == 0 else pltpu.roll(k, d, 0)
        bs = b if d == 0 else pltpu.roll(b, d, 0)
        vs = v if d == 0 else pltpu.roll(v, d, 0)
        e = jnp.exp(jnp.minimum(b - bs, 0.0))
        z = jnp.where(rowm >= d, q * ks * e, 0.0).astype(BF16)
        od = od + _dot(z, ones_hd) * vs

    mv = lax.broadcasted_iota(jnp.int32, (wv, wk), 0) // dv
    mk = lax.broadcasted_iota(jnp.int32, (wv, wk), 1) // dk
    mbd = jnp.where(mv == mk, 1.0, 0.0).astype(F32)

    def step(i, carry):
        r0 = pl.multiple_of(i * grp, grp)
        qg = qt_s[pl.ds(r0, grp), :]
        kg = kt_s[pl.ds(r0, grp), :]
        vg = vt_s[pl.ds(r0, grp), :]
        s = st_s[...]
        oi_s[pl.ds(r0, grp), :] = _dot_nt(qg, s.astype(BF16))
        upd = lax.dot_general(vg, kg, TN_DIMS, preferred_element_type=F32)
        st_s[...] = dec_s[pl.ds(r0, 1), :] * s + mbd * upd
        return carry

    lax.fori_loop(0, tt // grp, step, 0)

    o = oi_s[...] + od
    pv = lax.broadcasted_iota(jnp.int32, (wv, wv), 0) // dv
    pw = lax.broadcasted_iota(jnp.int32, (wv, wv), 1) // dv
    avg = jnp.where(pv == pw, 1.0 / dv, 0.0).astype(F32)
    ms = jnp.dot(o * o, avg, precision=HI, preferred_element_type=F32)
    o_ref[...] = o * lax.rsqrt(ms + NORM_EPS) * gain * (gg * jax.nn.sigmoid(gg))

    @pl.when(t_idx == pl.num_programs(1) - 1)
    def _():
        st_ref[...] = st_s[...]


def _hgrn_body(y_ref, la_ref, lc_ref, gain_ref, s0_ref, o_ref, st_ref, *scratch, w, dk, t_valid):
    y = y_ref[...]
    q = y[:, 0:w] * (dk ** -0.5)
    z = y[:, w:2 * w]
    v = y[:, 2 * w:3 * w]
    gg = y[:, 3 * w:4 * w]
    a = la_ref[...] + _log_sigmoid(z)
    cc = lc_ref[...]
    lf = jnp.maximum(a, cc) + jnp.log1p(jnp.exp(-jnp.abs(a - cc)))
    k = jnp.exp(la_ref[...] + _log_sigmoid(-z))
    _lin_core(q, k, v, lf, gg, gain_ref[...], s0_ref, o_ref, st_ref, *scratch, dk=dk, dv=dk, t_valid=t_valid)


def _gla_body(y_ref, w2_ref, b2_ref, gain_ref, s0_ref, o_ref, st_ref, *scratch, wk, wv, dk, dv, t_valid):
    y = y_ref[...]
    q = y[:, 0:wk] * (dk ** -0.5)
    k = y[:, wk:2 * wk]
    v = y[:, 2 * wk:2 * wk + wv]
    lr = y[:, 2 * wk + wv:2 * wk + wv + LANE]
    gg = y[:, 2 * wk + wv + LANE:2 * wk + 2 * wv + LANE]
    u = jnp.dot(lr, w2_ref[...], precision=HI, preferred_element_type=F32) + b2_ref[...]
    lf = _log_sigmoid(u) * (1.0 / GLA_TAU)
    _lin_core(q, k, v, lf, gg, gain_ref[...], s0_ref, o_ref, st_ref, *scratch, dk=dk, dv=dv, t_valid=t_valid)


def _lin_call(body, y, bsz, t, tt, vecs, s0, wk, wv):
    nt = t // tt
    cw = y.shape[1]
    return pl.pallas_call(
        body,
        grid=(bsz, nt),
        in_specs=[pl.BlockSpec((tt, cw), lambda b, i: (b * nt + i, 0))]
        + [pl.BlockSpec(a.shape, lambda b, i: (0, 0)) for a in vecs]
        + [pl.BlockSpec((None, wv, wk), lambda b, i: (b, 0, 0))],
        out_specs=[pl.BlockSpec((tt, wv), lambda b, i: (b * nt + i, 0)),
                   pl.BlockSpec((None, wv, wk), lambda b, i: (b, 0, 0))],
        out_shape=[jax.ShapeDtypeStruct((bsz * t, wv), F32), jax.ShapeDtypeStruct((bsz, wv, wk), F32)],
        scratch_shapes=[pltpu.VMEM((tt, wk), BF16), pltpu.VMEM((tt, wk), BF16), pltpu.VMEM((tt, wv), BF16),
                        pltpu.VMEM((tt, wk), F32), pltpu.VMEM((tt, wv), F32), pltpu.VMEM((wv, wk), F32)],
        compiler_params=_params(("parallel", "arbitrary")),
        name="gated_linear",
    )(y, *vecs, s0)


def _state_to_bd(state):
    bsz, nh, dk, dv = state.shape
    eye = jnp.eye(nh, dtype=state.dtype)
    return jnp.einsum('bhkv,hg->bhvgk', state, eye).reshape(bsz, nh * dv, nh * dk)


def _bd_to_state(st, nh):
    bsz, wv, wk = st.shape
    dv, dk = wv // nh, wk // nh
    blocks = st.reshape(bsz, nh, dv, nh, dk)
    idx = jnp.arange(nh)
    return blocks[:, idx, :, idx, :].transpose(1, 0, 3, 2)


def _pad_group(y, bsz, t):
    tp = -(-t // LIN_GROUP) * LIN_GROUP
    if tp == t:
        return y, tp
    y3 = jnp.pad(y.reshape(bsz, t, -1), ((0, 0), (0, tp - t), (0, 0)))
    return y3.reshape(bsz * tp, -1), tp


def _linear_mixer(body, y, bsz, t, vecs, state0, nh, dk, dv):
    yp, tp = _pad_group(y, bsz, t)
    tt = min(tp, LANE)
    wk, wv = nh * dk, nh * dv
    s0 = jnp.zeros((bsz, wv, wk), F32) if state0 is None else _state_to_bd(state0)
    body = functools.partial(body, t_valid=None if tp == t else t)
    o, st = _lin_call(body, yp, bsz, tp, tt, vecs, s0, wk, wv)
    if tp != t:
        o = o.reshape(bsz, tp, wv)[:, :t].reshape(bsz * t, wv)
    return o, _bd_to_state(st, nh)


def _col_update(idx, tiles, m_s, acc_s):
    m_old = m_s[idx]
    m_new = m_old
    for s, _, shift in tiles:
        smax = jnp.max(s, axis=0, keepdims=True)
        m_new = jnp.maximum(m_new, smax if shift is None else smax + shift)
    m_safe = jnp.where(m_new == -jnp.inf, 0.0, m_new)
    pv = None
    for s, vt, shift in tiles:
        p = jnp.exp2(s + ((-m_safe) if shift is None else (shift - m_safe)))
        d = _dot(vt, p.astype(BF16))
        pv = d if pv is None else pv + d
    acc_s[idx] = jnp.exp2(m_old - m_safe) * acc_s[idx] + pv
    m_s[idx] = m_new


def _col_finish(acc, dv):
    den = acc[dv:dv + 1, :]
    return acc[:dv, :] / jnp.where(den > 0, den, 1.0)


def _col_reset(m_s, acc_s):
    m_s[...] = jnp.full(m_s.shape, -jnp.inf, F32)
    acc_s[...] = jnp.zeros(acc_s.shape, F32)


def _online(s, v, m, l, a, v_t=False):
    m_new = jnp.maximum(m, jnp.max(s, axis=1, keepdims=True))
    m_safe = jnp.where(m_new == -jnp.inf, 0.0, m_new)
    p = jnp.exp2(s - m_safe)
    alpha = jnp.exp2(m - m_safe)
    l = alpha * l + jnp.sum(p, axis=1, keepdims=True)
    pv = _dot_nt(p.astype(BF16), v) if v_t else _dot(p.astype(BF16), v)
    return m_new, l, alpha * a + pv


def _finish(l, a):
    return a / jnp.where(l > 0, l, 1.0)


def _softmax_init(rows, width):
    return (jnp.full((rows, 1), -jnp.inf, F32), jnp.zeros((rows, 1), F32), jnp.zeros((rows, width), F32))


def _aug_values_t(v_t, tk):
    lead = v_t.shape[:-2]
    dv, t = v_t.shape[-2:]
    pad = jnp.zeros(lead + (ONES_PAD, t), v_t.dtype).at[..., 0, :].set(1.0)
    aug = jnp.concatenate([v_t, pad], axis=-2).reshape(lead + (dv + ONES_PAD, t // tk, tk))
    return jnp.swapaxes(aug, -3, -2).astype(BF16)


def _diff_prompt_body(sl_ref, lam_ref, qt_ref, k_ref, vt_ref, gain_ref, o_ref, boff_s, m_s, acc_s,
                      *, tq, tk, cd, lam_init):
    h = pl.program_id(1)
    i = pl.program_id(2)
    w = 2 * cd
    slope2 = sl_ref[h] * LOG2E
    qt = qt_ref[...] * (cd ** -0.5 * LOG2E)
    sub = lax.broadcasted_iota(jnp.int32, qt.shape, 0)
    qs = (jnp.where(sub < cd, qt, 0.0).astype(BF16), jnp.where(sub >= cd, qt, 0.0).astype(BF16))
    rel = lax.broadcasted_iota(jnp.int32, (tk, tq), 1) - lax.broadcasted_iota(jnp.int32, (tk, tq), 0)
    boff_s[...] = rel.astype(F32) * (-slope2)
    _col_reset(m_s, acc_s)
    ratio = tq // tk

    def key_tiles(jj):
        return [(k_ref[pl.ds(pl.multiple_of((jj * ratio + r) * tk, tk), tk), :], vt_ref[jj * ratio + r])
                for r in range(ratio)]

    def off_group(jj, carry):
        kv = key_tiles(jj)
        shifts = [((i - jj) * tq - r * tk).astype(F32) * (-slope2) for r in range(ratio)]
        boff = boff_s[...]
        for mi in range(2):
            _col_update(mi, [(_dot(kv[r][0], qs[mi]) + boff, kv[r][1], shifts[r]) for r in range(ratio)],
                        m_s, acc_s)
        return carry

    lax.fori_loop(0, i, off_group, 0)
    kv = key_tiles(i)
    biases = []
    for r in range(ratio):
        dist = rel - r * tk
        biases.append(jnp.where(dist >= 0, dist.astype(F32) * (-slope2), -jnp.inf))
    for mi in range(2):
        _col_update(mi, [(_dot(kv[r][0], qs[mi]) + biases[r], kv[r][1], None) for r in range(ratio)], m_s, acc_s)

    o = _col_finish(acc_s[0], w) - lam_ref[0] * _col_finish(acc_s[1], w)
    ms = jnp.mean(o * o, axis=0, keepdims=True)
    o_ref[...] = o * lax.rsqrt(ms + NORM_EPS) * gain_ref[...] * (1.0 - lam_init)


def _diff_prompt(q_t, k, v_aug, gain, lam, slopes, lam_init, tq, tk):
    bsz, nh, w, t = q_t.shape
    nk, dva = v_aug.shape[2], v_aug.shape[3]
    smem = pl.BlockSpec(memory_space=pltpu.SMEM)
    return pl.pallas_call(
        functools.partial(_diff_prompt_body, tq=tq, tk=tk, cd=w // 2, lam_init=lam_init),
        grid=(bsz, nh, t // tq),
        in_specs=[smem, smem,
                  pl.BlockSpec((None, None, w, tq), lambda b, h, i: (b, h, 0, i)),
                  pl.BlockSpec((None, None, t, w), lambda b, h, i: (b, h, 0, 0)),
                  pl.BlockSpec((None, None, nk, dva, tk), lambda b, h, i: (b, h, 0, 0, 0)),
                  pl.BlockSpec((w, 1), lambda b, h, i: (0, 0))],
        out_specs=pl.BlockSpec((None, None, w, tq), lambda b, h, i: (b, h, 0, i)),
        out_shape=jax.ShapeDtypeStruct((bsz, nh, w, t), F32),
        scratch_shapes=[pltpu.VMEM((tk, tq), F32), pltpu.VMEM((2, 1, tq), F32), pltpu.VMEM((2, dva, tq), F32)],
        compiler_params=_params(("parallel", "parallel", "arbitrary")),
        name="diff_attn_prompt",
    )(slopes, lam, q_t, k, v_aug, gain.reshape(w, 1))


def _compress(xk_ref, xv_ref, w_ref, nb):
    def body(j, acc):
        ak, av = acc
        xk = xk_ref[pl.ds(j, nb, stride=NSA_BLOCK), :].astype(BF16)
        xv = xv_ref[pl.ds(j, nb, stride=NSA_BLOCK), :].astype(BF16)
        return ak + _dot(xk, w_ref[0, j]), av + _dot(xv, w_ref[1, j])

    zero = jnp.zeros((nb, xk_ref.shape[1]), F32)
    ak, av = lax.fori_loop(0, NSA_BLOCK, body, (zero, zero))
    return jnp.concatenate([ak, av], axis=1)


def _compress_body(xk_ref, xv_ref, w_ref, c_ref, o_ref, *, nb, hd):
    acc = _compress(xk_ref, xv_ref, w_ref, nb) + c_ref[...]
    for p in range(4):
        o_ref[p] = acc[:, p * hd:(p + 1) * hd]


def _compress_prompt(yb, bsz, t, w_bd, cconst, hd):
    nb = t // NSA_BLOCK
    cw = 4 * hd
    hw = cw // 2
    return pl.pallas_call(
        functools.partial(_compress_body, nb=nb, hd=hd),
        grid=(bsz,),
        in_specs=[pl.BlockSpec((t, hw), lambda b: (b, 2)),
                  pl.BlockSpec((t, hw), lambda b: (b, 3)),
                  pl.BlockSpec(w_bd.shape, lambda b: (0, 0, 0, 0)),
                  pl.BlockSpec((1, cw), lambda b: (0, 0))],
        out_specs=pl.BlockSpec((None, 4, nb, hd), lambda b: (b, 0, 0, 0)),
        out_shape=jax.ShapeDtypeStruct((bsz, 4, nb, hd), F32),
        compiler_params=_params(("parallel",)),
        name="nsa_compress_prompt",
    )(yb, yb, w_bd, cconst)


def _nsa_prompt_body(sl_ref, qt_ref, kc_ref, vc_ref, ks_ref, vst_ref, kw_ref, vwt_ref, gl_ref, gb_ref, o_ref,
                     boff_s, m_s, acc_s, *, tq, tk, nb, hd, k_sel):
    g = pl.program_id(1)
    i = pl.program_id(2)
    qs = [(qt_ref[n] * (hd ** -0.5 * LOG2E)).astype(BF16) for n in range(B_GROUP)]
    sl2 = [sl_ref[g * B_GROUP + n] * LOG2E for n in range(B_GROUP)]
    qpos = i * tq + lax.broadcasted_iota(jnp.int32, (1, tq), 1)

    blk = lax.broadcasted_iota(jnp.int32, (nb, 1), 0)
    distc = qpos - (blk * NSA_BLOCK + (NSA_BLOCK - 1))
    distc_f = distc.astype(F32)
    kc = kc_ref[...].astype(BF16)
    vc = vc_ref[...].astype(BF16)
    o_cmp = []
    imp = None
    for n in range(B_GROUP):
        s = jnp.where(distc >= 0, _dot(kc, qs[n]) - sl2[n] * distc_f, -jnp.inf)
        m = jnp.max(s, axis=0, keepdims=True)
        e = jnp.exp2(s - jnp.where(m == -jnp.inf, 0.0, m))
        den = jnp.sum(e, axis=0, keepdims=True)
        p = e / jnp.where(den > 0, den, 1.0)
        o_cmp.append(lax.dot_general(vc, p.astype(BF16), TN_DIMS, preferred_element_type=F32))
        imp = p if imp is None else imp + p

    cur = qpos // NSA_BLOCK
    forced = (blk == 0) | (blk == cur) | (blk == cur - 1)
    score = jnp.where(blk > cur, -jnp.inf, jnp.where(forced, jnp.inf, imp))
    sub = 8
    slabs = [score[v * sub:(v + 1) * sub, :] for v in range(nb // sub)]
    ranks = [jnp.zeros((sub, tq), F32) for _ in slabs]
    sub_i = lax.broadcasted_iota(jnp.int32, (sub, 1), 0)
    for ib in range(nb):
        row = score[ib:ib + 1, :]
        for v, slab in enumerate(slabs):
            ge = jnp.where(row >= slab, 1.0, 0.0)
            gt = jnp.where(row > slab, 1.0, 0.0)
            if ib < v * sub:
                ahead = ge
            elif ib >= (v + 1) * sub:
                ahead = gt
            else:
                ahead = jnp.where(sub_i > ib - v * sub, ge, gt)
            ranks[v] = ranks[v] + ahead
    rank = jnp.concatenate(ranks, axis=0)
    selneg = jnp.where(rank < k_sel, 0.0, NEG_BIG).astype(BF16)

    rel = lax.broadcasted_iota(jnp.int32, (tk, tq), 1) - lax.broadcasted_iota(jnp.int32, (tk, tq), 0)
    rel_f = rel.astype(F32)
    for n in range(B_GROUP):
        boff_s[n] = rel_f * (-sl2[n])
    ratio = tq // tk
    erow = lax.broadcasted_iota(jnp.int32, (tk, nb), 0)
    eblk = lax.broadcasted_iota(jnp.int32, (tk, nb), 1)

    def block_mask(j):
        expand = jnp.where((erow + j * tk) // NSA_BLOCK == eblk, 1.0, 0.0).astype(BF16)
        return _dot(expand, selneg)

    _col_reset(m_s, acc_s)

    def key_tiles(k_ref, vt_ref, jj):
        return [(k_ref[pl.ds(pl.multiple_of((jj * ratio + r) * tk, tk), tk), :], vt_ref[jj * ratio + r])
                for r in range(ratio)]

    def sel_off(jj, carry):
        kv = key_tiles(ks_ref, vst_ref, jj)
        mbs = [block_mask(jj * ratio + r) for r in range(ratio)]
        for n in range(B_GROUP):
            boff = boff_s[n]
            _col_update(n, [(_dot(kv[r][0], qs[n]) + boff + mbs[r], kv[r][1],
                             ((i - jj) * tq - r * tk).astype(F32) * (-sl2[n])) for r in range(ratio)], m_s, acc_s)
        return carry

    lax.fori_loop(0, i, sel_off, 0)
    kv = key_tiles(ks_ref, vst_ref, i)
    mbs = [block_mask(i * ratio + r) for r in range(ratio)]
    dists = [rel - r * tk for r in range(ratio)]
    for n in range(B_GROUP):
        _col_update(n, [(_dot(kv[r][0], qs[n]) + mbs[r]
                         + jnp.where(dists[r] >= 0, dists[r].astype(F32) * (-sl2[n]), -jnp.inf), kv[r][1], None)
                        for r in range(ratio)], m_s, acc_s)
    o_sel = [_col_finish(acc_s[n], hd) for n in range(B_GROUP)]

    _col_reset(m_s, acc_s)
    for ds in range(-(NSA_WINDOW // tq), 1):
        def win_span(ds=ds):
            kv = key_tiles(kw_ref, vwt_ref, i + ds)
            dists = [rel - (ds * tq + r * tk) for r in range(ratio)]
            for n in range(B_GROUP):
                _col_update(n, [(_dot(kv[r][0], qs[n])
                                 + jnp.where((dists[r] >= 0) & (dists[r] < NSA_WINDOW),
                                             dists[r].astype(F32) * (-sl2[n]), -jnp.inf), kv[r][1], None)
                                for r in range(ratio)], m_s, acc_s)

        if ds < 0:
            pl.when(i + ds >= 0)(win_span)
        else:
            win_span()
    o_win = [_col_finish(acc_s[n], hd) for n in range(B_GROUP)]

    gates = jax.nn.sigmoid(gl_ref[...] + gb_ref[...])
    for n in range(B_GROUP):
        o_ref[n] = (gates[3 * n:3 * n + 1] * o_cmp[n] + gates[3 * n + 1:3 * n + 2] * o_sel[n]
                    + gates[3 * n + 2:3 * n + 3] * o_win[n])


def _nsa_prompt(q_t, cmp_kv, k_sel, vt_sel, k_win, vt_win, gate_t, gate_b, slopes, tq, tk):
    bsz, ng, _, hd, t = q_t.shape
    nb = cmp_kv.shape[2]
    nk, dva = vt_sel.shape[2], vt_sel.shape[3]
    k_spec = pl.BlockSpec((None, None, t, hd), lambda b, g, i: (b, g, 0, 0))
    v_spec = pl.BlockSpec((None, None, nk, dva, tk), lambda b, g, i: (b, g, 0, 0, 0))
    return pl.pallas_call(
        functools.partial(_nsa_prompt_body, tq=tq, tk=tk, nb=nb, hd=hd, k_sel=min(NSA_TOPK, nb)),
        grid=(bsz, ng, t // tq),
        in_specs=[pl.BlockSpec(memory_space=pltpu.SMEM),
                  pl.BlockSpec((None, None, B_GROUP, hd, tq), lambda b, g, i: (b, g, 0, 0, i)),
                  pl.BlockSpec((None, None, nb, hd), lambda b, g, i: (b, g, 0, 0)),
                  pl.BlockSpec((None, None, nb, hd), lambda b, g, i: (b, B_KV_HEADS + g, 0, 0)),
                  k_spec, v_spec, k_spec, v_spec,
                  pl.BlockSpec((None, None, 8, tq), lambda b, g, i: (b, g, 0, i)),
                  pl.BlockSpec((None, 8, 1), lambda b, g, i: (g, 0, 0))],
        out_specs=pl.BlockSpec((None, None, B_GROUP, hd, tq), lambda b, g, i: (b, g, 0, 0, i)),
        out_shape=jax.ShapeDtypeStruct((bsz, ng, B_GROUP, hd, t), F32),
        scratch_shapes=[pltpu.VMEM((B_GROUP, tk, tq), F32), pltpu.VMEM((B_GROUP, 1, tq), F32),
                        pltpu.VMEM((B_GROUP, dva, tq), F32)],
        compiler_params=_params(("parallel", "parallel", "arbitrary")),
        name="nsa_attn_prompt",
    )(slopes, q_t, cmp_kv, cmp_kv, k_sel, vt_sel, k_win, vt_win, gate_t, gate_b)


def _diff_decode_body(pt_ref, lam_ref, q_ref, new_ref, gain_ref, *refs, n_pp, page, past, nh, cd, lam_init, dec_t):
    pages = refs[:n_pp]
    o_ref = refs[n_pp]
    m_s, l_s, a_s = refs[n_pp + 1:]
    c = pl.program_id(1)
    w = 2 * cd * nh
    rows = 2 * dec_t * nh
    row = lax.broadcasted_iota(jnp.int32, (rows, 1), 0)
    rh = row % nh
    rt = (row % (dec_t * nh)) // nh
    slope2 = jnp.zeros((rows, 1), F32)
    for h in range(nh):
        slope2 = jnp.where(rh == h, 2.0 ** (-2 * (h + 1)) * LOG2E, slope2)
    qb = (q_ref[...] * (cd ** -0.5 * LOG2E)).astype(BF16)

    @pl.when(c == 0)
    def _():
        new = new_ref[...]
        kn = new[:, :w].astype(BF16)
        vn = new[:, w:].astype(BF16)
        col = lax.broadcasted_iota(jnp.int32, (1, new.shape[0]), 1)
        dist = rt - col
        s = jnp.where((dist >= 0) & (col < dec_t), _dot_nt(qb, kn) - slope2 * dist.astype(F32), -jnp.inf)
        m_s[...], l_s[...], a_s[...] = _online(s, vn, *_softmax_init(rows, w))

    kt = jnp.concatenate([pages[p][0:w, :].astype(BF16) for p in range(n_pp)], axis=1)
    vt = jnp.concatenate([pages[p][w:2 * w, :].astype(BF16) for p in range(n_pp)], axis=1)
    col = lax.broadcasted_iota(jnp.int32, (1, n_pp * page), 1)
    dist = (past + rt) - (c * (n_pp * page) + col)
    s = _dot(qb, kt) - slope2 * dist.astype(F32)
    m, l, a = _online(s, vt, m_s[...], l_s[...], a_s[...], v_t=True)
    m_s[...], l_s[...], a_s[...] = m, l, a

    @pl.when(c == pl.num_programs(1) - 1)
    def _():
        full = _finish(l, a)
        own = jnp.zeros((rows, 2 * cd), F32)
        for h in range(nh):
            own = jnp.where(rh == h, full[:, h * 2 * cd:(h + 1) * 2 * cd], own)
        half = rows // 2
        o = own[:half] - lam_ref[0] * own[half:]
        o_ref[...] = _rms(o, gain_ref[...]) * (1.0 - lam_init)


def _diff_decode(q_rows, new_kv, cache_t, layer, page_table, gain, lam, lam_init, dec_t, past):
    bsz, rows, w = q_rows.shape
    n_pages = page_table.shape[1]
    page = cache_t.shape[3]
    n_pp = math.gcd(16, n_pages)
    nh = C_HEADS
    cd = w // (2 * nh)

    def page_spec(p):
        return pl.BlockSpec((None, None, 2 * w, page),
                            lambda b, c, pt: (layer, pt[b * n_pages + c * n_pp + p], 0, 0))

    grid_spec = pltpu.PrefetchScalarGridSpec(
        num_scalar_prefetch=1,
        grid=(bsz, n_pages // n_pp),
        in_specs=[pl.BlockSpec(memory_space=pltpu.SMEM),
                  pl.BlockSpec((None, rows, w), lambda b, c, pt: (b, 0, 0)),
                  pl.BlockSpec((None, NEW_PAD, 2 * w), lambda b, c, pt: (b, 0, 0)),
                  pl.BlockSpec((1, 2 * cd), lambda b, c, pt: (0, 0))]
        + [page_spec(p) for p in range(n_pp)],
        out_specs=pl.BlockSpec((None, rows // 2, 2 * cd), lambda b, c, pt: (b, 0, 0)),
        scratch_shapes=[pltpu.VMEM((rows, 1), F32), pltpu.VMEM((rows, 1), F32), pltpu.VMEM((rows, w), F32)],
    )
    return pl.pallas_call(
        functools.partial(_diff_decode_body, n_pp=n_pp, page=page, past=past, nh=nh, cd=cd,
                          lam_init=lam_init, dec_t=dec_t),
        grid_spec=grid_spec,
        out_shape=jax.ShapeDtypeStruct((bsz, rows // 2, 2 * cd), F32),
        compiler_params=_params(("parallel", "arbitrary")),
        name="diff_attn_decode",
    )(page_table.reshape(-1), lam, q_rows, new_kv, gain.reshape(1, 2 * cd), *([cache_t] * n_pp))


def _nsa_decode_body(pt_ref, qh_ref, qd_ref, wc_ref, wn_ref, sn_ref, wbd_ref, cc_ref, gl_ref, gb_ref, *refs,
                     n_pp, page, past, hd, dec_t, n_pages, k_past, pitch):
    pages = refs[:n_pp]
    o_ref = refs[n_pp]
    x_s, ak_s, av_s, mtok_s, ocw_s, m_s, l_s, a_s = refs[n_pp + 1:]
    ph = pl.program_id(1)
    c = pl.program_id(2)
    last = pl.num_programs(2) - 1
    tp = DEC_ROWS
    rows = B_GROUP * tp
    row = lax.broadcasted_iota(jnp.int32, (rows, 1), 0)
    rt = row % tp
    scale2 = hd ** -0.5 * LOG2E
    qpos = past + rt

    def slope2(g):
        return jnp.where(row < tp, 2.0 ** (-(2 * (g * B_GROUP) + 1)) * LOG2E,
                         2.0 ** (-(2 * (g * B_GROUP + 1) + 1)) * LOG2E)

    def new_scores(qd, kn, sl):
        col = lax.broadcasted_iota(jnp.int32, (1, kn.shape[0]), 1)
        d = rt - col
        return jnp.where((d >= 0) & (col < dec_t), _dot_nt(qd, kn) - sl * d.astype(F32), -jnp.inf)

    @pl.when(ph == 0)
    def _():
        for p in range(n_pp):
            r0 = pl.multiple_of((c * n_pp + p) * pitch, 8)
            x_s[pl.ds(r0, 4 * hd), :] = pages[p][...]

    @pl.when((ph == 0) & (c == last))
    def _():
        unroll = 4

        def compress(cc):
            def body(du, acc):
                for u in range(unroll):
                    d = du * unroll + u
                    lhs = jnp.concatenate(
                        [x_s[pl.ds(cc * 2 * hd + g * hd + d, n_pages, stride=pitch), :]
                         for g in range(B_KV_HEADS)], axis=0).astype(BF16)
                    acc = acc + _dot(lhs, wbd_ref[cc, d])
                return acc

            return lax.fori_loop(0, hd // unroll, body, jnp.zeros((B_KV_HEADS * n_pages, 2 * hd), F32))

        ak_s[...] = compress(0)
        av_s[...] = compress(1)
        pgi = lax.broadcasted_iota(jnp.int32, (1, n_pages), 1)
        lane = lax.broadcasted_iota(jnp.int32, (1, page), 1)
        for g in range(B_KV_HEADS):
            sl = slope2(g)
            kc = (ak_s[g * n_pages:(g + 1) * n_pages, :] + cc_ref[0]).astype(BF16)
            vc = (av_s[g * n_pages:(g + 1) * n_pages, :] + cc_ref[1]).astype(BF16)
            ss = []
            for hf in range(2):
                qh = (qh_ref[g, hf] * scale2).astype(BF16)
                dist = qpos - ((2 * pgi + hf) * NSA_BLOCK + (NSA_BLOCK - 1))
                ss.append(jnp.where(dist >= 0, _dot_nt(qh, kc) - sl * dist.astype(F32), -jnp.inf))
            m = jnp.maximum(jnp.max(ss[0], axis=1, keepdims=True), jnp.max(ss[1], axis=1, keepdims=True))
            m = jnp.where(m == -jnp.inf, 0.0, m)
            es = [jnp.exp2(s - m) for s in ss]
            den = jnp.sum(es[0], axis=1, keepdims=True) + jnp.sum(es[1], axis=1, keepdims=True)
            den = jnp.where(den > 0, den, 1.0)
            ps = [e / den for e in es]
            full = [_dot(p.astype(BF16), vc) for p in ps]
            ocw_s[g, 0] = full[0][:, :hd] + full[1][:, hd:]

            imps = [p[:tp] + p[tp:] for p in ps]
            sc = [jnp.where(pgi == 0, jnp.inf, imps[0]), jnp.where(pgi == n_pages - 1, jnp.inf, imps[1])]
            rank = [jnp.zeros((tp, n_pages), F32), jnp.zeros((tp, n_pages), F32)]
            for pg in range(n_pages):
                for hf in range(2):
                    colv = sc[hf][:, pg:pg + 1]
                    for h2 in range(2):
                        tie = jnp.where(2 * pgi + h2 > 2 * pg + hf, 1.0, 0.0)
                        rank[h2] = rank[h2] + jnp.where(colv > sc[h2], 1.0, jnp.where(colv == sc[h2], tie, 0.0))
            sel = [jnp.where(r < k_past, 1.0, 0.0) for r in rank]
            sel = [jnp.concatenate([s_, s_], axis=0) for s_ in sel]
            for pg in range(n_pages):
                on = jnp.where(lane < NSA_BLOCK, sel[0][:, pg:pg + 1], sel[1][:, pg:pg + 1])
                dist = qpos - (pg * page + lane)
                lo = (pg % n_pp) * page
                mtok_s[g, pg // n_pp, :, lo:lo + page] = jnp.where(on > 0.5, dist.astype(F32) * (-sl), NEG_BIG)

            qd = (qd_ref[g] * scale2).astype(BF16)
            wkt = wc_ref[g * hd:(g + 1) * hd, :].astype(BF16)
            wvt = wc_ref[(B_KV_HEADS + g) * hd:(B_KV_HEADS + g + 1) * hd, :].astype(BF16)
            nwin = wkt.shape[1]
            d1 = (nwin + rt) - lax.broadcasted_iota(jnp.int32, (1, nwin), 1)
            s1 = jnp.where((d1 >= 0) & (d1 < NSA_WINDOW), _dot(qd, wkt) - sl * d1.astype(F32), -jnp.inf)
            wn = wn_ref[...]
            kn = wn[:, g * hd:(g + 1) * hd].astype(BF16)
            vn = wn[:, (B_KV_HEADS + g) * hd:(B_KV_HEADS + g + 1) * hd].astype(BF16)
            s2 = new_scores(qd, kn, sl)
            m = jnp.maximum(jnp.max(s1, axis=1, keepdims=True), jnp.max(s2, axis=1, keepdims=True))
            m = jnp.where(m == -jnp.inf, 0.0, m)
            e1 = jnp.exp2(s1 - m)
            e2 = jnp.exp2(s2 - m)
            den = jnp.sum(e1, axis=1, keepdims=True) + jnp.sum(e2, axis=1, keepdims=True)
            ocw_s[g, 1] = (_dot_nt(e1.astype(BF16), wvt) + _dot(e2.astype(BF16), vn)) / jnp.where(den > 0, den, 1.0)

    @pl.when(ph == 1)
    def _():
        for g in range(B_KV_HEADS):
            sl = slope2(g)
            qd = (qd_ref[g] * scale2).astype(BF16)

            @pl.when(c == 0)
            def _(g=g, sl=sl, qd=qd):
                sn = sn_ref[...]
                kn = sn[:, g * hd:(g + 1) * hd].astype(BF16)
                vn = sn[:, (B_KV_HEADS + g) * hd:(B_KV_HEADS + g + 1) * hd].astype(BF16)
                m_s[g], l_s[g], a_s[g] = _online(new_scores(qd, kn, sl), vn, *_softmax_init(rows, hd))

            kt = jnp.concatenate([pages[p][g * hd:(g + 1) * hd, :].astype(BF16) for p in range(n_pp)], axis=1)
            vt = jnp.concatenate([pages[p][(B_KV_HEADS + g) * hd:(B_KV_HEADS + g + 1) * hd, :].astype(BF16)
                                  for p in range(n_pp)], axis=1)
            s = _dot(qd, kt) + mtok_s[g, c]
            m_s[g], l_s[g], a_s[g] = _online(s, vt, m_s[g], l_s[g], a_s[g], v_t=True)

        @pl.when(c == last)
        def _():
            gates = jax.nn.sigmoid(gl_ref[...] + gb_ref[...])
            for g in range(B_KV_HEADS):
                o_sel = _finish(l_s[g], a_s[g])
                o_cmp = ocw_s[g, 0]
                o_win = ocw_s[g, 1]
                for n in range(B_GROUP):
                    r = slice(n * tp, (n + 1) * tp)
                    base = (g * B_GROUP + n) * 3
                    o_ref[g, n] = (gates[:, base:base + 1] * o_cmp[r] + gates[:, base + 1:base + 2] * o_sel[r]
                                   + gates[:, base + 2:base + 3] * o_win[r])


def _nsa_decode(q_half, q_plain, win_t, win_new, sel_new, cache_t, layer, page_table, w_bdt, cconst2,
                gate_logits, gate_b, dec_t, past, k_past):
    bsz = q_plain.shape[0]
    hd = q_plain.shape[-1]
    rows = q_plain.shape[2]
    n_pages = page_table.shape[1]
    page = cache_t.shape[3]
    nwin = win_t.shape[3]
    n_pp = math.gcd(16, n_pages)
    n_ch = n_pages // n_pp
    pitch = 4 * hd + PAGE_PITCH_PAD

    def page_spec(p):
        return pl.BlockSpec((None, None, 4 * hd, page),
                            lambda b, ph, c, pt: (layer, pt[b * n_pages + c * n_pp + p], ph, 0))

    fix = lambda *shape: pl.BlockSpec(shape, lambda b, ph, c, pt: (0,) * len(shape))
    grid_spec = pltpu.PrefetchScalarGridSpec(
        num_scalar_prefetch=1,
        grid=(bsz, 2, n_ch),
        in_specs=[pl.BlockSpec((None, B_KV_HEADS, 2, rows, 2 * hd), lambda b, ph, c, pt: (b, 0, 0, 0, 0)),
                  pl.BlockSpec((None, B_KV_HEADS, rows, hd), lambda b, ph, c, pt: (b, 0, 0, 0)),
                  pl.BlockSpec((None, None, 4 * hd, nwin), lambda b, ph, c, pt: (layer, b, 0, 0)),
                  pl.BlockSpec((None, NEW_PAD, 4 * hd), lambda b, ph, c, pt: (b, 0, 0)),
                  pl.BlockSpec((None, NEW_PAD, 4 * hd), lambda b, ph, c, pt: (b, 0, 0)),
                  fix(*w_bdt.shape), fix(*cconst2.shape),
                  pl.BlockSpec((None, DEC_ROWS, LANE), lambda b, ph, c, pt: (b, 0, 0)),
                  fix(1, LANE)]
        + [page_spec(p) for p in range(n_pp)],
        out_specs=pl.BlockSpec((None, B_KV_HEADS, B_GROUP, DEC_ROWS, hd), lambda b, ph, c, pt: (b, 0, 0, 0, 0)),
        scratch_shapes=[pltpu.VMEM((n_pages * pitch, page), F32),
                        pltpu.VMEM((B_KV_HEADS * n_pages, 2 * hd), F32),
                        pltpu.VMEM((B_KV_HEADS * n_pages, 2 * hd), F32),
                        pltpu.VMEM((B_KV_HEADS, n_ch, rows, n_pp * page), F32),
                        pltpu.VMEM((B_KV_HEADS, 2, rows, hd), F32),
                        pltpu.VMEM((B_KV_HEADS, rows, 1), F32), pltpu.VMEM((B_KV_HEADS, rows, 1), F32),
                        pltpu.VMEM((B_KV_HEADS, rows, hd), F32)],
    )
    return pl.pallas_call(
        functools.partial(_nsa_decode_body, n_pp=n_pp, page=page, past=past, hd=hd, dec_t=dec_t,
                          n_pages=n_pages, k_past=k_past, pitch=pitch),
        grid_spec=grid_spec,
        out_shape=jax.ShapeDtypeStruct((bsz, B_KV_HEADS, B_GROUP, DEC_ROWS, hd), F32),
        compiler_params=_params(("parallel", "arbitrary", "arbitrary")),
        name="nsa_decode",
    )(page_table.reshape(-1), q_half, q_plain, win_t, win_new, sel_new, w_bdt, cconst2, gate_logits, gate_b,
      *([cache_t] * n_pp))


def _split_points(d_model):
    br = d_model // N_BRANCH
    hd_b = br // B_HEADS
    d_dk = br // (2 * D_HEADS)
    splits = (br, br, br, br, br, 6 * B_KV_HEADS * hd_b, 3 * B_HEADS, br, br, br,
              D_HEADS * d_dk, D_HEADS * d_dk, br, GLA_RANK, br, N_BRANCH * d_model)
    return [0] + [int(p) for p in np.cumsum(splits)]


def _layer_weights(l, d_model, w_in, ffn_w_in, ffn_w_out, w_branch, w_out, nsa_cmp_pe, nsa_cmp_w,
                   gla_gate_w2, gla_gate_b, nsa_gate_b):
    pts = _split_points(d_model)
    w = w_in[l]
    seg = lambda i, j: w[:, pts[i]:pts[j]]
    padl = lambda a: jnp.pad(a, ((0, 0), (0, LANE - a.shape[1])))
    wa = seg(0, 4)
    wb = jnp.concatenate([seg(4, 6), padl(seg(6, 7))], axis=1)
    wc = seg(7, 10)
    wd = jnp.concatenate([seg(10, 13), padl(seg(13, 14)), seg(14, 15)], axis=1)
    hd = nsa_cmp_w.shape[-1]
    eye = jnp.eye(B_KV_HEADS, dtype=F32)
    w_bd = jnp.einsum('cjde,gh->cjgdhe', nsa_cmp_w[l], eye).reshape(
        2, NSA_BLOCK, B_KV_HEADS * hd, B_KV_HEADS * hd)
    w_bdt = jnp.einsum('cjde,ab->cdajbe', nsa_cmp_w[l], eye).reshape(
        2, hd, 2 * NSA_BLOCK, 2 * hd)
    cvec = jnp.einsum('cjd,cjde->ce', nsa_cmp_pe[l], nsa_cmp_w[l])
    cconst = jnp.repeat(cvec, B_KV_HEADS, axis=0).reshape(1, 4 * hd)
    cconst2 = jnp.concatenate([cvec, cvec], axis=1).reshape(2, 1, 2 * hd)
    w2 = jnp.pad(gla_gate_w2[l], ((0, LANE - GLA_RANK), (0, 0)))
    return dict(
        proj=[a.astype(BF16) for a in (wa, wb, wc, wd)],
        w_merge=seg(15, 16).astype(BF16), w_branch=w_branch[l].astype(BF16), w_out=w_out[l].astype(BF16),
        ffn_in=ffn_w_in[l].astype(BF16), ffn_out=ffn_w_out[l].astype(BF16),
        w_bd=w_bd.astype(BF16), w_bdt=w_bdt.astype(BF16), cconst=cconst, cconst2=cconst2,
        gla_w2=w2, gla_b=gla_gate_b[l].reshape(1, -1),
        gate_b=jnp.pad(nsa_gate_b[l].reshape(1, -1), ((0, 0), (0, LANE - 3 * B_HEADS))),
        gate_b_t=jnp.pad(nsa_gate_b[l].reshape(B_KV_HEADS, 3 * B_GROUP, 1), ((0, 0), (0, 8 - 3 * B_GROUP), (0, 0))),
    )


def _mixer_common(x, lw, g2, vec_a, vec_d, bsz, t, state_a, state_d, br):
    a_dk = br // A_HEADS
    d_dk = br // (2 * D_HEADS)
    d_dv = br // D_HEADS
    ya, yb, yc, yd = _proj(x, g2, lw['proj'])
    o_a, st_a = _linear_mixer(functools.partial(_hgrn_body, w=br, dk=a_dk), ya, bsz, t, vec_a, state_a,
                              A_HEADS, a_dk, a_dk)
    o_d, st_d = _linear_mixer(functools.partial(_gla_body, wk=D_HEADS * d_dk, wv=br, dk=d_dk, dv=d_dv),
                              yd, bsz, t, vec_d, state_d, D_HEADS, d_dk, d_dv)
    return yb, yc, o_a, st_a, o_d, st_d


def _feature_major(cache):
    l, n, t = cache.shape[:3]
    return jnp.transpose(cache, (0, 1, 3, 4, 5, 2)).reshape(l, n, -1, t)


def kernel(x_prompt, x_sample, cache_nsa_kv, cache_nsa_win, cache_diff_kv, state_hgrn, state_gla, page_table,
           norm_gains, ffn_w_in, ffn_w_out, w_in, hgrn_lb_logits, hgrn_norm_gain, nsa_cmp_pe, nsa_cmp_w,
           nsa_gate_b, diff_lambda, diff_norm_gain, gla_gate_w2, gla_gate_b, gla_norm_gain, w_branch, w_out):
    bp, tp, d_model = x_prompt.shape
    bs, ts, _ = x_sample.shape
    depth = w_in.shape[0]
    br = d_model // N_BRANCH
    hd = br // B_HEADS
    cd2 = br // C_HEADS
    page = cache_nsa_kv.shape[2]
    n_pages = page_table.shape[1]
    past = n_pages * page
    nwin = cache_nsa_win.shape[2]
    assert past % NSA_BLOCK == 0 and ts <= DEC_ROWS and tp % NSA_BLOCK == 0
    assert past // NSA_BLOCK >= NSA_TOPK and page == 2 * NSA_BLOCK and nwin == NSA_WINDOW
    k_past = NSA_TOPK - 1
    tq = min(256, tp)
    tk = min(128, tq)

    lb_cum = jnp.cumsum(jax.nn.softmax(hgrn_lb_logits.astype(F32), axis=0), axis=0)
    lower = lb_cum - lb_cum[0]
    slopes = 2.0 ** (-np.arange(1, B_HEADS + C_HEADS + 1, dtype=np.float64))
    sl_b = jnp.asarray(slopes[0::2], F32)
    sl_c = jnp.asarray(slopes[1::2], F32)

    nsa_pool_t = _feature_major(cache_nsa_kv)
    diff_pool_t = _feature_major(cache_diff_kv)
    win_pool_t = _feature_major(cache_nsa_win)

    xp = x_prompt.reshape(bp * tp, d_model)
    xs = x_sample.reshape(bs * ts, d_model)
    outs = {k: [] for k in ('kvp', 'kvs', 'winp', 'wins', 'dkp', 'dks', 'hp', 'hs', 'gp', 'gs')}

    for l in range(depth):
        lw = _layer_weights(l, d_model, w_in, ffn_w_in, ffn_w_out, w_branch, w_out, nsa_cmp_pe, nsa_cmp_w,
                            gla_gate_w2, gla_gate_b, nsa_gate_b)
        g = norm_gains[l]
        lb = lower[l].reshape(1, br)
        vec_a = (jnp.log1p(-lb), jnp.log(lb), hgrn_norm_gain[l].reshape(1, br))
        vec_d = (lw['gla_w2'], lw['gla_b'], gla_norm_gain[l].reshape(1, br))
        lv = diff_lambda[l].astype(F32)
        lam_init = 0.8 - 0.6 * math.exp(-0.3 * l)
        lam = (jnp.exp(jnp.sum(lv[0] * lv[1])) - jnp.exp(jnp.sum(lv[2] * lv[3])) + lam_init).reshape(1)

        xp = _ffn(xp, g[0], g[1], lw['ffn_in'][0], lw['ffn_out'][0])
        yb, yc, o_a, st_a, o_d, st_d = _mixer_common(xp, lw, g[2], vec_a, vec_d, bp, tp, None, None, br)
        outs['hp'].append(st_a)
        outs['gp'].append(st_d)
        kv6 = yb[:, br:br + 6 * B_KV_HEADS * hd].reshape(bp, tp, 6, B_KV_HEADS, hd)
        outs['kvp'].append(kv6[:, :, :4])
        outs['winp'].append(kv6[:, -min(NSA_WINDOW, tp):, 4:])
        outs['dkp'].append(yc[:, br:3 * br].reshape(bp, tp, 2, C_HEADS, cd2))

        c3 = yc.reshape(bp, tp, 3, C_HEADS, cd2)
        o_c = _diff_prompt(c3[:, :, 0].transpose(0, 2, 3, 1), c3[:, :, 1].transpose(0, 2, 1, 3).astype(BF16),
                           _aug_values_t(c3[:, :, 2].transpose(0, 2, 3, 1), tk),
                           diff_norm_gain[l], lam, sl_c, lam_init, tq, tk)
        o_c = o_c.transpose(0, 3, 1, 2).reshape(bp * tp, br)

        cmp_kv = _compress_prompt(yb, bp, tp, lw['w_bd'], lw['cconst'], hd)
        q_t = yb[:, 0:br].reshape(bp, tp, B_KV_HEADS, B_GROUP, hd).transpose(0, 2, 3, 4, 1)
        keys = lambda c: kv6[:, :, c].transpose(0, 2, 1, 3).astype(BF16)
        vals = lambda c: _aug_values_t(kv6[:, :, c].transpose(0, 2, 3, 1), tk)
        gate_t = yb[:, br + 6 * B_KV_HEADS * hd:br + 6 * B_KV_HEADS * hd + 3 * B_HEADS]
        gate_t = jnp.pad(gate_t.reshape(bp, tp, B_KV_HEADS, 3 * B_GROUP).transpose(0, 2, 3, 1),
                         ((0, 0), (0, 0), (0, 8 - 3 * B_GROUP), (0, 0)))
        o_b = _nsa_prompt(q_t, cmp_kv, keys(2), vals(3), keys(4), vals(5), gate_t, lw['gate_b_t'], sl_b, tq, tk)
        o_b = o_b.transpose(0, 4, 1, 2, 3).reshape(bp * tp, br)

        xp = _merge(xp, g[2], g[3], (o_a, o_b, o_c, o_d), lw['w_merge'], lw['w_branch'], lw['w_out'])
        xp = _ffn(xp, g[4], g[5], lw['ffn_in'][1], lw['ffn_out'][1])

        xs = _ffn(xs, g[0], g[1], lw['ffn_in'][0], lw['ffn_out'][0])
        yb, yc, o_a, st_a, o_d, st_d = _mixer_common(xs, lw, g[2], vec_a, vec_d, bs, ts,
                                                     state_hgrn[l], state_gla[l], br)
        outs['hs'].append(st_a)
        outs['gs'].append(st_d)
        new_kv = yb[:, br:br + 6 * B_KV_HEADS * hd].reshape(bs, ts, 6, B_KV_HEADS * hd)
        outs['kvs'].append(new_kv[:, :, :4].reshape(bs, ts, 4, B_KV_HEADS, hd))
        new_win = new_kv[:, :, 4:].reshape(bs, ts, 2 * B_KV_HEADS * hd)
        outs['wins'].append(jnp.concatenate(
            [cache_nsa_win[l][:, ts:], new_win.reshape(bs, ts, 2, B_KV_HEADS, hd)], axis=1))
        outs['dks'].append(yc[:, br:3 * br].reshape(bs, ts, 2, C_HEADS, cd2))
        pad_new = lambda a: jnp.pad(a, ((0, 0), (0, NEW_PAD - ts), (0, 0)))

        cq = yc[:, 0:br].reshape(bs, ts, C_HEADS, 2, cd2 // 2)
        sel = (jnp.arange(C_HEADS)[:, None, None, None] == jnp.arange(C_HEADS)[None, None, :, None]) & \
              (jnp.arange(2)[None, :, None, None] == jnp.arange(2)[None, None, None, :])
        q_rows = jnp.einsum('bthmd,hmgn->bmthgnd', cq, sel.astype(F32)).reshape(bs, 2 * ts * C_HEADS, br)
        new_c = pad_new(yc[:, br:3 * br].reshape(bs, ts, 2 * br))
        o_c = _diff_decode(q_rows, new_c, diff_pool_t, l, page_table, diff_norm_gain[l], lam, lam_init, ts, past)
        o_c = o_c.reshape(bs * ts, br)

        bq = yb[:, 0:br].reshape(bs, ts, B_KV_HEADS, B_GROUP, hd).transpose(0, 2, 3, 1, 4)
        bq = jnp.pad(bq, ((0, 0), (0, 0), (0, 0), (0, DEC_ROWS - ts), (0, 0)))
        q_plain = bq.reshape(bs, B_KV_HEADS, B_GROUP * DEC_ROWS, hd)
        zero = jnp.zeros_like(q_plain)
        q_half = jnp.stack([jnp.concatenate([q_plain, zero], axis=-1),
                            jnp.concatenate([zero, q_plain], axis=-1)], axis=2)
        gl = yb[:, br + 6 * B_KV_HEADS * hd:].reshape(bs, ts, LANE)
        gl = jnp.pad(gl, ((0, 0), (0, DEC_ROWS - ts), (0, 0)))
        o_b = _nsa_decode(q_half, q_plain, win_pool_t, pad_new(new_win),
                          pad_new(new_kv[:, :, 2:4].reshape(bs, ts, 4 * hd)), nsa_pool_t, l, page_table,
                          lw['w_bdt'], lw['cconst2'], gl, lw['gate_b'], ts, past, k_past)
        o_b = o_b[:, :, :, :ts].transpose(0, 3, 1, 2, 4).reshape(bs * ts, br)

        xs = _merge(xs, g[2], g[3], (o_a, o_b, o_c, o_d), lw['w_merge'], lw['w_branch'], lw['w_out'])
        xs = _ffn(xs, g[4], g[5], lw['ffn_in'][1], lw['ffn_out'][1])

    st = lambda k: jnp.stack(outs[k])
    return (xp.reshape(bp, tp, d_model), xs.reshape(bs, ts, d_model), st('kvp'), st('kvs'), st('winp'), st('wins'),
            st('dkp'), st('dks'), st('hp'), st('hs'), st('gp'), st('gs'))
```

```python
import functools
import math

import numpy as np
import jax
import jax.numpy as jnp
from jax import lax
from jax.experimental import pallas as pl
from jax.experimental.pallas import tpu as pltpu

F32 = jnp.float32
BF16 = jnp.bfloat16
HI = lax.Precision.HIGHEST

N_BRANCH = 4
A_HEADS = 4
B_HEADS = 4
B_KV_HEADS = 2
B_GROUP = 2
C_HEADS = 4
D_HEADS = 4
NSA_BLOCK = 64
NSA_TOPK = 16
NSA_WINDOW = 512
GLA_RANK = 16
GLA_TAU = 16.0
NORM_EPS = 1e-6
LANE = 128
LIN_GROUP = 16
NEW_PAD = 16
DEC_ROWS = 8
ONES_PAD = 16
PAGE_PITCH_PAD = 8
VMEM_LIMIT = 56 * 1024 * 1024
LOG2E = 1.4426950408889634
NEG_BIG = -1e30

NT_DIMS = (((1,), (1,)), ((), ()))
TN_DIMS = (((0,), (0,)), ((), ()))


def _params(semantics):
    return pltpu.CompilerParams(dimension_semantics=semantics, vmem_limit_bytes=VMEM_LIMIT)


def _rms(x, g):
    return x * lax.rsqrt(jnp.mean(x * x, axis=-1, keepdims=True) + NORM_EPS) * g


def _dot(a, b):
    return jnp.dot(a, b, preferred_element_type=F32)


def _dot_nt(a, b):
    return lax.dot_general(a, b, NT_DIMS, preferred_element_type=F32)


def _row_tile(n, cap):
    t = min(n, cap)
    while n % t or t % 8:
        t -= 1
    return t


def _ffn_body(x_ref, gpre_ref, gpost_ref, wg_ref, wu_ref, wo_ref, o_ref, xn_ref, acc_ref):
    f = pl.program_id(1)

    @pl.when(f == 0)
    def _():
        xn_ref[...] = _rms(x_ref[...], gpre_ref[...]).astype(BF16)
        acc_ref[...] = jnp.zeros_like(acc_ref)

    xn = xn_ref[...]
    gate = _dot(xn, wg_ref[...])
    up = _dot(xn, wu_ref[...])
    act = (gate * jax.nn.sigmoid(gate) * up).astype(BF16)
    acc_ref[...] += _dot(act, wo_ref[...])

    @pl.when(f == pl.num_programs(1) - 1)
    def _():
        o_ref[...] = x_ref[...] + 0.5 * _rms(acc_ref[...], gpost_ref[...])


def _ffn(x, g_pre, g_post, w_in, w_out):
    n, d = x.shape
    dff = w_out.shape[0]
    tf = 256
    nf = dff // tf
    tm = _row_tile(n, 1024)
    return pl.pallas_call(
        _ffn_body,
        grid=(n // tm, nf),
        in_specs=[
            pl.BlockSpec((tm, d), lambda i, f: (i, 0)),
            pl.BlockSpec((1, d), lambda i, f: (0, 0)),
            pl.BlockSpec((1, d), lambda i, f: (0, 0)),
            pl.BlockSpec((d, tf), lambda i, f: (0, f)),
            pl.BlockSpec((d, tf), lambda i, f: (0, nf + f)),
            pl.BlockSpec((tf, d), lambda i, f: (f, 0)),
        ],
        out_specs=pl.BlockSpec((tm, d), lambda i, f: (i, 0)),
        out_shape=jax.ShapeDtypeStruct((n, d), F32),
        scratch_shapes=[pltpu.VMEM((tm, d), BF16), pltpu.VMEM((tm, d), F32)],
        compiler_params=_params(("parallel", "arbitrary")),
        name="ffn",
    )(x, g_pre.reshape(1, d), g_post.reshape(1, d), w_in, w_in, w_out)


def _proj_body(x_ref, g_ref, wa, wb, wc, wd, oa, ob, oc, od):
    xn = _rms(x_ref[...], g_ref[...]).astype(BF16)
    for w, o in ((wa, oa), (wb, ob), (wc, oc), (wd, od)):
        o[...] = _dot(xn, w[...])


def _proj(x, g, ws):
    n, d = x.shape
    tm = _row_tile(n, 512)
    return pl.pallas_call(
        _proj_body,
        grid=(n // tm,),
        in_specs=[pl.BlockSpec((tm, d), lambda i: (i, 0)), pl.BlockSpec((1, d), lambda i: (0, 0))]
        + [pl.BlockSpec(w.shape, lambda i: (0, 0)) for w in ws],
        out_specs=[pl.BlockSpec((tm, w.shape[1]), lambda i: (i, 0)) for w in ws],
        out_shape=[jax.ShapeDtypeStruct((n, w.shape[1]), F32) for w in ws],
        compiler_params=_params(("parallel",)),
        name="mixer_in_proj",
    )(x, g.reshape(1, d), *ws)


def _merge_body(x_ref, g2_ref, g3_ref, oa, ob, oc, od, wm_ref, wb_ref, wo_ref, out_ref):
    x = x_ref[...]
    d = x.shape[1]
    h = _rms(x, g2_ref[...]).astype(BF16)
    s = None
    for n, br in enumerate((oa, ob, oc, od)):
        gate = jax.nn.sigmoid(_dot(h, wm_ref[:, n * d:(n + 1) * d]))
        term = gate * _dot(br[...].astype(BF16), wb_ref[n])
        s = term if s is None else s + term
    y = _dot(s.astype(BF16), wo_ref[...])
    out_ref[...] = x + _rms(y, g3_ref[...])


def _merge(x, g2, g3, branches, w_merge, w_branch, w_out):
    n, d = x.shape
    br = branches[0].shape[1]
    tm = _row_tile(n, 256)
    row = lambda i: (i, 0)
    fix2 = lambda i: (0, 0)
    return pl.pallas_call(
        _merge_body,
        grid=(n // tm,),
        in_specs=[pl.BlockSpec((tm, d), row), pl.BlockSpec((1, d), fix2), pl.BlockSpec((1, d), fix2)]
        + [pl.BlockSpec((tm, br), row)] * N_BRANCH
        + [pl.BlockSpec(w_merge.shape, fix2), pl.BlockSpec(w_branch.shape, lambda i: (0, 0, 0)),
           pl.BlockSpec(w_out.shape, fix2)],
        out_specs=pl.BlockSpec((tm, d), row),
        out_shape=jax.ShapeDtypeStruct((n, d), F32),
        compiler_params=_params(("parallel",)),
        name="merge",
    )(x, g2.reshape(1, d), g3.reshape(1, d), *branches, w_merge, w_branch, w_out)


def _log_sigmoid(z):
    return jnp.minimum(z, 0.0) - jnp.log1p(jnp.exp(-jnp.abs(z)))


def _lin_core(q, k, v, lf, gg, gain, s0_ref, o_ref, st_ref, qt_s, kt_s, vt_s, dec_s, oi_s, st_s,
              *, dk, dv, t_valid):
    tt, wk = q.shape
    wv = v.shape[1]
    grp = LIN_GROUP
    t_idx = pl.program_id(1)

    @pl.when(t_idx == 0)
    def _():
        st_s[...] = s0_ref[...]

    if t_valid is not None:
        tok = t_idx * tt + lax.broadcasted_iota(jnp.int32, (tt, 1), 0)
        lf = jnp.where(tok < t_valid, lf, 0.0)

    r = lax.broadcasted_iota(jnp.int32, (tt, tt), 0)
    c = lax.broadcasted_iota(jnp.int32, (tt, tt), 1)
    same = (r // grp) == (c // grp)
    tri = jnp.where(same & (c <= r), 1.0, 0.0).astype(F32)
    ones_g = jnp.where(same, 1.0, 0.0).astype(F32)
    b = jnp.dot(tri, lf, precision=HI, preferred_element_type=F32)
    bl = jnp.dot(ones_g, lf, precision=HI, preferred_element_type=F32)

    qt_s[...] = (q * jnp.exp(b)).astype(BF16)
    kt_s[...] = (k * jnp.exp(bl - b)).astype(BF16)
    vt_s[...] = v.astype(BF16)
    dec_s[...] = jnp.exp(bl)

    hk = lax.broadcasted_iota(jnp.int32, (wk, wv), 0) // dk
    hv = lax.broadcasted_iota(jnp.int32, (wk, wv), 1) // dv
    ones_hd = jnp.where(hk == hv, 1.0, 0.0).astype(BF16)

    rowm = lax.broadcasted_iota(jnp.int32, (tt, 1), 0) % grp
    od = jnp.zeros((tt, wv), F32)
    for d in range(grp):
        ks = k if d == 0 else pltpu.roll(k, d, 0)
        bs = b if d == 0 else pltpu.roll(b, d, 0)
        vs = v if d == 0 else pltpu.roll(v, d, 0)
        e = jnp.exp(jnp.minimum(b - bs, 0.0))
        z = jnp.where(rowm >= d, q * ks * e, 0.0).astype(BF16)
        od = od + _dot(z, ones_hd) * vs

    mv = lax.broadcasted_iota(jnp.int32, (wv, wk), 0) // dv
    mk = lax.broadcasted_iota(jnp.int32, (wv, wk), 1) // dk
    mbd = jnp.where(mv == mk, 1.0, 0.0).astype(F32)

    def step(i, carry):
        r0 = pl.multiple_of(i * grp, grp)
        qg = qt_s[pl.ds(r0, grp), :]
        kg = kt_s[pl.ds(r0, grp), :]
        vg = vt_s[pl.ds(r0, grp), :]
        s = st_s[...]
        oi_s[pl.ds(r0, grp), :] = _dot_nt(qg, s.astype(BF16))
        upd = lax.dot_general(vg, kg, TN_DIMS, preferred_element_type=F32)
        st_s[...] = dec_s[pl.ds(r0, 1), :] * s + mbd * upd
        return carry

    lax.fori_loop(0, tt // grp, step, 0)

    o = oi_s[...] + od
    pv = lax.broadcasted_iota(jnp.int32, (wv, wv), 0) // dv
    pw = lax.broadcasted_iota(jnp.int32, (wv, wv), 1) // dv
    avg = jnp.where(pv == pw, 1.0 / dv, 0.0).astype(F32)
    ms = jnp.dot(o * o, avg, precision=HI, preferred_element_type=F32)
    o_ref[...] = o * lax.rsqrt(ms + NORM_EPS) * gain * (gg * jax.nn.sigmoid(gg))

    @pl.when(t_idx == pl.num_programs(1) - 1)
    def _():
        st_ref[...] = st_s[...]


def _hgrn_body(y_ref, la_ref, lc_ref, gain_ref, s0_ref, o_ref, st_ref, *scratch, w, dk, t_valid):
    y = y_ref[...]
    q = y[:, 0:w] * (dk ** -0.5)
    z = y[:, w:2 * w]
    v = y[:, 2 * w:3 * w]
    gg = y[:, 3 * w:4 * w]
    a = la_ref[...] + _log_sigmoid(z)
    cc = lc_ref[...]
    lf = jnp.maximum(a, cc) + jnp.log1p(jnp.exp(-jnp.abs(a - cc)))
    k = jnp.exp(la_ref[...] + _log_sigmoid(-z))
    _lin_core(q, k, v, lf, gg, gain_ref[...], s0_ref, o_ref, st_ref, *scratch, dk=dk, dv=dk, t_valid=t_valid)


def _gla_body(y_ref, w2_ref, b2_ref, gain_ref, s0_ref, o_ref, st_ref, *scratch, wk, wv, dk, dv, t_valid):
    y = y_ref[...]
    q = y[:, 0:wk] * (dk ** -0.5)
    k = y[:, wk:2 * wk]
    v = y[:, 2 * wk:2 * wk + wv]
    lr = y[:, 2 * wk + wv:2 * wk + wv + LANE]
    gg = y[:, 2 * wk + wv + LANE:2 * wk + 2 * wv + LANE]
    u = jnp.dot(lr, w2_ref[...], precision=HI, preferred_element_type=F32) + b2_ref[...]
    lf = _log_sigmoid(u) * (1.0 / GLA_TAU)
    _lin_core(q, k, v, lf, gg, gain_ref[...], s0_ref, o_ref, st_ref, *scratch, dk=dk, dv=dv, t_valid=t_valid)


def _lin_call(body, y, bsz, t, tt, vecs, s0, wk, wv):
    nt = t // tt
    cw = y.shape[1]
    return pl.pallas_call(
        body,
        grid=(bsz, nt),
        in_specs=[pl.BlockSpec((tt, cw), lambda b, i: (b * nt + i, 0))]
        + [pl.BlockSpec(a.shape, lambda b, i: (0, 0)) for a in vecs]
        + [pl.BlockSpec((None, wv, wk), lambda b, i: (b, 0, 0))],
        out_specs=[pl.BlockSpec((tt, wv), lambda b, i: (b * nt + i, 0)),
                   pl.BlockSpec((None, wv, wk), lambda b, i: (b, 0, 0))],
        out_shape=[jax.ShapeDtypeStruct((bsz * t, wv), F32), jax.ShapeDtypeStruct((bsz, wv, wk), F32)],
        scratch_shapes=[pltpu.VMEM((tt, wk), BF16), pltpu.VMEM((tt, wk), BF16), pltpu.VMEM((tt, wv), BF16),
                        pltpu.VMEM((tt, wk), F32), pltpu.VMEM((tt, wv), F32), pltpu.VMEM((wv, wk), F32)],
        compiler_params=_params(("parallel", "arbitrary")),
        name="gated_linear",
    )(y, *vecs, s0)


def _state_to_bd(state):
    bsz, nh, dk, dv = state.shape
    eye = jnp.eye(nh, dtype=state.dtype)
    return jnp.einsum('bhkv,hg->bhvgk', state, eye).reshape(bsz, nh * dv, nh * dk)


def _bd_to_state(st, nh):
    bsz, wv, wk = st.shape
    dv, dk = wv // nh, wk // nh
    blocks = st.reshape(bsz, nh, dv, nh, dk)
    idx = jnp.arange(nh)
    return blocks[:, idx, :, idx, :].transpose(1, 0, 3, 2)


def _pad_group(y, bsz, t):
    tp = -(-t // LIN_GROUP) * LIN_GROUP
    if tp == t:
        return y, tp
    y3 = jnp.pad(y.reshape(bsz, t, -1), ((0, 0), (0, tp - t), (0, 0)))
    return y3.reshape(bsz * tp, -1), tp


def _linear_mixer(body, y, bsz, t, vecs, state0, nh, dk, dv):
    yp, tp = _pad_group(y, bsz, t)
    tt = min(tp, LANE)
    wk, wv = nh * dk, nh * dv
    s0 = jnp.zeros((bsz, wv, wk), F32) if state0 is None else _state_to_bd(state0)
    body = functools.partial(body, t_valid=None if tp == t else t)
    o, st = _lin_call(body, yp, bsz, tp, tt, vecs, s0, wk, wv)
    if tp != t:
        o = o.reshape(bsz, tp, wv)[:, :t].reshape(bsz * t, wv)
    return o, _bd_to_state(st, nh)


def _col_update(idx, tiles, m_s, acc_s):
    m_old = m_s[idx]
    m_new = m_old
    for s, _, shift in tiles:
        smax = jnp.max(s, axis=0, keepdims=True)
        m_new = jnp.maximum(m_new, smax if shift is None else smax + shift)
    m_safe = jnp.where(m_new == -jnp.inf, 0.0, m_new)
    pv = None
    for s, vt, shift in tiles:
        p = jnp.exp2(s + ((-m_safe) if shift is None else (shift - m_safe)))
        d = _dot(vt, p.astype(BF16))
        pv = d if pv is None else pv + d
    acc_s[idx] = jnp.exp2(m_old - m_safe) * acc_s[idx] + pv
    m_s[idx] = m_new


def _col_finish(acc, dv):
    den = acc[dv:dv + 1, :]
    return acc[:dv, :] / jnp.where(den > 0, den, 1.0)


def _col_reset(m_s, acc_s):
    m_s[...] = jnp.full(m_s.shape, -jnp.inf, F32)
    acc_s[...] = jnp.zeros(acc_s.shape, F32)


def _online(s, v, m, l, a, v_t=False):
    m_new = jnp.maximum(m, jnp.max(s, axis=1, keepdims=True))
    m_safe = jnp.where(m_new == -jnp.inf, 0.0, m_new)
    p = jnp.exp2(s - m_safe)
    alpha = jnp.exp2(m - m_safe)
    l = alpha * l + jnp.sum(p, axis=1, keepdims=True)
    pv = _dot_nt(p.astype(BF16), v) if v_t else _dot(p.astype(BF16), v)
    return m_new, l, alpha * a + pv


def _finish(l, a):
    return a / jnp.where(l > 0, l, 1.0)


def _softmax_init(rows, width):
    return (jnp.full((rows, 1), -jnp.inf, F32), jnp.zeros((rows, 1), F32), jnp.zeros((rows, width), F32))


def _stage_keys_values(k_ref, v_ref, kb_s, vt_s, *, t, tk, w):
    kb_s[...] = k_ref[...].astype(BF16)
    n_heads = vt_s.shape[0]
    ones = jnp.where(lax.broadcasted_iota(jnp.int32, (ONES_PAD, tk), 0) == 0, 1.0, 0.0).astype(BF16)

    def body(jt, carry):
        vt = v_ref[pl.ds(pl.multiple_of(jt * tk, tk), tk), :].T
        for hh in range(n_heads):
            vt_s[hh, jt, 0:w, :] = vt[hh * w:(hh + 1) * w, :].astype(BF16)
            vt_s[hh, jt, w:w + ONES_PAD, :] = ones
        return carry

    lax.fori_loop(0, t // tk, body, 0)


def _place_rows(x, lo, total):
    r, n = x.shape
    parts = []
    if lo:
        parts.append(jnp.zeros((lo, n), x.dtype))
    parts.append(x)
    if total - lo - r:
        parts.append(jnp.zeros((total - lo - r, n), x.dtype))
    return jnp.concatenate(parts, axis=0) if len(parts) > 1 else x


def _diff_prompt_body(sl_ref, lam_ref, q_ref, k_ref, v_ref, gain_ref, o_ref, kb_s, vt_s, boff_s, m_s, acc_s,
                      *, t, tq, tk, cd, lam_init):
    hp = pl.program_id(1)
    i = pl.program_id(2)
    w = 2 * cd
    pair = LANE // w

    @pl.when(i == 0)
    def _():
        _stage_keys_values(k_ref, v_ref, kb_s, vt_s, t=t, tk=tk, w=w)

    qt = (q_ref[...] * (cd ** -0.5 * LOG2E)).T
    slope2 = [sl_ref[hp * pair + hh] * LOG2E for hh in range(pair)]
    chains = [(hh, _place_rows(qt[hh * w + mi * cd:hh * w + (mi + 1) * cd, :], hh * w + mi * cd, LANE).astype(BF16))
              for hh in range(pair) for mi in range(2)]
    rel = lax.broadcasted_iota(jnp.int32, (tk, tq), 1) - lax.broadcasted_iota(jnp.int32, (tk, tq), 0)
    rel_f = rel.astype(F32)
    for hh in range(pair):
        boff_s[hh] = rel_f * (-slope2[hh])
    _col_reset(m_s, acc_s)
    ratio = tq // tk

    def key_tiles(jj):
        return [kb_s[pl.ds(pl.multiple_of((jj * ratio + r) * tk, tk), tk), :] for r in range(ratio)]

    def off_group(jj, carry):
        kts = key_tiles(jj)
        for ci, (hh, qc) in enumerate(chains):
            boff = boff_s[hh]
            _col_update(ci, [(_dot(kts[r], qc) + boff, vt_s[hh, jj * ratio + r],
                              ((i - jj) * tq - r * tk).astype(F32) * (-slope2[hh])) for r in range(ratio)],
                        m_s, acc_s)
        return carry

    lax.fori_loop(0, i, off_group, 0)
    kts = key_tiles(i)
    dists = [rel - r * tk for r in range(ratio)]
    for ci, (hh, qc) in enumerate(chains):
        _col_update(ci, [(_dot(kts[r], qc) + jnp.where(dists[r] >= 0, dists[r].astype(F32) * (-slope2[hh]), -jnp.inf),
                          vt_s[hh, i * ratio + r], None) for r in range(ratio)], m_s, acc_s)

    outs = []
    for hh in range(pair):
        o = _col_finish(acc_s[2 * hh], w) - lam_ref[0] * _col_finish(acc_s[2 * hh + 1], w)
        ms = jnp.mean(o * o, axis=0, keepdims=True)
        outs.append(o * lax.rsqrt(ms + NORM_EPS))
    o_ref[...] = (jnp.concatenate(outs, axis=0) * gain_ref[...] * (1.0 - lam_init)).T


def _diff_prompt(yc, bsz, t, gain, lam, slopes, lam_init, tq, tk):
    w = gain.shape[0]
    nh = yc.shape[1] // (3 * w)
    pair = LANE // w
    nhp = nh // pair
    nq = t // tq
    dva = w + ONES_PAD
    smem = pl.BlockSpec(memory_space=pltpu.SMEM)
    return pl.pallas_call(
        functools.partial(_diff_prompt_body, t=t, tq=tq, tk=tk, cd=w // 2, lam_init=lam_init),
        grid=(bsz, nhp, nq),
        in_specs=[smem, smem,
                  pl.BlockSpec((tq, LANE), lambda b, hp, i: (b * nq + i, hp)),
                  pl.BlockSpec((t, LANE), lambda b, hp, i: (b, nhp + hp)),
                  pl.BlockSpec((t, LANE), lambda b, hp, i: (b, 2 * nhp + hp)),
                  pl.BlockSpec((LANE, 1), lambda b, hp, i: (0, 0))],
        out_specs=pl.BlockSpec((tq, LANE), lambda b, hp, i: (b * nq + i, hp)),
        out_shape=jax.ShapeDtypeStruct((bsz * t, nh * w), F32),
        scratch_shapes=[pltpu.VMEM((t, LANE), BF16), pltpu.VMEM((pair, t // tk, dva, tk), BF16),
                        pltpu.VMEM((pair, tk, tq), F32), pltpu.VMEM((2 * pair, 1, tq), F32),
                        pltpu.VMEM((2 * pair, dva, tq), F32)],
        compiler_params=_params(("parallel", "parallel", "arbitrary")),
        name="diff_attn_prompt",
    )(slopes, lam, yc, yc, yc, jnp.tile(gain, pair).reshape(LANE, 1))


def _compress(xk_ref, xv_ref, w_ref, nb):
    def body(j, acc):
        ak, av = acc
        xk = xk_ref[pl.ds(j, nb, stride=NSA_BLOCK), :].astype(BF16)
        xv = xv_ref[pl.ds(j, nb, stride=NSA_BLOCK), :].astype(BF16)
        return ak + _dot(xk, w_ref[0, j]), av + _dot(xv, w_ref[1, j])

    zero = jnp.zeros((nb, xk_ref.shape[1]), F32)
    ak, av = lax.fori_loop(0, NSA_BLOCK, body, (zero, zero))
    return jnp.concatenate([ak, av], axis=1)


def _compress_body(xk_ref, xv_ref, w_ref, c_ref, o_ref, *, nb, hd):
    acc = _compress(xk_ref, xv_ref, w_ref, nb) + c_ref[...]
    for p in range(4):
        o_ref[p] = acc[:, p * hd:(p + 1) * hd]


def _compress_prompt(yb, bsz, t, w_bd, cconst, hd):
    nb = t // NSA_BLOCK
    cw = 4 * hd
    hw = cw // 2
    return pl.pallas_call(
        functools.partial(_compress_body, nb=nb, hd=hd),
        grid=(bsz,),
        in_specs=[pl.BlockSpec((t, hw), lambda b: (b, 2)),
                  pl.BlockSpec((t, hw), lambda b: (b, 3)),
                  pl.BlockSpec(w_bd.shape, lambda b: (0, 0, 0, 0)),
                  pl.BlockSpec((1, cw), lambda b: (0, 0))],
        out_specs=pl.BlockSpec((None, 4, nb, hd), lambda b: (b, 0, 0, 0)),
        out_shape=jax.ShapeDtypeStruct((bsz, 4, nb, hd), F32),
        compiler_params=_params(("parallel",)),
        name="nsa_compress_prompt",
    )(yb, yb, w_bd, cconst)


def _rank_rows(score, tq):
    nb = score.shape[0]
    sub = 8
    slabs = [score[v * sub:(v + 1) * sub, :] for v in range(nb // sub)]
    ranks = [jnp.zeros((sub, tq), F32) for _ in slabs]
    sub_i = lax.broadcasted_iota(jnp.int32, (sub, 1), 0)
    for ib in range(nb):
        row = score[ib:ib + 1, :]
        for v, slab in enumerate(slabs):
            if ib < v * sub:
                ahead = jnp.where(row >= slab, 1.0, 0.0)
            elif ib >= (v + 1) * sub:
                ahead = jnp.where(row > slab, 1.0, 0.0)
            else:
                ahead = jnp.where(sub_i > ib - v * sub, jnp.where(row >= slab, 1.0, 0.0),
                                  jnp.where(row > slab, 1.0, 0.0))
            ranks[v] = ranks[v] + ahead
    return jnp.concatenate(ranks, axis=0)


def _nsa_prompt_body(sl_ref, q_ref, cmp_ref, ks_ref, vs_ref, kw_ref, vw_ref, gl_ref, gb_ref, o_ref,
                     ksb_s, kwb_s, vst_s, vwt_s, boff_s, m_s, acc_s, *, t, tq, tk, nb, hd, k_sel):
    i = pl.program_id(1)
    heads = [(g, n) for g in range(B_KV_HEADS) for n in range(B_GROUP)]
    nh = len(heads)

    @pl.when(i == 0)
    def _():
        _stage_keys_values(ks_ref, vs_ref, ksb_s, vst_s, t=t, tk=tk, w=hd)
        _stage_keys_values(kw_ref, vw_ref, kwb_s, vwt_s, t=t, tk=tk, w=hd)

    qt = (q_ref[...] * (hd ** -0.5 * LOG2E)).T
    qh = [qt[hi * hd:(hi + 1) * hd, :].astype(BF16) for hi in range(nh)]
    qk = [_place_rows(qh[hi], g * hd, LANE) for hi, (g, _) in enumerate(heads)]
    sl2 = [sl_ref[hi] * LOG2E for hi in range(nh)]
    qpos = i * tq + lax.broadcasted_iota(jnp.int32, (1, tq), 1)

    blk = lax.broadcasted_iota(jnp.int32, (nb, 1), 0)
    distc = qpos - (blk * NSA_BLOCK + (NSA_BLOCK - 1))
    distc_f = distc.astype(F32)
    cur = qpos // NSA_BLOCK
    forced = (blk == 0) | (blk == cur) | (blk == cur - 1)
    o_cmp = []
    selneg = []
    for g in range(B_KV_HEADS):
        kc = cmp_ref[g].astype(BF16)
        vc = cmp_ref[B_KV_HEADS + g].astype(BF16)
        imp = None
        for n in range(B_GROUP):
            hi = g * B_GROUP + n
            s = jnp.where(distc >= 0, _dot(kc, qh[hi]) - sl2[hi] * distc_f, -jnp.inf)
            m = jnp.max(s, axis=0, keepdims=True)
            e = jnp.exp2(s - jnp.where(m == -jnp.inf, 0.0, m))
            den = jnp.sum(e, axis=0, keepdims=True)
            p = e / jnp.where(den > 0, den, 1.0)
            o_cmp.append(lax.dot_general(vc, p.astype(BF16), TN_DIMS, preferred_element_type=F32))
            imp = p if imp is None else imp + p
        score = jnp.where(blk > cur, -jnp.inf, jnp.where(forced, jnp.inf, imp))
        selneg.append(jnp.where(_rank_rows(score, tq) < k_sel, 0.0, NEG_BIG).astype(BF16))

    rel = lax.broadcasted_iota(jnp.int32, (tk, tq), 1) - lax.broadcasted_iota(jnp.int32, (tk, tq), 0)
    rel_f = rel.astype(F32)
    for hi in range(nh):
        boff_s[hi] = rel_f * (-sl2[hi])
    ratio = tq // tk
    erow = lax.broadcasted_iota(jnp.int32, (tk, nb), 0)
    eblk = lax.broadcasted_iota(jnp.int32, (tk, nb), 1)

    def block_masks(j):
        expand = jnp.where((erow + j * tk) // NSA_BLOCK == eblk, 1.0, 0.0).astype(BF16)
        return [_dot(expand, sn) for sn in selneg]

    _col_reset(m_s, acc_s)

    def key_tiles(kb_s, jj):
        return [kb_s[pl.ds(pl.multiple_of((jj * ratio + r) * tk, tk), tk), :] for r in range(ratio)]

    def sel_off(jj, carry):
        kts = key_tiles(ksb_s, jj)
        mbs = [block_masks(jj * ratio + r) for r in range(ratio)]
        for hi, (g, _) in enumerate(heads):
            boff = boff_s[hi]
            _col_update(hi, [(_dot(kts[r], qk[hi]) + boff + mbs[r][g], vst_s[g, jj * ratio + r],
                              ((i - jj) * tq - r * tk).astype(F32) * (-sl2[hi])) for r in range(ratio)], m_s, acc_s)
        return carry

    lax.fori_loop(0, i, sel_off, 0)
    kts = key_tiles(ksb_s, i)
    mbs = [block_masks(i * ratio + r) for r in range(ratio)]
    dists = [rel - r * tk for r in range(ratio)]
    for hi, (g, _) in enumerate(heads):
        _col_update(hi, [(_dot(kts[r], qk[hi]) + mbs[r][g]
                          + jnp.where(dists[r] >= 0, dists[r].astype(F32) * (-sl2[hi]), -jnp.inf),
                          vst_s[g, i * ratio + r], None) for r in range(ratio)], m_s, acc_s)
    o_sel = [_col_finish(acc_s[hi], hd) for hi in range(nh)]

    _col_reset(m_s, acc_s)
    for ds in range(-(NSA_WINDOW // tq), 1):
        def win_span(ds=ds):
            kts = key_tiles(kwb_s, i + ds)
            dists = [rel - (ds * tq + r * tk) for r in range(ratio)]
            for hi, (g, _) in enumerate(heads):
                _col_update(hi, [(_dot(kts[r], qk[hi])
                                  + jnp.where((dists[r] >= 0) & (dists[r] < NSA_WINDOW),
                                              dists[r].astype(F32) * (-sl2[hi]), -jnp.inf),
                                  vwt_s[g, (i + ds) * ratio + r], None) for r in range(ratio)], m_s, acc_s)

        if ds < 0:
            pl.when(i + ds >= 0)(win_span)
        else:
            win_span()
    o_win = [_col_finish(acc_s[hi], hd) for hi in range(nh)]

    gates = jax.nn.sigmoid(gl_ref[...] + gb_ref[...]).T
    outs = [gates[3 * hi:3 * hi + 1] * o_cmp[hi] + gates[3 * hi + 1:3 * hi + 2] * o_sel[hi]
            + gates[3 * hi + 2:3 * hi + 3] * o_win[hi] for hi in range(nh)]
    o_ref[...] = jnp.concatenate(outs, axis=0).T


def _nsa_prompt(yb, cmp_kv, bsz, t, gate_b, slopes, tq, tk):
    nb, hd = cmp_kv.shape[2], cmp_kv.shape[3]
    qw = B_HEADS * hd
    nq = t // tq
    first = qw // LANE
    dva = hd + ONES_PAD
    kv_spec = lambda c: pl.BlockSpec((t, LANE), lambda b, i: (b, first + c))
    return pl.pallas_call(
        functools.partial(_nsa_prompt_body, t=t, tq=tq, tk=tk, nb=nb, hd=hd, k_sel=min(NSA_TOPK, nb)),
        grid=(bsz, nq),
        in_specs=[pl.BlockSpec(memory_space=pltpu.SMEM),
                  pl.BlockSpec((tq, qw), lambda b, i: (b * nq + i, 0)),
                  pl.BlockSpec((None, 2 * B_KV_HEADS, nb, hd), lambda b, i: (b, 0, 0, 0)),
                  kv_spec(2), kv_spec(3), kv_spec(4), kv_spec(5),
                  pl.BlockSpec((tq, LANE), lambda b, i: (b * nq + i, first + 6)),
                  pl.BlockSpec((1, LANE), lambda b, i: (0, 0))],
        out_specs=pl.BlockSpec((tq, qw), lambda b, i: (b * nq + i, 0)),
        out_shape=jax.ShapeDtypeStruct((bsz * t, qw), F32),
        scratch_shapes=[pltpu.VMEM((t, LANE), BF16), pltpu.VMEM((t, LANE), BF16),
                        pltpu.VMEM((B_KV_HEADS, t // tk, dva, tk), BF16),
                        pltpu.VMEM((B_KV_HEADS, t // tk, dva, tk), BF16),
                        pltpu.VMEM((B_HEADS, tk, tq), F32), pltpu.VMEM((B_HEADS, 1, tq), F32),
                        pltpu.VMEM((B_HEADS, dva, tq), F32)],
        compiler_params=_params(("parallel", "arbitrary")),
        name="nsa_attn_prompt",
    )(slopes, yb, cmp_kv, yb, yb, yb, yb, yb, gate_b)


def _diff_decode_body(pt_ref, lam_ref, q_ref, new_ref, gain_ref, *refs, n_pp, page, past, nh, cd, lam_init, dec_t):
    pages = refs[:n_pp]
    o_ref = refs[n_pp]
    m_s, l_s, a_s = refs[n_pp + 1:]
    c = pl.program_id(1)
    w = 2 * cd * nh
    rows = 2 * dec_t * nh
    row = lax.broadcasted_iota(jnp.int32, (rows, 1), 0)
    rh = row % nh
    rt = (row % (dec_t * nh)) // nh
    slope2 = jnp.zeros((rows, 1), F32)
    for h in range(nh):
        slope2 = jnp.where(rh == h, 2.0 ** (-2 * (h + 1)) * LOG2E, slope2)
    qb = (q_ref[...] * (cd ** -0.5 * LOG2E)).astype(BF16)

    @pl.when(c == 0)
    def _():
        new = new_ref[...]
        kn = new[:, :w].astype(BF16)
        vn = new[:, w:].astype(BF16)
        col = lax.broadcasted_iota(jnp.int32, (1, new.shape[0]), 1)
        dist = rt - col
        s = jnp.where((dist >= 0) & (col < dec_t), _dot_nt(qb, kn) - slope2 * dist.astype(F32), -jnp.inf)
        m_s[...], l_s[...], a_s[...] = _online(s, vn, *_softmax_init(rows, w))

    kt = jnp.concatenate([pages[p][0:w, :].astype(BF16) for p in range(n_pp)], axis=1)
    vt = jnp.concatenate([pages[p][w:2 * w, :].astype(BF16) for p in range(n_pp)], axis=1)
    col = lax.broadcasted_iota(jnp.int32, (1, n_pp * page), 1)
    dist = (past + rt) - (c * (n_pp * page) + col)
    s = _dot(qb, kt) - slope2 * dist.astype(F32)
    m, l, a = _online(s, vt, m_s[...], l_s[...], a_s[...], v_t=True)
    m_s[...], l_s[...], a_s[...] = m, l, a

    @pl.when(c == pl.num_programs(1) - 1)
    def _():
        full = _finish(l, a)
        own = jnp.zeros((rows, 2 * cd), F32)
        for h in range(nh):
            own = jnp.where(rh == h, full[:, h * 2 * cd:(h + 1) * 2 * cd], own)
        half = rows // 2
        o = own[:half] - lam_ref[0] * own[half:]
        o_ref[...] = _rms(o, gain_ref[...]) * (1.0 - lam_init)


def _diff_decode(q_rows, new_kv, cache_t, layer, page_table, gain, lam, lam_init, dec_t, past):
    bsz, rows, w = q_rows.shape
    n_pages = page_table.shape[1]
    page = cache_t.shape[3]
    n_pp = math.gcd(16, n_pages)
    nh = C_HEADS
    cd = w // (2 * nh)

    def page_spec(p):
        return pl.BlockSpec((None, None, 2 * w, page),
                            lambda b, c, pt: (layer, pt[b * n_pages + c * n_pp + p], 0, 0))

    grid_spec = pltpu.PrefetchScalarGridSpec(
        num_scalar_prefetch=1,
        grid=(bsz, n_pages // n_pp),
        in_specs=[pl.BlockSpec(memory_space=pltpu.SMEM),
                  pl.BlockSpec((None, rows, w), lambda b, c, pt: (b, 0, 0)),
                  pl.BlockSpec((None, NEW_PAD, 2 * w), lambda b, c, pt: (b, 0, 0)),
                  pl.BlockSpec((1, 2 * cd), lambda b, c, pt: (0, 0))]
        + [page_spec(p) for p in range(n_pp)],
        out_specs=pl.BlockSpec((None, rows // 2, 2 * cd), lambda b, c, pt: (b, 0, 0)),
        scratch_shapes=[pltpu.VMEM((rows, 1), F32), pltpu.VMEM((rows, 1), F32), pltpu.VMEM((rows, w), F32)],
    )
    return pl.pallas_call(
        functools.partial(_diff_decode_body, n_pp=n_pp, page=page, past=past, nh=nh, cd=cd,
                          lam_init=lam_init, dec_t=dec_t),
        grid_spec=grid_spec,
        out_shape=jax.ShapeDtypeStruct((bsz, rows // 2, 2 * cd), F32),
        compiler_params=_params(("parallel", "arbitrary")),
        name="diff_attn_decode",
    )(page_table.reshape(-1), lam, q_rows, new_kv, gain.reshape(1, 2 * cd), *([cache_t] * n_pp))


def _nsa_decode_body(pt_ref, qh_ref, qd_ref, wc_ref, wn_ref, sn_ref, wbd_ref, cc_ref, gl_ref, gb_ref, *refs,
                     n_pp, page, past, hd, dec_t, n_pages, k_past, pitch):
    pages = refs[:n_pp]
    o_ref = refs[n_pp]
    x_s, ak_s, av_s, mtok_s, ocw_s, m_s, l_s, a_s = refs[n_pp + 1:]
    ph = pl.program_id(1)
    c = pl.program_id(2)
    last = pl.num_programs(2) - 1
    tp = DEC_ROWS
    rows = B_GROUP * tp
    row = lax.broadcasted_iota(jnp.int32, (rows, 1), 0)
    rt = row % tp
    scale2 = hd ** -0.5 * LOG2E
    qpos = past + rt

    def slope2(g):
        return jnp.where(row < tp, 2.0 ** (-(2 * (g * B_GROUP) + 1)) * LOG2E,
                         2.0 ** (-(2 * (g * B_GROUP + 1) + 1)) * LOG2E)

    def new_scores(qd, kn, sl):
        col = lax.broadcasted_iota(jnp.int32, (1, kn.shape[0]), 1)
        d = rt - col
        return jnp.where((d >= 0) & (col < dec_t), _dot_nt(qd, kn) - sl * d.astype(F32), -jnp.inf)

    @pl.when(ph == 0)
    def _():
        for p in range(n_pp):
            r0 = pl.multiple_of((c * n_pp + p) * pitch, 8)
            x_s[pl.ds(r0, 4 * hd), :] = pages[p][...]

    @pl.when((ph == 0) & (c == last))
    def _():
        unroll = 4

        def compress(cc):
            def body(du, acc):
                for u in range(unroll):
                    d = du * unroll + u
                    lhs = jnp.concatenate(
                        [x_s[pl.ds(cc * 2 * hd + g * hd + d, n_pages, stride=pitch), :]
                         for g in range(B_KV_HEADS)], axis=0).astype(BF16)
                    acc = acc + _dot(lhs, wbd_ref[cc, d])
                return acc

            return lax.fori_loop(0, hd // unroll, body, jnp.zeros((B_KV_HEADS * n_pages, 2 * hd), F32))

        ak_s[...] = compress(0)
        av_s[...] = compress(1)
        pgi = lax.broadcasted_iota(jnp.int32, (1, n_pages), 1)
        lane = lax.broadcasted_iota(jnp.int32, (1, page), 1)
        for g in range(B_KV_HEADS):
            sl = slope2(g)
            kc = (ak_s[g * n_pages:(g + 1) * n_pages, :] + cc_ref[0]).astype(BF16)
            vc = (av_s[g * n_pages:(g + 1) * n_pages, :] + cc_ref[1]).astype(BF16)
            ss = []
            for hf in range(2):
                qh = (qh_ref[g, hf] * scale2).astype(BF16)
                dist = qpos - ((2 * pgi + hf) * NSA_BLOCK + (NSA_BLOCK - 1))
                ss.append(jnp.where(dist >= 0, _dot_nt(qh, kc) - sl * dist.astype(F32), -jnp.inf))
            m = jnp.maximum(jnp.max(ss[0], axis=1, keepdims=True), jnp.max(ss[1], axis=1, keepdims=True))
            m = jnp.where(m == -jnp.inf, 0.0, m)
            es = [jnp.exp2(s - m) for s in ss]
            den = jnp.sum(es[0], axis=1, keepdims=True) + jnp.sum(es[1], axis=1, keepdims=True)
            den = jnp.where(den > 0, den, 1.0)
            ps = [e / den for e in es]
            full = [_dot(p.astype(BF16), vc) for p in ps]
            ocw_s[g, 0] = full[0][:, :hd] + full[1][:, hd:]

            imps = [p[:tp] + p[tp:] for p in ps]
            sc = [jnp.where(pgi == 0, jnp.inf, imps[0]), jnp.where(pgi == n_pages - 1, jnp.inf, imps[1])]
            rank = [jnp.zeros((tp, n_pages), F32), jnp.zeros((tp, n_pages), F32)]
            for pg in range(n_pages):
                for hf in range(2):
                    colv = sc[hf][:, pg:pg + 1]
                    for h2 in range(2):
                        tie = jnp.where(2 * pgi + h2 > 2 * pg + hf, 1.0, 0.0)
                        rank[h2] = rank[h2] + jnp.where(colv > sc[h2], 1.0, jnp.where(colv == sc[h2], tie, 0.0))
            sel = [jnp.where(r < k_past, 1.0, 0.0) for r in rank]
            sel = [jnp.concatenate([s_, s_], axis=0) for s_ in sel]
            for pg in range(n_pages):
                on = jnp.where(lane < NSA_BLOCK, sel[0][:, pg:pg + 1], sel[1][:, pg:pg + 1])
                dist = qpos - (pg * page + lane)
                lo = (pg % n_pp) * page
                mtok_s[g, pg // n_pp, :, lo:lo + page] = jnp.where(on > 0.5, dist.astype(F32) * (-sl), NEG_BIG)

            qd = (qd_ref[g] * scale2).astype(BF16)
            wkt = wc_ref[g * hd:(g + 1) * hd, :].astype(BF16)
            wvt = wc_ref[(B_KV_HEADS + g) * hd:(B_KV_HEADS + g + 1) * hd, :].astype(BF16)
            nwin = wkt.shape[1]
            d1 = (nwin + rt) - lax.broadcasted_iota(jnp.int32, (1, nwin), 1)
            s1 = jnp.where((d1 >= 0) & (d1 < NSA_WINDOW), _dot(qd, wkt) - sl * d1.astype(F32), -jnp.inf)
            wn = wn_ref[...]
            kn = wn[:, g * hd:(g + 1) * hd].astype(BF16)
            vn = wn[:, (B_KV_HEADS + g) * hd:(B_KV_HEADS + g + 1) * hd].astype(BF16)
            s2 = new_scores(qd, kn, sl)
            m = jnp.maximum(jnp.max(s1, axis=1, keepdims=True), jnp.max(s2, axis=1, keepdims=True))
            m = jnp.where(m == -jnp.inf, 0.0, m)
            e1 = jnp.exp2(s1 - m)
            e2 = jnp.exp2(s2 - m)
            den = jnp.sum(e1, axis=1, keepdims=True) + jnp.sum(e2, axis=1, keepdims=True)
            ocw_s[g, 1] = (_dot_nt(e1.astype(BF16), wvt) + _dot(e2.astype(BF16), vn)) / jnp.where(den > 0, den, 1.0)

    @pl.when(ph == 1)
    def _():
        for g in range(B_KV_HEADS):
            sl = slope2(g)
            qd = (qd_ref[g] * scale2).astype(BF16)

            @pl.when(c == 0)
            def _(g=g, sl=sl, qd=qd):
                sn = sn_ref[...]
                kn = sn[:, g * hd:(g + 1) * hd].astype(BF16)
                vn = sn[:, (B_KV_HEADS + g) * hd:(B_KV_HEADS + g + 1) * hd].astype(BF16)
                m_s[g], l_s[g], a_s[g] = _online(new_scores(qd, kn, sl), vn, *_softmax_init(rows, hd))

            kt = jnp.concatenate([pages[p][g * hd:(g + 1) * hd, :].astype(BF16) for p in range(n_pp)], axis=1)
            vt = jnp.concatenate([pages[p][(B_KV_HEADS + g) * hd:(B_KV_HEADS + g + 1) * hd, :].astype(BF16)
                                  for p in range(n_pp)], axis=1)
            s = _dot(qd, kt) + mtok_s[g, c]
            m_s[g], l_s[g], a_s[g] = _online(s, vt, m_s[g], l_s[g], a_s[g], v_t=True)

        @pl.when(c == last)
        def _():
            gates = jax.nn.sigmoid(gl_ref[...] + gb_ref[...])
            for g in range(B_KV_HEADS):
                o_sel = _finish(l_s[g], a_s[g])
                o_cmp = ocw_s[g, 0]
                o_win = ocw_s[g, 1]
                for n in range(B_GROUP):
                    r = slice(n * tp, (n + 1) * tp)
                    base = (g * B_GROUP + n) * 3
                    o_ref[g, n] = (gates[:, base:base + 1] * o_cmp[r] + gates[:, base + 1:base + 2] * o_sel[r]
                                   + gates[:, base + 2:base + 3] * o_win[r])


def _nsa_decode(q_half, q_plain, win_t, win_new, sel_new, cache_t, layer, page_table, w_bdt, cconst2,
                gate_logits, gate_b, dec_t, past, k_past):
    bsz = q_plain.shape[0]
    hd = q_plain.shape[-1]
    rows = q_plain.shape[2]
    n_pages = page_table.shape[1]
    page = cache_t.shape[3]
    nwin = win_t.shape[3]
    n_pp = math.gcd(16, n_pages)
    n_ch = n_pages // n_pp
    pitch = 4 * hd + PAGE_PITCH_PAD

    def page_spec(p):
        return pl.BlockSpec((None, None, 4 * hd, page),
                            lambda b, ph, c, pt: (layer, pt[b * n_pages + c * n_pp + p], ph, 0))

    fix = lambda *shape: pl.BlockSpec(shape, lambda b, ph, c, pt: (0,) * len(shape))
    grid_spec = pltpu.PrefetchScalarGridSpec(
        num_scalar_prefetch=1,
        grid=(bsz, 2, n_ch),
        in_specs=[pl.BlockSpec((None, B_KV_HEADS, 2, rows, 2 * hd), lambda b, ph, c, pt: (b, 0, 0, 0, 0)),
                  pl.BlockSpec((None, B_KV_HEADS, rows, hd), lambda b, ph, c, pt: (b, 0, 0, 0)),
                  pl.BlockSpec((None, None, 4 * hd, nwin), lambda b, ph, c, pt: (layer, b, 0, 0)),
                  pl.BlockSpec((None, NEW_PAD, 4 * hd), lambda b, ph, c, pt: (b, 0, 0)),
                  pl.BlockSpec((None, NEW_PAD, 4 * hd), lambda b, ph, c, pt: (b, 0, 0)),
                  fix(*w_bdt.shape), fix(*cconst2.shape),
                  pl.BlockSpec((None, DEC_ROWS, LANE), lambda b, ph, c, pt: (b, 0, 0)),
                  fix(1, LANE)]
        + [page_spec(p) for p in range(n_pp)],
        out_specs=pl.BlockSpec((None, B_KV_HEADS, B_GROUP, DEC_ROWS, hd), lambda b, ph, c, pt: (b, 0, 0, 0, 0)),
        scratch_shapes=[pltpu.VMEM((n_pages * pitch, page), F32),
                        pltpu.VMEM((B_KV_HEADS * n_pages, 2 * hd), F32),
                        pltpu.VMEM((B_KV_HEADS * n_pages, 2 * hd), F32),
                        pltpu.VMEM((B_KV_HEADS, n_ch, rows, n_pp * page), F32),
                        pltpu.VMEM((B_KV_HEADS, 2, rows, hd), F32),
                        pltpu.VMEM((B_KV_HEADS, rows, 1), F32), pltpu.VMEM((B_KV_HEADS, rows, 1), F32),
                        pltpu.VMEM((B_KV_HEADS, rows, hd), F32)],
    )
    return pl.pallas_call(
        functools.partial(_nsa_decode_body, n_pp=n_pp, page=page, past=past, hd=hd, dec_t=dec_t,
                          n_pages=n_pages, k_past=k_past, pitch=pitch),
        grid_spec=grid_spec,
        out_shape=jax.ShapeDtypeStruct((bsz, B_KV_HEADS, B_GROUP, DEC_ROWS, hd), F32),
        compiler_params=_params(("parallel", "arbitrary", "arbitrary")),
        name="nsa_decode",
    )(page_table.reshape(-1), q_half, q_plain, win_t, win_new, sel_new, w_bdt, cconst2, gate_logits, gate_b,
      *([cache_t] * n_pp))


def _split_points(d_model):
    br = d_model // N_BRANCH
    hd_b = br // B_HEADS
    d_dk = br // (2 * D_HEADS)
    splits = (br, br, br, br, br, 6 * B_KV_HEADS * hd_b, 3 * B_HEADS, br, br, br,
              D_HEADS * d_dk, D_HEADS * d_dk, br, GLA_RANK, br, N_BRANCH * d_model)
    return [0] + [int(p) for p in np.cumsum(splits)]


def _layer_weights(l, d_model, w_in, ffn_w_in, ffn_w_out, w_branch, w_out, nsa_cmp_pe, nsa_cmp_w,
                   gla_gate_w2, gla_gate_b, nsa_gate_b):
    pts = _split_points(d_model)
    w = w_in[l]
    seg = lambda i, j: w[:, pts[i]:pts[j]]
    padl = lambda a: jnp.pad(a, ((0, 0), (0, LANE - a.shape[1])))
    wa = seg(0, 4)
    wb = jnp.concatenate([seg(4, 6), padl(seg(6, 7))], axis=1)
    wc = seg(7, 10)
    wd = jnp.concatenate([seg(10, 13), padl(seg(13, 14)), seg(14, 15)], axis=1)
    hd = nsa_cmp_w.shape[-1]
    eye = jnp.eye(B_KV_HEADS, dtype=F32)
    w_bd = jnp.einsum('cjde,gh->cjgdhe', nsa_cmp_w[l], eye).reshape(
        2, NSA_BLOCK, B_KV_HEADS * hd, B_KV_HEADS * hd)
    w_bdt = jnp.einsum('cjde,ab->cdajbe', nsa_cmp_w[l], eye).reshape(
        2, hd, 2 * NSA_BLOCK, 2 * hd)
    cvec = jnp.einsum('cjd,cjde->ce', nsa_cmp_pe[l], nsa_cmp_w[l])
    cconst = jnp.repeat(cvec, B_KV_HEADS, axis=0).reshape(1, 4 * hd)
    cconst2 = jnp.concatenate([cvec, cvec], axis=1).reshape(2, 1, 2 * hd)
    w2 = jnp.pad(gla_gate_w2[l], ((0, LANE - GLA_RANK), (0, 0)))
    return dict(
        proj=[a.astype(BF16) for a in (wa, wb, wc, wd)],
        w_merge=seg(15, 16).astype(BF16), w_branch=w_branch[l].astype(BF16), w_out=w_out[l].astype(BF16),
        ffn_in=ffn_w_in[l].astype(BF16), ffn_out=ffn_w_out[l].astype(BF16),
        w_bd=w_bd.astype(BF16), w_bdt=w_bdt.astype(BF16), cconst=cconst, cconst2=cconst2,
        gla_w2=w2, gla_b=gla_gate_b[l].reshape(1, -1),
        gate_b=jnp.pad(nsa_gate_b[l].reshape(1, -1), ((0, 0), (0, LANE - 3 * B_HEADS))),
    )


def _mixer_common(x, lw, g2, vec_a, vec_d, bsz, t, state_a, state_d, br):
    a_dk = br // A_HEADS
    d_dk = br // (2 * D_HEADS)
    d_dv = br // D_HEADS
    ya, yb, yc, yd = _proj(x, g2, lw['proj'])
    o_a, st_a = _linear_mixer(functools.partial(_hgrn_body, w=br, dk=a_dk), ya, bsz, t, vec_a, state_a,
                              A_HEADS, a_dk, a_dk)
    o_d, st_d = _linear_mixer(functools.partial(_gla_body, wk=D_HEADS * d_dk, wv=br, dk=d_dk, dv=d_dv),
                              yd, bsz, t, vec_d, state_d, D_HEADS, d_dk, d_dv)
    return yb, yc, o_a, st_a, o_d, st_d


def _feature_major(cache):
    l, n, t = cache.shape[:3]
    return jnp.transpose(cache, (0, 1, 3, 4, 5, 2)).reshape(l, n, -1, t)


def kernel(x_prompt, x_sample, cache_nsa_kv, cache_nsa_win, cache_diff_kv, state_hgrn, state_gla, page_table,
           norm_gains, ffn_w_in, ffn_w_out, w_in, hgrn_lb_logits, hgrn_norm_gain, nsa_cmp_pe, nsa_cmp_w,
           nsa_gate_b, diff_lambda, diff_norm_gain, gla_gate_w2, gla_gate_b, gla_norm_gain, w_branch, w_out):
    bp, tp, d_model = x_prompt.shape
    bs, ts, _ = x_sample.shape
    depth = w_in.shape[0]
    br = d_model // N_BRANCH
    hd = br // B_HEADS
    cd2 = br // C_HEADS
    page = cache_nsa_kv.shape[2]
    n_pages = page_table.shape[1]
    past = n_pages * page
    nwin = cache_nsa_win.shape[2]
    assert past % NSA_BLOCK == 0 and ts <= DEC_ROWS and tp % NSA_BLOCK == 0
    assert past // NSA_BLOCK >= NSA_TOPK and page == 2 * NSA_BLOCK and nwin == NSA_WINDOW
    k_past = NSA_TOPK - 1
    tq = min(256, tp)
    tk = min(128, tq)

    lb_cum = jnp.cumsum(jax.nn.softmax(hgrn_lb_logits.astype(F32), axis=0), axis=0)
    lower = lb_cum - lb_cum[0]
    slopes = 2.0 ** (-np.arange(1, B_HEADS + C_HEADS + 1, dtype=np.float64))
    sl_b = jnp.asarray(slopes[0::2], F32)
    sl_c = jnp.asarray(slopes[1::2], F32)

    nsa_pool_t = _feature_major(cache_nsa_kv)
    diff_pool_t = _feature_major(cache_diff_kv)
    win_pool_t = _feature_major(cache_nsa_win)

    xp = x_prompt.reshape(bp * tp, d_model)
    xs = x_sample.reshape(bs * ts, d_model)
    outs = {k: [] for k in ('kvp', 'kvs', 'winp', 'wins', 'dkp', 'dks', 'hp', 'hs', 'gp', 'gs')}

    for l in range(depth):
        lw = _layer_weights(l, d_model, w_in, ffn_w_in, ffn_w_out, w_branch, w_out, nsa_cmp_pe, nsa_cmp_w,
                            gla_gate_w2, gla_gate_b, nsa_gate_b)
        g = norm_gains[l]
        lb = lower[l].reshape(1, br)
        vec_a = (jnp.log1p(-lb), jnp.log(lb), hgrn_norm_gain[l].reshape(1, br))
        vec_d = (lw['gla_w2'], lw['gla_b'], gla_norm_gain[l].reshape(1, br))
        lv = diff_lambda[l].astype(F32)
        lam_init = 0.8 - 0.6 * math.exp(-0.3 * l)
        lam = (jnp.exp(jnp.sum(lv[0] * lv[1])) - jnp.exp(jnp.sum(lv[2] * lv[3])) + lam_init).reshape(1)

        xp = _ffn(xp, g[0], g[1], lw['ffn_in'][0], lw['ffn_out'][0])
        yb, yc, o_a, st_a, o_d, st_d = _mixer_common(xp, lw, g[2], vec_a, vec_d, bp, tp, None, None, br)
        outs['hp'].append(st_a)
        outs['gp'].append(st_d)
        kv6 = yb[:, br:br + 6 * B_KV_HEADS * hd].reshape(bp, tp, 6, B_KV_HEADS, hd)
        outs['kvp'].append(kv6[:, :, :4])
        outs['winp'].append(kv6[:, -min(NSA_WINDOW, tp):, 4:])
        outs['dkp'].append(yc[:, br:3 * br].reshape(bp, tp, 2, C_HEADS, cd2))

        o_c = _diff_prompt(yc, bp, tp, diff_norm_gain[l], lam, sl_c, lam_init, tq, tk)
        cmp_kv = _compress_prompt(yb, bp, tp, lw['w_bd'], lw['cconst'], hd)
        o_b = _nsa_prompt(yb, cmp_kv, bp, tp, lw['gate_b'], sl_b, tq, tk)

        xp = _merge(xp, g[2], g[3], (o_a, o_b, o_c, o_d), lw['w_merge'], lw['w_branch'], lw['w_out'])
        xp = _ffn(xp, g[4], g[5], lw['ffn_in'][1], lw['ffn_out'][1])

        xs = _ffn(xs, g[0], g[1], lw['ffn_in'][0], lw['ffn_out'][0])
        yb, yc, o_a, st_a, o_d, st_d = _mixer_common(xs, lw, g[2], vec_a, vec_d, bs, ts,
                                                     state_hgrn[l], state_gla[l], br)
        outs['hs'].append(st_a)
        outs['gs'].append(st_d)
        new_kv = yb[:, br:br + 6 * B_KV_HEADS * hd].reshape(bs, ts, 6, B_KV_HEADS * hd)
        outs['kvs'].append(new_kv[:, :, :4].reshape(bs, ts, 4, B_KV_HEADS, hd))
        new_win = new_kv[:, :, 4:].reshape(bs, ts, 2 * B_KV_HEADS * hd)
        outs['wins'].append(jnp.concatenate(
            [cache_nsa_win[l][:, ts:], new_win.reshape(bs, ts, 2, B_KV_HEADS, hd)], axis=1))
        outs['dks'].append(yc[:, br:3 * br].reshape(bs, ts, 2, C_HEADS, cd2))
        pad_new = lambda a: jnp.pad(a, ((0, 0), (0, NEW_PAD - ts), (0, 0)))

        cq = yc[:, 0:br].reshape(bs, ts, C_HEADS, 2, cd2 // 2)
        sel = (jnp.arange(C_HEADS)[:, None, None, None] == jnp.arange(C_HEADS)[None, None, :, None]) & \
              (jnp.arange(2)[None, :, None, None] == jnp.arange(2)[None, None, None, :])
        q_rows = jnp.einsum('bthmd,hmgn->bmthgnd', cq, sel.astype(F32)).reshape(bs, 2 * ts * C_HEADS, br)
        new_c = pad_new(yc[:, br:3 * br].reshape(bs, ts, 2 * br))
        o_c = _diff_decode(q_rows, new_c, diff_pool_t, l, page_table, diff_norm_gain[l], lam, lam_init, ts, past)
        o_c = o_c.reshape(bs * ts, br)

        bq = yb[:, 0:br].reshape(bs, ts, B_KV_HEADS, B_GROUP, hd).transpose(0, 2, 3, 1, 4)
        bq = jnp.pad(bq, ((0, 0), (0, 0), (0, 0), (0, DEC_ROWS - ts), (0, 0)))
        q_plain = bq.reshape(bs, B_KV_HEADS, B_GROUP * DEC_ROWS, hd)
        zero = jnp.zeros_like(q_plain)
        q_half = jnp.stack([jnp.concatenate([q_plain, zero], axis=-1),
                            jnp.concatenate([zero, q_plain], axis=-1)], axis=2)
        gl = yb[:, br + 6 * B_KV_HEADS * hd:].reshape(bs, ts, LANE)
        gl = jnp.pad(gl, ((0, 0), (0, DEC_ROWS - ts), (0, 0)))
        o_b = _nsa_decode(q_half, q_plain, win_pool_t, pad_new(new_win),
                          pad_new(new_kv[:, :, 2:4].reshape(bs, ts, 4 * hd)), nsa_pool_t, l, page_table,
                          lw['w_bdt'], lw['cconst2'], gl, lw['gate_b'], ts, past, k_past)
        o_b = o_b[:, :, :, :ts].transpose(0, 3, 1, 2, 4).reshape(bs * ts, br)

        xs = _merge(xs, g[2], g[3], (o_a, o_b, o_c, o_d), lw['w_merge'], lw['w_branch'], lw['w_out'])
        xs = _ffn(xs, g[4], g[5], lw['ffn_in'][1], lw['ffn_out'][1])

    st = lambda k: jnp.stack(outs[k])
    return (xp.reshape(bp, tp, d_model), xs.reshape(bs, ts, d_model), st('kvp'), st('kvs'), st('winp'), st('wins'),
            st('dkp'), st('dks'), st('hp'), st('hs'), st('gp'), st('gs'))
```

```python
import functools
import math

import numpy as np
import jax
import jax.numpy as jnp
from jax import lax
from jax.experimental import pallas as pl
from jax.experimental.pallas import tpu as pltpu

F32 = jnp.float32
BF16 = jnp.bfloat16
HI = lax.Precision.HIGHEST

N_BRANCH = 4
A_HEADS = 4
B_HEADS = 4
B_KV_HEADS = 2
B_GROUP = 2
C_HEADS = 4
D_HEADS = 4
NSA_BLOCK = 64
NSA_TOPK = 16
NSA_WINDOW = 512
GLA_RANK = 16
GLA_TAU = 16.0
NORM_EPS = 1e-6
LANE = 128
LIN_GROUP = 16
NEW_PAD = 16
DEC_ROWS = 8
ONES_PAD = 16
PAGE_PITCH_PAD = 8
VMEM_LIMIT = 56 * 1024 * 1024
LOG2E = 1.4426950408889634
NEG_BIG = -1e30

NT_DIMS = (((1,), (1,)), ((), ()))
TN_DIMS = (((0,), (0,)), ((), ()))


def _params(semantics):
    return pltpu.CompilerParams(dimension_semantics=semantics, vmem_limit_bytes=VMEM_LIMIT)


def _rms(x, g):
    return x * lax.rsqrt(jnp.mean(x * x, axis=-1, keepdims=True) + NORM_EPS) * g


def _dot(a, b):
    return jnp.dot(a, b, preferred_element_type=F32)


def _dot_nt(a, b):
    return lax.dot_general(a, b, NT_DIMS, preferred_element_type=F32)


def _row_tile(n, cap):
    t = min(n, cap)
    while n % t or t % 8:
        t -= 1
    return t


def _ffn_body(x_ref, gpre_ref, gpost_ref, wg_ref, wu_ref, wo_ref, o_ref, xn_ref, acc_ref):
    f = pl.program_id(1)

    @pl.when(f == 0)
    def _():
        xn_ref[...] = _rms(x_ref[...], gpre_ref[...]).astype(BF16)
        acc_ref[...] = jnp.zeros_like(acc_ref)

    xn = xn_ref[...]
    gate = _dot(xn, wg_ref[...])
    up = _dot(xn, wu_ref[...])
    act = (gate * jax.nn.sigmoid(gate) * up).astype(BF16)
    acc_ref[...] += _dot(act, wo_ref[...])

    @pl.when(f == pl.num_programs(1) - 1)
    def _():
        o_ref[...] = x_ref[...] + 0.5 * _rms(acc_ref[...], gpost_ref[...])


def _ffn(x, g_pre, g_post, w_in, w_out):
    n, d = x.shape
    dff = w_out.shape[0]
    tf = 256
    nf = dff // tf
    tm = _row_tile(n, 1024)
    return pl.pallas_call(
        _ffn_body,
        grid=(n // tm, nf),
        in_specs=[
            pl.BlockSpec((tm, d), lambda i, f: (i, 0)),
            pl.BlockSpec((1, d), lambda i, f: (0, 0)),
            pl.BlockSpec((1, d), lambda i, f: (0, 0)),
            pl.BlockSpec((d, tf), lambda i, f: (0, f)),
            pl.BlockSpec((d, tf), lambda i, f: (0, nf + f)),
            pl.BlockSpec((tf, d), lambda i, f: (f, 0)),
        ],
        out_specs=pl.BlockSpec((tm, d), lambda i, f: (i, 0)),
        out_shape=jax.ShapeDtypeStruct((n, d), F32),
        scratch_shapes=[pltpu.VMEM((tm, d), BF16), pltpu.VMEM((tm, d), F32)],
        compiler_params=_params(("parallel", "arbitrary")),
        name="ffn",
    )(x, g_pre.reshape(1, d), g_post.reshape(1, d), w_in, w_in, w_out)


def _proj_body(x_ref, g_ref, wa, wb, wc, wd, oa, ob, oc, od, *cache_t, br):
    xn = _rms(x_ref[...], g_ref[...]).astype(BF16)
    ys = [_dot(xn, w[...]) for w in (wa, wb, wc, wd)]
    for y, o in zip(ys, (oa, ob, oc, od)):
        o[...] = y
    if cache_t:
        kv_t, win_t, dk_t = cache_t
        kv_t[...] = ys[1][:, br:3 * br].T
        win_t[...] = ys[1][:, 3 * br:4 * br].T
        dk_t[...] = ys[2][:, br:3 * br].T


def _proj(x, g, ws, br, seqs=None):
    n, d = x.shape
    tm = _row_tile(n, 512)
    out_specs = [pl.BlockSpec((tm, w.shape[1]), lambda i: (i, 0)) for w in ws]
    out_shape = [jax.ShapeDtypeStruct((n, w.shape[1]), F32) for w in ws]
    if seqs is not None:
        bsz, t = seqs
        per = t // tm
        for rows in (2 * br, br, 2 * br):
            out_specs.append(pl.BlockSpec((None, rows, tm), lambda i: (i // per, 0, i % per)))
            out_shape.append(jax.ShapeDtypeStruct((bsz, rows, t), F32))
    return pl.pallas_call(
        functools.partial(_proj_body, br=br),
        grid=(n // tm,),
        in_specs=[pl.BlockSpec((tm, d), lambda i: (i, 0)), pl.BlockSpec((1, d), lambda i: (0, 0))]
        + [pl.BlockSpec(w.shape, lambda i: (0, 0)) for w in ws],
        out_specs=out_specs,
        out_shape=out_shape,
        compiler_params=_params(("parallel",)),
        name="mixer_in_proj",
    )(x, g.reshape(1, d), *ws)


def _merge_body(x_ref, g2_ref, g3_ref, oa, ob, oc, od, wm_ref, wb_ref, wo_ref, out_ref):
    x = x_ref[...]
    d = x.shape[1]
    h = _rms(x, g2_ref[...]).astype(BF16)
    s = None
    for n, br in enumerate((oa, ob, oc, od)):
        gate = jax.nn.sigmoid(_dot(h, wm_ref[:, n * d:(n + 1) * d]))
        term = gate * _dot(br[...].astype(BF16), wb_ref[n])
        s = term if s is None else s + term
    y = _dot(s.astype(BF16), wo_ref[...])
    out_ref[...] = x + _rms(y, g3_ref[...])


def _merge(x, g2, g3, branches, w_merge, w_branch, w_out):
    n, d = x.shape
    br = branches[0].shape[1]
    tm = _row_tile(n, 256)
    row = lambda i: (i, 0)
    fix2 = lambda i: (0, 0)
    return pl.pallas_call(
        _merge_body,
        grid=(n // tm,),
        in_specs=[pl.BlockSpec((tm, d), row), pl.BlockSpec((1, d), fix2), pl.BlockSpec((1, d), fix2)]
        + [pl.BlockSpec((tm, br), row)] * N_BRANCH
        + [pl.BlockSpec(w_merge.shape, fix2), pl.BlockSpec(w_branch.shape, lambda i: (0, 0, 0)),
           pl.BlockSpec(w_out.shape, fix2)],
        out_specs=pl.BlockSpec((tm, d), row),
        out_shape=jax.ShapeDtypeStruct((n, d), F32),
        compiler_params=_params(("parallel",)),
        name="merge",
    )(x, g2.reshape(1, d), g3.reshape(1, d), *branches, w_merge, w_branch, w_out)


def _softplus_neg_abs(z):
    return jnp.log(1.0 + jnp.exp(-jnp.abs(z)))


def _log_sigmoid(z):
    return jnp.minimum(z, 0.0) - _softplus_neg_abs(z)


def _dot_exact_lhs(a, x):
    hi = x.astype(BF16)
    lo = (x - hi.astype(F32)).astype(BF16)
    return _dot(a, hi) + _dot(a, lo)


def _lin_core(q, k, v, lf, gg, gain, s0_ref, o_ref, st_ref, qt_s, kt_s, vt_s, dec_s, oi_s, st_s,
              *, dk, dv, t_valid, n_seq):
    rows, wk = q.shape
    tt = rows // n_seq
    wv = v.shape[1]
    grp = LIN_GROUP
    t_idx = pl.program_id(1)

    @pl.when(t_idx == 0)
    def _():
        st_s[...] = s0_ref[...]

    if t_valid is not None:
        tok = t_idx * tt + lax.broadcasted_iota(jnp.int32, (rows, 1), 0) % tt
        lf = jnp.where(tok < t_valid, lf, 0.0)

    r = lax.broadcasted_iota(jnp.int32, (tt, tt), 0)
    c = lax.broadcasted_iota(jnp.int32, (tt, tt), 1)
    same = (r // grp) == (c // grp)
    tri = jnp.where(same & (c <= r), 1.0, 0.0).astype(BF16)
    ones_g = jnp.where(same, 1.0, 0.0).astype(BF16)
    lfs = [lf[si * tt:(si + 1) * tt] for si in range(n_seq)]
    b = jnp.concatenate([_dot_exact_lhs(tri, x) for x in lfs], axis=0)
    bl = jnp.concatenate([_dot_exact_lhs(ones_g, x) for x in lfs], axis=0)

    qt_s[...] = (q * jnp.exp(b)).astype(BF16)
    kt_s[...] = (k * jnp.exp(bl - b)).astype(BF16)
    vt_s[...] = v.astype(BF16)
    dec_s[...] = jnp.exp(bl)

    hk = lax.broadcasted_iota(jnp.int32, (wk, wv), 0) // dk
    hv = lax.broadcasted_iota(jnp.int32, (wk, wv), 1) // dv
    ones_hd = jnp.where(hk == hv, 1.0, 0.0).astype(BF16)

    rowm = lax.broadcasted_iota(jnp.int32, (rows, 1), 0) % grp
    od = jnp.zeros((rows, wv), F32)
    for d in range(grp):
        ks = k if d == 0 else pltpu.roll(k, d, 0)
        bs = b if d == 0 else pltpu.roll(b, d, 0)
        vs = v if d == 0 else pltpu.roll(v, d, 0)
        e = jnp.exp(jnp.minimum(b - bs, 0.0))
        z = jnp.where(rowm >= d, q * ks * e, 0.0).astype(BF16)
        od = od + _dot(z, ones_hd) * vs

    mv = lax.broadcasted_iota(jnp.int32, (wv, wk), 0) // dv
    mk = lax.broadcasted_iota(jnp.int32, (wv, wk), 1) // dk
    mbd = jnp.where(mv == mk, 1.0, 0.0).astype(F32)

    def step(i, carry):
        for si in range(n_seq):
            r0 = pl.multiple_of(si * tt + i * grp, grp)
            qg = qt_s[pl.ds(r0, grp), :]
            kg = kt_s[pl.ds(r0, grp), :]
            vg = vt_s[pl.ds(r0, grp), :]
            s = st_s[si]
            oi_s[pl.ds(r0, grp), :] = _dot_nt(qg, s.astype(BF16))
            upd = lax.dot_general(vg, kg, TN_DIMS, preferred_element_type=F32)
            st_s[si] = dec_s[pl.ds(r0, 1), :] * s + mbd * upd
        return carry

    lax.fori_loop(0, tt // grp, step, 0)

    o = oi_s[...] + od
    pv = lax.broadcasted_iota(jnp.int32, (wv, wv), 0) // dv
    pw = lax.broadcasted_iota(jnp.int32, (wv, wv), 1) // dv
    avg = jnp.where(pv == pw, 1.0 / dv, 0.0).astype(BF16)
    o2 = o * o
    o2_hi = o2.astype(BF16)
    ms = _dot(o2_hi, avg) + _dot((o2 - o2_hi.astype(F32)).astype(BF16), avg)
    y = o * lax.rsqrt(ms + NORM_EPS) * gain * (gg * jax.nn.sigmoid(gg))
    o_ref[...] = y.reshape(o_ref.shape)

    @pl.when(t_idx == pl.num_programs(1) - 1)
    def _():
        st_ref[...] = st_s[...]


def _hgrn_body(y_ref, la_ref, lc_ref, gain_ref, s0_ref, o_ref, st_ref, *scratch, w, dk, t_valid):
    n_seq = y_ref.shape[0]
    y = y_ref[...].reshape(-1, y_ref.shape[2])
    q = y[:, 0:w] * (dk ** -0.5)
    z = y[:, w:2 * w]
    v = y[:, 2 * w:3 * w]
    gg = y[:, 3 * w:4 * w]
    ls = _log_sigmoid(z)
    a = la_ref[...] + ls
    cc = lc_ref[...]
    lf = jnp.maximum(a, cc) + _softplus_neg_abs(a - cc)
    k = jnp.exp(la_ref[...] + (ls - z))
    _lin_core(q, k, v, lf, gg, gain_ref[...], s0_ref, o_ref, st_ref, *scratch, dk=dk, dv=dk, t_valid=t_valid,
              n_seq=n_seq)


def _gla_body(y_ref, w2_ref, b2_ref, gain_ref, s0_ref, o_ref, st_ref, *scratch, wk, wv, dk, dv, t_valid):
    n_seq = y_ref.shape[0]
    y = y_ref[...].reshape(-1, y_ref.shape[2])
    q = y[:, 0:wk] * (dk ** -0.5)
    k = y[:, wk:2 * wk]
    v = y[:, 2 * wk:2 * wk + wv]
    lr = y[:, 2 * wk + wv:2 * wk + wv + LANE]
    gg = y[:, 2 * wk + wv + LANE:2 * wk + 2 * wv + LANE]
    u = jnp.dot(lr, w2_ref[...], precision=HI, preferred_element_type=F32) + b2_ref[...]
    lf = _log_sigmoid(u) * (1.0 / GLA_TAU)
    _lin_core(q, k, v, lf, gg, gain_ref[...], s0_ref, o_ref, st_ref, *scratch, dk=dk, dv=dv, t_valid=t_valid,
              n_seq=n_seq)


def _lin_call(body, y, n_seq, tt, vecs, s0):
    bsz, t, cw = y.shape
    wv, wk = s0.shape[1:]
    rows = n_seq * tt
    return pl.pallas_call(
        body,
        grid=(bsz // n_seq, t // tt),
        in_specs=[pl.BlockSpec((n_seq, tt, cw), lambda b, i: (b, i, 0))]
        + [pl.BlockSpec(a.shape, lambda b, i: (0, 0)) for a in vecs]
        + [pl.BlockSpec((n_seq, wv, wk), lambda b, i: (b, 0, 0))],
        out_specs=[pl.BlockSpec((n_seq, tt, wv), lambda b, i: (b, i, 0)),
                   pl.BlockSpec((n_seq, wv, wk), lambda b, i: (b, 0, 0))],
        out_shape=[jax.ShapeDtypeStruct((bsz, t, wv), F32), jax.ShapeDtypeStruct((bsz, wv, wk), F32)],
        scratch_shapes=[pltpu.VMEM((rows, wk), BF16), pltpu.VMEM((rows, wk), BF16), pltpu.VMEM((rows, wv), BF16),
                        pltpu.VMEM((rows, wk), F32), pltpu.VMEM((rows, wv), F32), pltpu.VMEM((n_seq, wv, wk), F32)],
        compiler_params=_params(("parallel", "arbitrary")),
        name="gated_linear",
    )(y, *vecs, s0)


def _state_to_bd(state):
    bsz, nh, dk, dv = state.shape
    eye = jnp.eye(nh, dtype=state.dtype)
    return jnp.einsum('bhkv,hg->bhvgk', state, eye).reshape(bsz, nh * dv, nh * dk)


def _bd_to_state(st, nh):
    bsz, wv, wk = st.shape
    dv, dk = wv // nh, wk // nh
    blocks = st.reshape(bsz, nh, dv, nh, dk)
    idx = jnp.arange(nh)
    return blocks[:, idx, :, idx, :].transpose(1, 0, 3, 2)


def _linear_mixer(body, y, bsz, t, vecs, state0, nh, dk, dv):
    assert dv & (dv - 1) == 0
    tp = -(-t // LIN_GROUP) * LIN_GROUP
    y3 = y.reshape(bsz, t, -1)
    if tp != t:
        y3 = jnp.pad(y3, ((0, 0), (0, tp - t), (0, 0)))
    tt = min(tp, LANE)
    n_seq = math.gcd(bsz, min(8, max(1, 512 // tt)))
    wk, wv = nh * dk, nh * dv
    s0 = jnp.zeros((bsz, wv, wk), F32) if state0 is None else _state_to_bd(state0)
    body = functools.partial(body, t_valid=None if tp == t else t)
    o, st = _lin_call(body, y3, n_seq, tt, vecs, s0)
    return o[:, :t].reshape(bsz * t, wv), _bd_to_state(st, nh)


def _col_update(idx, tiles, m_s, acc_s):
    m_old = m_s[idx]
    m_new = m_old
    for s, _, shift in tiles:
        smax = jnp.max(s, axis=0, keepdims=True)
        m_new = jnp.maximum(m_new, smax if shift is None else smax + shift)
    m_safe = jnp.where(m_new == -jnp.inf, 0.0, m_new)
    pv = None
    for s, vt, shift in tiles:
        p = jnp.exp2(s + ((-m_safe) if shift is None else (shift - m_safe)))
        d = _dot(vt, p.astype(BF16))
        pv = d if pv is None else pv + d
    acc_s[idx] = jnp.exp2(m_old - m_safe) * acc_s[idx] + pv
    m_s[idx] = m_new


def _col_finish(acc, dv):
    den = acc[dv:dv + 1, :]
    return acc[:dv, :] / jnp.where(den > 0, den, 1.0)


def _col_reset(m_s, acc_s):
    m_s[...] = jnp.full(m_s.shape, -jnp.inf, F32)
    acc_s[...] = jnp.zeros(acc_s.shape, F32)


def _online(s, v, m, l, a, v_t=False):
    m_new = jnp.maximum(m, jnp.max(s, axis=1, keepdims=True))
    m_safe = jnp.where(m_new == -jnp.inf, 0.0, m_new)
    p = jnp.exp2(s - m_safe)
    alpha = jnp.exp2(m - m_safe)
    l = alpha * l + jnp.sum(p, axis=1, keepdims=True)
    pv = _dot_nt(p.astype(BF16), v) if v_t else _dot(p.astype(BF16), v)
    return m_new, l, alpha * a + pv


def _finish(l, a):
    return a / jnp.where(l > 0, l, 1.0)


def _softmax_init(rows, width):
    return (jnp.full((rows, 1), -jnp.inf, F32), jnp.zeros((rows, 1), F32), jnp.zeros((rows, width), F32))


def _stage_keys_values(k_ref, v_ref, kb_s, vt_s, *, t, tk, w):
    kb_s[...] = k_ref[...].astype(BF16)
    n_heads = vt_s.shape[0]
    ones = jnp.where(lax.broadcasted_iota(jnp.int32, (ONES_PAD, tk), 0) == 0, 1.0, 0.0).astype(BF16)

    def body(jt, carry):
        vt = v_ref[pl.ds(pl.multiple_of(jt * tk, tk), tk), :].T
        for hh in range(n_heads):
            vt_s[hh, jt, 0:w, :] = vt[hh * w:(hh + 1) * w, :].astype(BF16)
            vt_s[hh, jt, w:w + ONES_PAD, :] = ones
        return carry

    lax.fori_loop(0, t // tk, body, 0)


def _place_rows(x, lo, total):
    r, n = x.shape
    parts = []
    if lo:
        parts.append(jnp.zeros((lo, n), x.dtype))
    parts.append(x)
    if total - lo - r:
        parts.append(jnp.zeros((total - lo - r, n), x.dtype))
    return jnp.concatenate(parts, axis=0) if len(parts) > 1 else x


def _diff_prompt_body(sl_ref, lam_ref, q_ref, k_ref, v_ref, gain_ref, o_ref, kb_s, vt_s, boff_s, m_s, acc_s,
                      *, t, tq, tk, cd, lam_init):
    hp = pl.program_id(1)
    i = pl.program_id(2)
    w = 2 * cd
    pair = LANE // w

    @pl.when(i == 0)
    def _():
        _stage_keys_values(k_ref, v_ref, kb_s, vt_s, t=t, tk=tk, w=w)

    qt = (q_ref[...] * (cd ** -0.5 * LOG2E)).T
    slope2 = [sl_ref[hp * pair + hh] * LOG2E for hh in range(pair)]
    chains = [(hh, _place_rows(qt[hh * w + mi * cd:hh * w + (mi + 1) * cd, :], hh * w + mi * cd, LANE).astype(BF16))
              for hh in range(pair) for mi in range(2)]
    rel = lax.broadcasted_iota(jnp.int32, (tk, tq), 1) - lax.broadcasted_iota(jnp.int32, (tk, tq), 0)
    rel_f = rel.astype(F32)
    for hh in range(pair):
        boff_s[hh] = rel_f * (-slope2[hh])
    _col_reset(m_s, acc_s)
    ratio = tq // tk

    def key_tiles(jj):
        return [kb_s[pl.ds(pl.multiple_of((jj * ratio + r) * tk, tk), tk), :] for r in range(ratio)]

    def off_group(jj, carry):
        kts = key_tiles(jj)
        for ci, (hh, qc) in enumerate(chains):
            boff = boff_s[hh]
            _col_update(ci, [(_dot(kts[r], qc) + boff, vt_s[hh, jj * ratio + r],
                              ((i - jj) * tq - r * tk).astype(F32) * (-slope2[hh])) for r in range(ratio)],
                        m_s, acc_s)
        return carry

    lax.fori_loop(0, i, off_group, 0)
    kts = key_tiles(i)
    dists = [rel - r * tk for r in range(ratio)]
    for ci, (hh, qc) in enumerate(chains):
        _col_update(ci, [(_dot(kts[r], qc) + jnp.where(dists[r] >= 0, dists[r].astype(F32) * (-slope2[hh]), -jnp.inf),
                          vt_s[hh, i * ratio + r], None) for r in range(ratio)], m_s, acc_s)

    outs = []
    for hh in range(pair):
        o = _col_finish(acc_s[2 * hh], w) - lam_ref[0] * _col_finish(acc_s[2 * hh + 1], w)
        ms = jnp.mean(o * o, axis=0, keepdims=True)
        outs.append(o * lax.rsqrt(ms + NORM_EPS))
    o_ref[...] = (jnp.concatenate(outs, axis=0) * gain_ref[...] * (1.0 - lam_init)).T


def _diff_prompt(yc, bsz, t, gain, lam, slopes, lam_init, tq, tk):
    w = gain.shape[0]
    nh = yc.shape[1] // (3 * w)
    pair = LANE // w
    nhp = nh // pair
    nq = t // tq
    dva = w + ONES_PAD
    smem = pl.BlockSpec(memory_space=pltpu.SMEM)
    return pl.pallas_call(
        functools.partial(_diff_prompt_body, t=t, tq=tq, tk=tk, cd=w // 2, lam_init=lam_init),
        grid=(bsz, nhp, nq),
        in_specs=[smem, smem,
                  pl.BlockSpec((tq, LANE), lambda b, hp, i: (b * nq + i, hp)),
                  pl.BlockSpec((t, LANE), lambda b, hp, i: (b, nhp + hp)),
                  pl.BlockSpec((t, LANE), lambda b, hp, i: (b, 2 * nhp + hp)),
                  pl.BlockSpec((LANE, 1), lambda b, hp, i: (0, 0))],
        out_specs=pl.BlockSpec((tq, LANE), lambda b, hp, i: (b * nq + i, hp)),
        out_shape=jax.ShapeDtypeStruct((bsz * t, nh * w), F32),
        scratch_shapes=[pltpu.VMEM((t, LANE), BF16), pltpu.VMEM((pair, t // tk, dva, tk), BF16),
                        pltpu.VMEM((pair, tk, tq), F32), pltpu.VMEM((2 * pair, 1, tq), F32),
                        pltpu.VMEM((2 * pair, dva, tq), F32)],
        compiler_params=_params(("parallel", "parallel", "arbitrary")),
        name="diff_attn_prompt",
    )(slopes, lam, yc, yc, yc, jnp.tile(gain, pair).reshape(LANE, 1))


def _compress(xk_ref, xv_ref, w_ref, nb):
    def body(j, acc):
        ak, av = acc
        xk = xk_ref[pl.ds(j, nb, stride=NSA_BLOCK), :].astype(BF16)
        xv = xv_ref[pl.ds(j, nb, stride=NSA_BLOCK), :].astype(BF16)
        return ak + _dot(xk, w_ref[0, j]), av + _dot(xv, w_ref[1, j])

    zero = jnp.zeros((nb, xk_ref.shape[1]), F32)
    ak, av = lax.fori_loop(0, NSA_BLOCK, body, (zero, zero))
    return jnp.concatenate([ak, av], axis=1)


def _compress_body(xk_ref, xv_ref, w_ref, c_ref, o_ref, *, nb, hd):
    acc = _compress(xk_ref, xv_ref, w_ref, nb) + c_ref[...]
    for p in range(4):
        o_ref[p] = acc[:, p * hd:(p + 1) * hd]


def _compress_prompt(yb, bsz, t, w_bd, cconst, hd):
    nb = t // NSA_BLOCK
    cw = 4 * hd
    hw = cw // 2
    return pl.pallas_call(
        functools.partial(_compress_body, nb=nb, hd=hd),
        grid=(bsz,),
        in_specs=[pl.BlockSpec((t, hw), lambda b: (b, 2)),
                  pl.BlockSpec((t, hw), lambda b: (b, 3)),
                  pl.BlockSpec(w_bd.shape, lambda b: (0, 0, 0, 0)),
                  pl.BlockSpec((1, cw), lambda b: (0, 0))],
        out_specs=pl.BlockSpec((None, 4, nb, hd), lambda b: (b, 0, 0, 0)),
        out_shape=jax.ShapeDtypeStruct((bsz, 4, nb, hd), F32),
        compiler_params=_params(("parallel",)),
        name="nsa_compress_prompt",
    )(yb, yb, w_bd, cconst)


def _rank_rows(score, tq):
    nb = score.shape[0]
    sub = 8
    slabs = [score[v * sub:(v + 1) * sub, :] for v in range(nb // sub)]
    ranks = [jnp.zeros((sub, tq), F32) for _ in slabs]
    sub_i = lax.broadcasted_iota(jnp.int32, (sub, 1), 0)
    for ib in range(nb):
        row = score[ib:ib + 1, :]
        for v, slab in enumerate(slabs):
            if ib < v * sub:
                ahead = jnp.where(row >= slab, 1.0, 0.0)
            elif ib >= (v + 1) * sub:
                ahead = jnp.where(row > slab, 1.0, 0.0)
            else:
                ahead = jnp.where(sub_i > ib - v * sub, jnp.where(row >= slab, 1.0, 0.0),
                                  jnp.where(row > slab, 1.0, 0.0))
            ranks[v] = ranks[v] + ahead
    return jnp.concatenate(ranks, axis=0)


def _nsa_prompt_body(sl_ref, q_ref, cmp_ref, ks_ref, vs_ref, kw_ref, vw_ref, gl_ref, gb_ref, o_ref,
                     ksb_s, kwb_s, vst_s, vwt_s, boff_s, m_s, acc_s, *, t, tq, tk, nb, hd, k_sel):
    i = pl.program_id(1)
    heads = [(g, n) for g in range(B_KV_HEADS) for n in range(B_GROUP)]
    nh = len(heads)

    @pl.when(i == 0)
    def _():
        _stage_keys_values(ks_ref, vs_ref, ksb_s, vst_s, t=t, tk=tk, w=hd)
        _stage_keys_values(kw_ref, vw_ref, kwb_s, vwt_s, t=t, tk=tk, w=hd)

    qt = (q_ref[...] * (hd ** -0.5 * LOG2E)).T
    qh = [qt[hi * hd:(hi + 1) * hd, :].astype(BF16) for hi in range(nh)]
    qk = [_place_rows(qh[hi], g * hd, LANE) for hi, (g, _) in enumerate(heads)]
    sl2 = [sl_ref[hi] * LOG2E for hi in range(nh)]
    qpos = i * tq + lax.broadcasted_iota(jnp.int32, (1, tq), 1)

    blk = lax.broadcasted_iota(jnp.int32, (nb, 1), 0)
    distc = qpos - (blk * NSA_BLOCK + (NSA_BLOCK - 1))
    distc_f = distc.astype(F32)
    cur = qpos // NSA_BLOCK
    forced = (blk == 0) | (blk == cur) | (blk == cur - 1)
    o_cmp = []
    selneg = []
    for g in range(B_KV_HEADS):
        kc = cmp_ref[g].astype(BF16)
        vc = cmp_ref[B_KV_HEADS + g].astype(BF16)
        imp = None
        for n in range(B_GROUP):
            hi = g * B_GROUP + n
            s = jnp.where(distc >= 0, _dot(kc, qh[hi]) - sl2[hi] * distc_f, -jnp.inf)
            m = jnp.max(s, axis=0, keepdims=True)
            e = jnp.exp2(s - jnp.where(m == -jnp.inf, 0.0, m))
            den = jnp.sum(e, axis=0, keepdims=True)
            p = e / jnp.where(den > 0, den, 1.0)
            o_cmp.append(lax.dot_general(vc, p.astype(BF16), TN_DIMS, preferred_element_type=F32))
            imp = p if imp is None else imp + p
        score = jnp.where(blk > cur, -jnp.inf, jnp.where(forced, jnp.inf, imp))
        selneg.append(jnp.where(_rank_rows(score, tq) < k_sel, 0.0, NEG_BIG).astype(BF16))

    rel = lax.broadcasted_iota(jnp.int32, (tk, tq), 1) - lax.broadcasted_iota(jnp.int32, (tk, tq), 0)
    rel_f = rel.astype(F32)
    for hi in range(nh):
        boff_s[hi] = rel_f * (-sl2[hi])
    ratio = tq // tk
    erow = lax.broadcasted_iota(jnp.int32, (tk, nb), 0)
    eblk = lax.broadcasted_iota(jnp.int32, (tk, nb), 1)

    def block_masks(j):
        expand = jnp.where((erow + j * tk) // NSA_BLOCK == eblk, 1.0, 0.0).astype(BF16)
        return [_dot(expand, sn) for sn in selneg]

    _col_reset(m_s, acc_s)

    def key_tiles(kb_s, jj):
        return [kb_s[pl.ds(pl.multiple_of((jj * ratio + r) * tk, tk), tk), :] for r in range(ratio)]

    def sel_off(jj, carry):
        kts = key_tiles(ksb_s, jj)
        mbs = [block_masks(jj * ratio + r) for r in range(ratio)]
        for hi, (g, _) in enumerate(heads):
            boff = boff_s[hi]
            _col_update(hi, [(_dot(kts[r], qk[hi]) + boff + mbs[r][g], vst_s[g, jj * ratio + r],
                              ((i - jj) * tq - r * tk).astype(F32) * (-sl2[hi])) for r in range(ratio)], m_s, acc_s)
        return carry

    lax.fori_loop(0, i, sel_off, 0)
    kts = key_tiles(ksb_s, i)
    mbs = [block_masks(i * ratio + r) for r in range(ratio)]
    dists = [rel - r * tk for r in range(ratio)]
    for hi, (g, _) in enumerate(heads):
        _col_update(hi, [(_dot(kts[r], qk[hi]) + mbs[r][g]
                          + jnp.where(dists[r] >= 0, dists[r].astype(F32) * (-sl2[hi]), -jnp.inf),
                          vst_s[g, i * ratio + r], None) for r in range(ratio)], m_s, acc_s)
    o_sel = [_col_finish(acc_s[hi], hd) for hi in range(nh)]

    _col_reset(m_s, acc_s)
    for ds in range(-(NSA_WINDOW // tq), 1):
        def win_span(ds=ds):
            kts = key_tiles(kwb_s, i + ds)
            dists = [rel - (ds * tq + r * tk) for r in range(ratio)]
            for hi, (g, _) in enumerate(heads):
                _col_update(hi, [(_dot(kts[r], qk[hi])
                                  + jnp.where((dists[r] >= 0) & (dists[r] < NSA_WINDOW),
                                              dists[r].astype(F32) * (-sl2[hi]), -jnp.inf),
                                  vwt_s[g, (i + ds) * ratio + r], None) for r in range(ratio)], m_s, acc_s)

        if ds < 0:
            pl.when(i + ds >= 0)(win_span)
        else:
            win_span()
    o_win = [_col_finish(acc_s[hi], hd) for hi in range(nh)]

    gates = jax.nn.sigmoid(gl_ref[...] + gb_ref[...]).T
    outs = [gates[3 * hi:3 * hi + 1] * o_cmp[hi] + gates[3 * hi + 1:3 * hi + 2] * o_sel[hi]
            + gates[3 * hi + 2:3 * hi + 3] * o_win[hi] for hi in range(nh)]
    o_ref[...] = jnp.concatenate(outs, axis=0).T


def _nsa_prompt(yb, cmp_kv, bsz, t, gate_b, slopes, tq, tk):
    nb, hd = cmp_kv.shape[2], cmp_kv.shape[3]
    qw = B_HEADS * hd
    nq = t // tq
    first = qw // LANE
    dva = hd + ONES_PAD
    kv_spec = lambda c: pl.BlockSpec((t, LANE), lambda b, i: (b, first + c))
    return pl.pallas_call(
        functools.partial(_nsa_prompt_body, t=t, tq=tq, tk=tk, nb=nb, hd=hd, k_sel=min(NSA_TOPK, nb)),
        grid=(bsz, nq),
        in_specs=[pl.BlockSpec(memory_space=pltpu.SMEM),
                  pl.BlockSpec((tq, qw), lambda b, i: (b * nq + i, 0)),
                  pl.BlockSpec((None, 2 * B_KV_HEADS, nb, hd), lambda b, i: (b, 0, 0, 0)),
                  kv_spec(2), kv_spec(3), kv_spec(4), kv_spec(5),
                  pl.BlockSpec((tq, LANE), lambda b, i: (b * nq + i, first + 6)),
                  pl.BlockSpec((1, LANE), lambda b, i: (0, 0))],
        out_specs=pl.BlockSpec((tq, qw), lambda b, i: (b * nq + i, 0)),
        out_shape=jax.ShapeDtypeStruct((bsz * t, qw), F32),
        scratch_shapes=[pltpu.VMEM((t, LANE), BF16), pltpu.VMEM((t, LANE), BF16),
                        pltpu.VMEM((B_KV_HEADS, t // tk, dva, tk), BF16),
                        pltpu.VMEM((B_KV_HEADS, t // tk, dva, tk), BF16),
                        pltpu.VMEM((B_HEADS, tk, tq), F32), pltpu.VMEM((B_HEADS, 1, tq), F32),
                        pltpu.VMEM((B_HEADS, dva, tq), F32)],
        compiler_params=_params(("parallel", "arbitrary")),
        name="nsa_attn_prompt",
    )(slopes, yb, cmp_kv, yb, yb, yb, yb, yb, gate_b)


def _diff_decode_body(pt_ref, lam_ref, q_ref, new_ref, gain_ref, *refs, n_pp, page, past, nh, cd, lam_init, dec_t):
    pages = refs[:n_pp]
    o_ref = refs[n_pp]
    m_s, l_s, a_s = refs[n_pp + 1:]
    c = pl.program_id(1)
    w = 2 * cd * nh
    rows = 2 * dec_t * nh
    row = lax.broadcasted_iota(jnp.int32, (rows, 1), 0)
    rh = row % nh
    rt = (row % (dec_t * nh)) // nh
    slope2 = jnp.zeros((rows, 1), F32)
    for h in range(nh):
        slope2 = jnp.where(rh == h, 2.0 ** (-2 * (h + 1)) * LOG2E, slope2)
    qb = (q_ref[...] * (cd ** -0.5 * LOG2E)).astype(BF16)

    @pl.when(c == 0)
    def _():
        new = new_ref[...]
        kn = new[:, :w].astype(BF16)
        vn = new[:, w:].astype(BF16)
        col = lax.broadcasted_iota(jnp.int32, (1, new.shape[0]), 1)
        dist = rt - col
        s = jnp.where((dist >= 0) & (col < dec_t), _dot_nt(qb, kn) - slope2 * dist.astype(F32), -jnp.inf)
        m_s[...], l_s[...], a_s[...] = _online(s, vn, *_softmax_init(rows, w))

    kt = jnp.concatenate([pages[p][0:w, :].astype(BF16) for p in range(n_pp)], axis=1)
    vt = jnp.concatenate([pages[p][w:2 * w, :].astype(BF16) for p in range(n_pp)], axis=1)
    col = lax.broadcasted_iota(jnp.int32, (1, n_pp * page), 1)
    dist = (past + rt) - (c * (n_pp * page) + col)
    s = _dot(qb, kt) - slope2 * dist.astype(F32)
    m, l, a = _online(s, vt, m_s[...], l_s[...], a_s[...], v_t=True)
    m_s[...], l_s[...], a_s[...] = m, l, a

    @pl.when(c == pl.num_programs(1) - 1)
    def _():
        full = _finish(l, a)
        own = jnp.zeros((rows, 2 * cd), F32)
        for h in range(nh):
            own = jnp.where(rh == h, full[:, h * 2 * cd:(h + 1) * 2 * cd], own)
        half = rows // 2
        o = own[:half] - lam_ref[0] * own[half:]
        o_ref[...] = _rms(o, gain_ref[...]) * (1.0 - lam_init)


def _diff_decode(q_rows, new_kv, cache_t, layer, page_table, gain, lam, lam_init, dec_t, past):
    bsz, rows, w = q_rows.shape
    n_pages = page_table.shape[1]
    page = cache_t.shape[3]
    n_pp = math.gcd(16, n_pages)
    nh = C_HEADS
    cd = w // (2 * nh)

    def page_spec(p):
        return pl.BlockSpec((None, None, 2 * w, page),
                            lambda b, c, pt: (layer, pt[b * n_pages + c * n_pp + p], 0, 0))

    grid_spec = pltpu.PrefetchScalarGridSpec(
        num_scalar_prefetch=1,
        grid=(bsz, n_pages // n_pp),
        in_specs=[pl.BlockSpec(memory_space=pltpu.SMEM),
                  pl.BlockSpec((None, rows, w), lambda b, c, pt: (b, 0, 0)),
                  pl.BlockSpec((None, NEW_PAD, 2 * w), lambda b, c, pt: (b, 0, 0)),
                  pl.BlockSpec((1, 2 * cd), lambda b, c, pt: (0, 0))]
        + [page_spec(p) for p in range(n_pp)],
        out_specs=pl.BlockSpec((None, rows // 2, 2 * cd), lambda b, c, pt: (b, 0, 0)),
        scratch_shapes=[pltpu.VMEM((rows, 1), F32), pltpu.VMEM((rows, 1), F32), pltpu.VMEM((rows, w), F32)],
    )
    return pl.pallas_call(
        functools.partial(_diff_decode_body, n_pp=n_pp, page=page, past=past, nh=nh, cd=cd,
                          lam_init=lam_init, dec_t=dec_t),
        grid_spec=grid_spec,
        out_shape=jax.ShapeDtypeStruct((bsz, rows // 2, 2 * cd), F32),
        compiler_params=_params(("parallel", "arbitrary")),
        name="diff_attn_decode",
    )(page_table.reshape(-1), lam, q_rows, new_kv, gain.reshape(1, 2 * cd), *([cache_t] * n_pp))


def _nsa_decode_body(pt_ref, qh_ref, qd_ref, wc_ref, wn_ref, sn_ref, wbd_ref, cc_ref, gl_ref, gb_ref, *refs,
                     n_pp, page, past, hd, dec_t, n_pages, k_past, pitch):
    pages = refs[:n_pp]
    o_ref = refs[n_pp]
    x_s, ak_s, av_s, mtok_s, ocw_s, m_s, l_s, a_s = refs[n_pp + 1:]
    ph = pl.program_id(1)
    c = pl.program_id(2)
    last = pl.num_programs(2) - 1
    tp = DEC_ROWS
    rows = B_GROUP * tp
    row = lax.broadcasted_iota(jnp.int32, (rows, 1), 0)
    rt = row % tp
    scale2 = hd ** -0.5 * LOG2E
    qpos = past + rt

    def slope2(g):
        return jnp.where(row < tp, 2.0 ** (-(2 * (g * B_GROUP) + 1)) * LOG2E,
                         2.0 ** (-(2 * (g * B_GROUP + 1) + 1)) * LOG2E)

    def new_scores(qd, kn, sl):
        col = lax.broadcasted_iota(jnp.int32, (1, kn.shape[0]), 1)
        d = rt - col
        return jnp.where((d >= 0) & (col < dec_t), _dot_nt(qd, kn) - sl * d.astype(F32), -jnp.inf)

    @pl.when(ph == 0)
    def _():
        for p in range(n_pp):
            r0 = pl.multiple_of((c * n_pp + p) * pitch, 8)
            x_s[pl.ds(r0, 4 * hd), :] = pages[p][...]

    @pl.when((ph == 0) & (c == last))
    def _():
        unroll = 4

        def compress(cc):
            def body(du, acc):
                for u in range(unroll):
                    d = du * unroll + u
                    lhs = jnp.concatenate(
                        [x_s[pl.ds(cc * 2 * hd + g * hd + d, n_pages, stride=pitch), :]
                         for g in range(B_KV_HEADS)], axis=0).astype(BF16)
                    acc = acc + _dot(lhs, wbd_ref[cc, d])
                return acc

            return lax.fori_loop(0, hd // unroll, body, jnp.zeros((B_KV_HEADS * n_pages, 2 * hd), F32))

        ak_s[...] = compress(0)
        av_s[...] = compress(1)
        pgi = lax.broadcasted_iota(jnp.int32, (1, n_pages), 1)
        lane = lax.broadcasted_iota(jnp.int32, (1, page), 1)
        for g in range(B_KV_HEADS):
            sl = slope2(g)
            kc = (ak_s[g * n_pages:(g + 1) * n_pages, :] + cc_ref[0]).astype(BF16)
            vc = (av_s[g * n_pages:(g + 1) * n_pages, :] + cc_ref[1]).astype(BF16)
            ss = []
            for hf in range(2):
                qh = (qh_ref[g, hf] * scale2).astype(BF16)
                dist = qpos - ((2 * pgi + hf) * NSA_BLOCK + (NSA_BLOCK - 1))
                ss.append(jnp.where(dist >= 0, _dot_nt(qh, kc) - sl * dist.astype(F32), -jnp.inf))
            m = jnp.maximum(jnp.max(ss[0], axis=1, keepdims=True), jnp.max(ss[1], axis=1, keepdims=True))
            m = jnp.where(m == -jnp.inf, 0.0, m)
            es = [jnp.exp2(s - m) for s in ss]
            den = jnp.sum(es[0], axis=1, keepdims=True) + jnp.sum(es[1], axis=1, keepdims=True)
            den = jnp.where(den > 0, den, 1.0)
            ps = [e / den for e in es]
            full = [_dot(p.astype(BF16), vc) for p in ps]
            ocw_s[g, 0] = full[0][:, :hd] + full[1][:, hd:]

            imps = [p[:tp] + p[tp:] for p in ps]
            sc = [jnp.where(pgi == 0, jnp.inf, imps[0]), jnp.where(pgi == n_pages - 1, jnp.inf, imps[1])]
            rank = [jnp.zeros((tp, n_pages), F32), jnp.zeros((tp, n_pages), F32)]
            for pg in range(n_pages):
                for hf in range(2):
                    colv = sc[hf][:, pg:pg + 1]
                    for h2 in range(2):
                        tie = jnp.where(2 * pgi + h2 > 2 * pg + hf, 1.0, 0.0)
                        rank[h2] = rank[h2] + jnp.where(colv > sc[h2], 1.0, jnp.where(colv == sc[h2], tie, 0.0))
            sel = [jnp.where(r < k_past, 1.0, 0.0) for r in rank]
            sel = [jnp.concatenate([s_, s_], axis=0) for s_ in sel]
            for pg in range(n_pages):
                on = jnp.where(lane < NSA_BLOCK, sel[0][:, pg:pg + 1], sel[1][:, pg:pg + 1])
                dist = qpos - (pg * page + lane)
                lo = (pg % n_pp) * page
                mtok_s[g, pg // n_pp, :, lo:lo + page] = jnp.where(on > 0.5, dist.astype(F32) * (-sl), NEG_BIG)

            qd = (qd_ref[g] * scale2).astype(BF16)
            wkt = wc_ref[g * hd:(g + 1) * hd, :].astype(BF16)
            wvt = wc_ref[(B_KV_HEADS + g) * hd:(B_KV_HEADS + g + 1) * hd, :].astype(BF16)
            nwin = wkt.shape[1]
            d1 = (nwin + rt) - lax.broadcasted_iota(jnp.int32, (1, nwin), 1)
            s1 = jnp.where((d1 >= 0) & (d1 < NSA_WINDOW), _dot(qd, wkt) - sl * d1.astype(F32), -jnp.inf)
            wn = wn_ref[...]
            kn = wn[:, g * hd:(g + 1) * hd].astype(BF16)
            vn = wn[:, (B_KV_HEADS + g) * hd:(B_KV_HEADS + g + 1) * hd].astype(BF16)
            s2 = new_scores(qd, kn, sl)
            m = jnp.maximum(jnp.max(s1, axis=1, keepdims=True), jnp.max(s2, axis=1, keepdims=True))
            m = jnp.where(m == -jnp.inf, 0.0, m)
            e1 = jnp.exp2(s1 - m)
            e2 = jnp.exp2(s2 - m)
            den = jnp.sum(e1, axis=1, keepdims=True) + jnp.sum(e2, axis=1, keepdims=True)
            ocw_s[g, 1] = (_dot_nt(e1.astype(BF16), wvt) + _dot(e2.astype(BF16), vn)) / jnp.where(den > 0, den, 1.0)

    @pl.when(ph == 1)
    def _():
        for g in range(B_KV_HEADS):
            sl = slope2(g)
            qd = (qd_ref[g] * scale2).astype(BF16)

            @pl.when(c == 0)
            def _(g=g, sl=sl, qd=qd):
                sn = sn_ref[...]
                kn = sn[:, g * hd:(g + 1) * hd].astype(BF16)
                vn = sn[:, (B_KV_HEADS + g) * hd:(B_KV_HEADS + g + 1) * hd].astype(BF16)
                m_s[g], l_s[g], a_s[g] = _online(new_scores(qd, kn, sl), vn, *_softmax_init(rows, hd))

            kt = jnp.concatenate([pages[p][g * hd:(g + 1) * hd, :].astype(BF16) for p in range(n_pp)], axis=1)
            vt = jnp.concatenate([pages[p][(B_KV_HEADS + g) * hd:(B_KV_HEADS + g + 1) * hd, :].astype(BF16)
                                  for p in range(n_pp)], axis=1)
            s = _dot(qd, kt) + mtok_s[g, c]
            m_s[g], l_s[g], a_s[g] = _online(s, vt, m_s[g], l_s[g], a_s[g], v_t=True)

        @pl.when(c == last)
        def _():
            gates = jax.nn.sigmoid(gl_ref[...] + gb_ref[...])
            for g in range(B_KV_HEADS):
                o_sel = _finish(l_s[g], a_s[g])
                o_cmp = ocw_s[g, 0]
                o_win = ocw_s[g, 1]
                for n in range(B_GROUP):
                    r = slice(n * tp, (n + 1) * tp)
                    base = (g * B_GROUP + n) * 3
                    o_ref[g, n] = (gates[:, base:base + 1] * o_cmp[r] + gates[:, base + 1:base + 2] * o_sel[r]
                                   + gates[:, base + 2:base + 3] * o_win[r])


def _nsa_decode(q_half, q_plain, win_t, win_new, sel_new, cache_t, layer, page_table, w_bdt, cconst2,
                gate_logits, gate_b, dec_t, past, k_past):
    bsz = q_plain.shape[0]
    hd = q_plain.shape[-1]
    rows = q_plain.shape[2]
    n_pages = page_table.shape[1]
    page = cache_t.shape[3]
    nwin = win_t.shape[3]
    n_pp = math.gcd(16, n_pages)
    n_ch = n_pages // n_pp
    pitch = 4 * hd + PAGE_PITCH_PAD

    def page_spec(p):
        return pl.BlockSpec((None, None, 4 * hd, page),
                            lambda b, ph, c, pt: (layer, pt[b * n_pages + c * n_pp + p], ph, 0))

    fix = lambda *shape: pl.BlockSpec(shape, lambda b, ph, c, pt: (0,) * len(shape))
    grid_spec = pltpu.PrefetchScalarGridSpec(
        num_scalar_prefetch=1,
        grid=(bsz, 2, n_ch),
        in_specs=[pl.BlockSpec((None, B_KV_HEADS, 2, rows, 2 * hd), lambda b, ph, c, pt: (b, 0, 0, 0, 0)),
                  pl.BlockSpec((None, B_KV_HEADS, rows, hd), lambda b, ph, c, pt: (b, 0, 0, 0)),
                  pl.BlockSpec((None, None, 4 * hd, nwin), lambda b, ph, c, pt: (layer, b, 0, 0)),
                  pl.BlockSpec((None, NEW_PAD, 4 * hd), lambda b, ph, c, pt: (b, 0, 0)),
                  pl.BlockSpec((None, NEW_PAD, 4 * hd), lambda b, ph, c, pt: (b, 0, 0)),
                  fix(*w_bdt.shape), fix(*cconst2.shape),
                  pl.BlockSpec((None, DEC_ROWS, LANE), lambda b, ph, c, pt: (b, 0, 0)),
                  fix(1, LANE)]
        + [page_spec(p) for p in range(n_pp)],
        out_specs=pl.BlockSpec((None, B_KV_HEADS, B_GROUP, DEC_ROWS, hd), lambda b, ph, c, pt: (b, 0, 0, 0, 0)),
        scratch_shapes=[pltpu.VMEM((n_pages * pitch, page), F32),
                        pltpu.VMEM((B_KV_HEADS * n_pages, 2 * hd), F32),
                        pltpu.VMEM((B_KV_HEADS * n_pages, 2 * hd), F32),
                        pltpu.VMEM((B_KV_HEADS, n_ch, rows, n_pp * page), F32),
                        pltpu.VMEM((B_KV_HEADS, 2, rows, hd), F32),
                        pltpu.VMEM((B_KV_HEADS, rows, 1), F32), pltpu.VMEM((B_KV_HEADS, rows, 1), F32),
                        pltpu.VMEM((B_KV_HEADS, rows, hd), F32)],
    )
    return pl.pallas_call(
        functools.partial(_nsa_decode_body, n_pp=n_pp, page=page, past=past, hd=hd, dec_t=dec_t,
                          n_pages=n_pages, k_past=k_past, pitch=pitch),
        grid_spec=grid_spec,
        out_shape=jax.ShapeDtypeStruct((bsz, B_KV_HEADS, B_GROUP, DEC_ROWS, hd), F32),
        compiler_params=_params(("parallel", "arbitrary", "arbitrary")),
        name="nsa_decode",
    )(page_table.reshape(-1), q_half, q_plain, win_t, win_new, sel_new, w_bdt, cconst2, gate_logits, gate_b,
      *([cache_t] * n_pp))


def _split_points(d_model):
    br = d_model // N_BRANCH
    hd_b = br // B_HEADS
    d_dk = br // (2 * D_HEADS)
    splits = (br, br, br, br, br, 6 * B_KV_HEADS * hd_b, 3 * B_HEADS, br, br, br,
              D_HEADS * d_dk, D_HEADS * d_dk, br, GLA_RANK, br, N_BRANCH * d_model)
    return [0] + [int(p) for p in np.cumsum(splits)]


def _layer_weights(l, d_model, w_in, ffn_w_in, ffn_w_out, w_branch, w_out, nsa_cmp_pe, nsa_cmp_w,
                   gla_gate_w2, gla_gate_b, nsa_gate_b):
    pts = _split_points(d_model)
    w = w_in[l]
    seg = lambda i, j: w[:, pts[i]:pts[j]]
    padl = lambda a: jnp.pad(a, ((0, 0), (0, LANE - a.shape[1])))
    wa = seg(0, 4)
    wb = jnp.concatenate([seg(4, 6), padl(seg(6, 7))], axis=1)
    wc = seg(7, 10)
    wd = jnp.concatenate([seg(10, 13), padl(seg(13, 14)), seg(14, 15)], axis=1)
    hd = nsa_cmp_w.shape[-1]
    eye = jnp.eye(B_KV_HEADS, dtype=F32)
    w_bd = jnp.einsum('cjde,gh->cjgdhe', nsa_cmp_w[l], eye).reshape(
        2, NSA_BLOCK, B_KV_HEADS * hd, B_KV_HEADS * hd)
    w_bdt = jnp.einsum('cjde,ab->cdajbe', nsa_cmp_w[l], eye).reshape(
        2, hd, 2 * NSA_BLOCK, 2 * hd)
    cvec = jnp.einsum('cjd,cjde->ce', nsa_cmp_pe[l], nsa_cmp_w[l])
    cconst = jnp.repeat(cvec, B_KV_HEADS, axis=0).reshape(1, 4 * hd)
    cconst2 = jnp.concatenate([cvec, cvec], axis=1).reshape(2, 1, 2 * hd)
    w2 = jnp.pad(gla_gate_w2[l], ((0, LANE - GLA_RANK), (0, 0)))
    return dict(
        proj=[a.astype(BF16) for a in (wa, wb, wc, wd)],
        w_merge=seg(15, 16).astype(BF16), w_branch=w_branch[l].astype(BF16), w_out=w_out[l].astype(BF16),
        ffn_in=ffn_w_in[l].astype(BF16), ffn_out=ffn_w_out[l].astype(BF16),
        w_bd=w_bd.astype(BF16), w_bdt=w_bdt.astype(BF16), cconst=cconst, cconst2=cconst2,
        gla_w2=w2, gla_b=gla_gate_b[l].reshape(1, -1),
        gate_b=jnp.pad(nsa_gate_b[l].reshape(1, -1), ((0, 0), (0, LANE - 3 * B_HEADS))),
    )


def _mixer_common(x, lw, g2, vec_a, vec_d, bsz, t, state_a, state_d, br, cache_rows_t=False):
    a_dk = br // A_HEADS
    d_dk = br // (2 * D_HEADS)
    d_dv = br // D_HEADS
    ya, yb, yc, yd, *new_t = _proj(x, g2, lw['proj'], br, (bsz, t) if cache_rows_t else None)
    o_a, st_a = _linear_mixer(functools.partial(_hgrn_body, w=br, dk=a_dk), ya, bsz, t, vec_a, state_a,
                              A_HEADS, a_dk, a_dk)
    o_d, st_d = _linear_mixer(functools.partial(_gla_body, wk=D_HEADS * d_dk, wv=br, dk=d_dk, dv=d_dv),
                              yd, bsz, t, vec_d, state_d, D_HEADS, d_dk, d_dv)
    return yb, yc, o_a, st_a, o_d, st_d, new_t


def _feature_major(cache):
    l, n, t = cache.shape[:3]
    return jnp.transpose(cache, (0, 1, 3, 4, 5, 2)).reshape(l, n, -1, t)


def kernel(x_prompt, x_sample, cache_nsa_kv, cache_nsa_win, cache_diff_kv, state_hgrn, state_gla, page_table,
           norm_gains, ffn_w_in, ffn_w_out, w_in, hgrn_lb_logits, hgrn_norm_gain, nsa_cmp_pe, nsa_cmp_w,
           nsa_gate_b, diff_lambda, diff_norm_gain, gla_gate_w2, gla_gate_b, gla_norm_gain, w_branch, w_out):
    bp, tp, d_model = x_prompt.shape
    bs, ts, _ = x_sample.shape
    depth = w_in.shape[0]
    br = d_model // N_BRANCH
    hd = br // B_HEADS
    cd2 = br // C_HEADS
    page = cache_nsa_kv.shape[2]
    n_pages = page_table.shape[1]
    past = n_pages * page
    nwin = cache_nsa_win.shape[2]
    assert past % NSA_BLOCK == 0 and ts <= DEC_ROWS and tp % NSA_BLOCK == 0
    assert past // NSA_BLOCK >= NSA_TOPK and page == 2 * NSA_BLOCK and nwin == NSA_WINDOW
    k_past = NSA_TOPK - 1
    tq = min(256, tp)
    tk = min(128, tq)

    lb_cum = jnp.cumsum(jax.nn.softmax(hgrn_lb_logits.astype(F32), axis=0), axis=0)
    lower = lb_cum - lb_cum[0]
    slopes = 2.0 ** (-np.arange(1, B_HEADS + C_HEADS + 1, dtype=np.float64))
    sl_b = jnp.asarray(slopes[0::2], F32)
    sl_c = jnp.asarray(slopes[1::2], F32)

    nsa_pool_t = _feature_major(cache_nsa_kv)
    diff_pool_t = _feature_major(cache_diff_kv)
    win_pool_t = _feature_major(cache_nsa_win)

    xp = x_prompt.reshape(bp * tp, d_model)
    xs = x_sample.reshape(bs * ts, d_model)
    outs = {k: [] for k in ('kvp', 'kvs', 'winp', 'wins', 'dkp', 'dks', 'hp', 'hs', 'gp', 'gs')}

    for l in range(depth):
        lw = _layer_weights(l, d_model, w_in, ffn_w_in, ffn_w_out, w_branch, w_out, nsa_cmp_pe, nsa_cmp_w,
                            gla_gate_w2, gla_gate_b, nsa_gate_b)
        g = norm_gains[l]
        lb = lower[l].reshape(1, br)
        vec_a = (jnp.log1p(-lb), jnp.log(lb), hgrn_norm_gain[l].reshape(1, br))
        vec_d = (lw['gla_w2'], lw['gla_b'], gla_norm_gain[l].reshape(1, br))
        lv = diff_lambda[l].astype(F32)
        lam_init = 0.8 - 0.6 * math.exp(-0.3 * l)
        lam = (jnp.exp(jnp.sum(lv[0] * lv[1])) - jnp.exp(jnp.sum(lv[2] * lv[3])) + lam_init).reshape(1)

        xp = _ffn(xp, g[0], g[1], lw['ffn_in'][0], lw['ffn_out'][0])
        yb, yc, o_a, st_a, o_d, st_d, (kv_t, win_t, dk_t) = _mixer_common(
            xp, lw, g[2], vec_a, vec_d, bp, tp, None, None, br, cache_rows_t=True)
        outs['hp'].append(st_a)
        outs['gp'].append(st_d)
        outs['kvp'].append(kv_t)
        outs['winp'].append(win_t[:, :, -min(NSA_WINDOW, tp):])
        outs['dkp'].append(dk_t)

        o_c = _diff_prompt(yc, bp, tp, diff_norm_gain[l], lam, sl_c, lam_init, tq, tk)
        cmp_kv = _compress_prompt(yb, bp, tp, lw['w_bd'], lw['cconst'], hd)
        o_b = _nsa_prompt(yb, cmp_kv, bp, tp, lw['gate_b'], sl_b, tq, tk)

        xp = _merge(xp, g[2], g[3], (o_a, o_b, o_c, o_d), lw['w_merge'], lw['w_branch'], lw['w_out'])
        xp = _ffn(xp, g[4], g[5], lw['ffn_in'][1], lw['ffn_out'][1])

        xs = _ffn(xs, g[0], g[1], lw['ffn_in'][0], lw['ffn_out'][0])
        yb, yc, o_a, st_a, o_d, st_d, _ = _mixer_common(xs, lw, g[2], vec_a, vec_d, bs, ts,
                                                        state_hgrn[l], state_gla[l], br)
        outs['hs'].append(st_a)
        outs['gs'].append(st_d)
        new_kv = yb[:, br:br + 6 * B_KV_HEADS * hd].reshape(bs, ts, 6, B_KV_HEADS * hd)
        outs['kvs'].append(new_kv[:, :, :4].reshape(bs, ts, 4, B_KV_HEADS, hd))
        new_win = new_kv[:, :, 4:].reshape(bs, ts, 2 * B_KV_HEADS * hd)
        outs['wins'].append(jnp.concatenate(
            [cache_nsa_win[l][:, ts:], new_win.reshape(bs, ts, 2, B_KV_HEADS, hd)], axis=1))
        outs['dks'].append(yc[:, br:3 * br].reshape(bs, ts, 2, C_HEADS, cd2))
        pad_new = lambda a: jnp.pad(a, ((0, 0), (0, NEW_PAD - ts), (0, 0)))

        cq = yc[:, 0:br].reshape(bs, ts, C_HEADS, 2, cd2 // 2)
        sel = (jnp.arange(C_HEADS)[:, None, None, None] == jnp.arange(C_HEADS)[None, None, :, None]) & \
              (jnp.arange(2)[None, :, None, None] == jnp.arange(2)[None, None, None, :])
        q_rows = jnp.einsum('bthmd,hmgn->bmthgnd', cq, sel.astype(F32)).reshape(bs, 2 * ts * C_HEADS, br)
        new_c = pad_new(yc[:, br:3 * br].reshape(bs, ts, 2 * br))
        o_c = _diff_decode(q_rows, new_c, diff_pool_t, l, page_table, diff_norm_gain[l], lam, lam_init, ts, past)
        o_c = o_c.reshape(bs * ts, br)

        bq = yb[:, 0:br].reshape(bs, ts, B_KV_HEADS, B_GROUP, hd).transpose(0, 2, 3, 1, 4)
        bq = jnp.pad(bq, ((0, 0), (0, 0), (0, 0), (0, DEC_ROWS - ts), (0, 0)))
        q_plain = bq.reshape(bs, B_KV_HEADS, B_GROUP * DEC_ROWS, hd)
        zero = jnp.zeros_like(q_plain)
        q_half = jnp.stack([jnp.concatenate([q_plain, zero], axis=-1),
                            jnp.concatenate([zero, q_plain], axis=-1)], axis=2)
        gl = yb[:, br + 6 * B_KV_HEADS * hd:].reshape(bs, ts, LANE)
        gl = jnp.pad(gl, ((0, 0), (0, DEC_ROWS - ts), (0, 0)))
        o_b = _nsa_decode(q_half, q_plain, win_pool_t, pad_new(new_win),
                          pad_new(new_kv[:, :, 2:4].reshape(bs, ts, 4 * hd)), nsa_pool_t, l, page_table,
                          lw['w_bdt'], lw['cconst2'], gl, lw['gate_b'], ts, past, k_past)
        o_b = o_b[:, :, :, :ts].transpose(0, 3, 1, 2, 4).reshape(bs * ts, br)

        xs = _merge(xs, g[2], g[3], (o_a, o_b, o_c, o_d), lw['w_merge'], lw['w_branch'], lw['w_out'])
        xs = _ffn(xs, g[4], g[5], lw['ffn_in'][1], lw['ffn_out'][1])

    st = lambda k: jnp.stack(outs[k])

    def token_major(k, a, b):
        x = st(k)
        return x.reshape(x.shape[:2] + (a, b, -1, x.shape[-1])).transpose(0, 1, 5, 2, 3, 4)

    return (xp.reshape(bp, tp, d_model), xs.reshape(bs, ts, d_model),
            token_major('kvp', 4, B_KV_HEADS), st('kvs'), token_major('winp', 2, B_KV_HEADS), st('wins'),
            token_major('dkp', 2, C_HEADS), st('dks'), st('hp'), st('hs'), st('gp'), st('gs'))
```

```python
import functools
import math

import numpy as np
import jax
import jax.numpy as jnp
from jax import lax
from jax.experimental import pallas as pl
from jax.experimental.pallas import tpu as pltpu

F32 = jnp.float32
BF16 = jnp.bfloat16
HI = lax.Precision.HIGHEST

N_BRANCH = 4
A_HEADS = 4
B_HEADS = 4
B_KV_HEADS = 2
B_GROUP = 2
C_HEADS = 4
D_HEADS = 4
NSA_BLOCK = 64
NSA_TOPK = 16
NSA_WINDOW = 512
GLA_RANK = 16
GLA_TAU = 16.0
NORM_EPS = 1e-6
LANE = 128
LIN_GROUP = 16
NEW_PAD = 16
DEC_ROWS = 8
ONES_PAD = 16
PAGE_PITCH_PAD = 8
VMEM_LIMIT = 56 * 1024 * 1024
LOG2E = 1.4426950408889634
NEG_BIG = -1e30

NT_DIMS = (((1,), (1,)), ((), ()))
TN_DIMS = (((0,), (0,)), ((), ()))


def _params(semantics):
    return pltpu.CompilerParams(dimension_semantics=semantics, vmem_limit_bytes=VMEM_LIMIT)


def _rms(x, g):
    return x * lax.rsqrt(jnp.mean(x * x, axis=-1, keepdims=True) + NORM_EPS) * g


def _dot(a, b):
    return jnp.dot(a, b, preferred_element_type=F32)


def _dot_nt(a, b):
    return lax.dot_general(a, b, NT_DIMS, preferred_element_type=F32)


def _row_tile(n, cap):
    t = min(n, cap)
    while n % t or t % 8:
        t -= 1
    return t


def _ffn_body(x_ref, gpre_ref, gpost_ref, wg_ref, wu_ref, wo_ref, o_ref, xn_ref, acc_ref):
    f = pl.program_id(1)

    @pl.when(f == 0)
    def _():
        xn_ref[...] = _rms(x_ref[...], gpre_ref[...]).astype(BF16)
        acc_ref[...] = jnp.zeros_like(acc_ref)

    xn = xn_ref[...]
    gate = _dot(xn, wg_ref[...])
    up = _dot(xn, wu_ref[...])
    act = (gate * jax.nn.sigmoid(gate) * up).astype(BF16)
    acc_ref[...] += _dot(act, wo_ref[...])

    @pl.when(f == pl.num_programs(1) - 1)
    def _():
        o_ref[...] = x_ref[...] + 0.5 * _rms(acc_ref[...], gpost_ref[...])


def _ffn(x, g_pre, g_post, w_in, w_out):
    n, d = x.shape
    dff = w_out.shape[0]
    tf = 256
    nf = dff // tf
    tm = _row_tile(n, 1024)
    return pl.pallas_call(
        _ffn_body,
        grid=(n // tm, nf),
        in_specs=[
            pl.BlockSpec((tm, d), lambda i, f: (i, 0)),
            pl.BlockSpec((1, d), lambda i, f: (0, 0)),
            pl.BlockSpec((1, d), lambda i, f: (0, 0)),
            pl.BlockSpec((d, tf), lambda i, f: (0, f)),
            pl.BlockSpec((d, tf), lambda i, f: (0, nf + f)),
            pl.BlockSpec((tf, d), lambda i, f: (f, 0)),
        ],
        out_specs=pl.BlockSpec((tm, d), lambda i, f: (i, 0)),
        out_shape=jax.ShapeDtypeStruct((n, d), F32),
        scratch_shapes=[pltpu.VMEM((tm, d), BF16), pltpu.VMEM((tm, d), F32)],
        compiler_params=_params(("parallel", "arbitrary")),
        name="ffn",
    )(x, g_pre.reshape(1, d), g_post.reshape(1, d), w_in, w_in, w_out)


def _proj_body(x_ref, g_ref, wa, wb, wc, wd, oa, ob, oc, od, *cache_t, br):
    xn = _rms(x_ref[...], g_ref[...]).astype(BF16)
    ys = [_dot(xn, w[...]) for w in (wa, wb, wc, wd)]
    for y, o in zip(ys, (oa, ob, oc, od)):
        o[...] = y
    if cache_t:
        kv_t, win_t, dk_t = cache_t
        kv_t[...] = ys[1][:, br:3 * br].T
        win_t[...] = ys[1][:, 3 * br:4 * br].T
        dk_t[...] = ys[2][:, br:3 * br].T


def _proj(x, g, ws, br, seqs=None):
    n, d = x.shape
    tm = _row_tile(n, 512)
    out_specs = [pl.BlockSpec((tm, w.shape[1]), lambda i: (i, 0)) for w in ws]
    out_shape = [jax.ShapeDtypeStruct((n, w.shape[1]), F32) for w in ws]
    if seqs is not None:
        bsz, t = seqs
        per = t // tm
        for rows in (2 * br, br, 2 * br):
            out_specs.append(pl.BlockSpec((None, rows, tm), lambda i: (i // per, 0, i % per)))
            out_shape.append(jax.ShapeDtypeStruct((bsz, rows, t), F32))
    return pl.pallas_call(
        functools.partial(_proj_body, br=br),
        grid=(n // tm,),
        in_specs=[pl.BlockSpec((tm, d), lambda i: (i, 0)), pl.BlockSpec((1, d), lambda i: (0, 0))]
        + [pl.BlockSpec(w.shape, lambda i: (0, 0)) for w in ws],
        out_specs=out_specs,
        out_shape=out_shape,
        compiler_params=_params(("parallel",)),
        name="mixer_in_proj",
    )(x, g.reshape(1, d), *ws)


def _merge_body(x_ref, g2_ref, g3_ref, oa, ob, oc, od, wm_ref, wb_ref, wo_ref, out_ref):
    x = x_ref[...]
    d = x.shape[1]
    h = _rms(x, g2_ref[...]).astype(BF16)
    s = None
    for n, br in enumerate((oa, ob, oc, od)):
        gate = jax.nn.sigmoid(_dot(h, wm_ref[:, n * d:(n + 1) * d]))
        term = gate * _dot(br[...].astype(BF16), wb_ref[n])
        s = term if s is None else s + term
    y = _dot(s.astype(BF16), wo_ref[...])
    out_ref[...] = x + _rms(y, g3_ref[...])


def _merge(x, g2, g3, branches, w_merge, w_branch, w_out):
    n, d = x.shape
    br = branches[0].shape[1]
    tm = _row_tile(n, 256)
    row = lambda i: (i, 0)
    fix2 = lambda i: (0, 0)
    return pl.pallas_call(
        _merge_body,
        grid=(n // tm,),
        in_specs=[pl.BlockSpec((tm, d), row), pl.BlockSpec((1, d), fix2), pl.BlockSpec((1, d), fix2)]
        + [pl.BlockSpec((tm, br), row)] * N_BRANCH
        + [pl.BlockSpec(w_merge.shape, fix2), pl.BlockSpec(w_branch.shape, lambda i: (0, 0, 0)),
           pl.BlockSpec(w_out.shape, fix2)],
        out_specs=pl.BlockSpec((tm, d), row),
        out_shape=jax.ShapeDtypeStruct((n, d), F32),
        compiler_params=_params(("parallel",)),
        name="merge",
    )(x, g2.reshape(1, d), g3.reshape(1, d), *branches, w_merge, w_branch, w_out)


def _softplus_neg_abs(z):
    return jnp.log(1.0 + jnp.exp(-jnp.abs(z)))


def _log_sigmoid(z):
    return jnp.minimum(z, 0.0) - _softplus_neg_abs(z)


def _dot_exact_lhs(a, x):
    hi = x.astype(BF16)
    lo = (x - hi.astype(F32)).astype(BF16)
    return _dot(a, hi) + _dot(a, lo)


def _lin_core(q, k, v, lf, gg, gain, s0_ref, o_ref, st_ref, qt_s, kt_s, vt_s, dec_s, oi_s, st_s,
              *, dk, dv, t_valid, n_seq):
    rows, wk = q.shape
    tt = rows // n_seq
    wv = v.shape[1]
    grp = LIN_GROUP
    t_idx = pl.program_id(1)

    @pl.when(t_idx == 0)
    def _():
        st_s[...] = s0_ref[...]

    if t_valid is not None:
        tok = t_idx * tt + lax.broadcasted_iota(jnp.int32, (rows, 1), 0) % tt
        lf = jnp.where(tok < t_valid, lf, 0.0)

    r = lax.broadcasted_iota(jnp.int32, (tt, tt), 0)
    c = lax.broadcasted_iota(jnp.int32, (tt, tt), 1)
    same = (r // grp) == (c // grp)
    tri = jnp.where(same & (c <= r), 1.0, 0.0).astype(BF16)
    ones_g = jnp.where(same, 1.0, 0.0).astype(BF16)
    lfs = [lf[si * tt:(si + 1) * tt] for si in range(n_seq)]
    b = jnp.concatenate([_dot_exact_lhs(tri, x) for x in lfs], axis=0)
    bl = jnp.concatenate([_dot_exact_lhs(ones_g, x) for x in lfs], axis=0)

    qt_s[...] = (q * jnp.exp(b)).astype(BF16)
    kt_s[...] = (k * jnp.exp(bl - b)).astype(BF16)
    vt_s[...] = v.astype(BF16)
    dec_s[...] = jnp.exp(bl)

    hk = lax.broadcasted_iota(jnp.int32, (wk, wv), 0) // dk
    hv = lax.broadcasted_iota(jnp.int32, (wk, wv), 1) // dv
    ones_hd = jnp.where(hk == hv, 1.0, 0.0).astype(BF16)

    rowm = lax.broadcasted_iota(jnp.int32, (rows, 1), 0) % grp
    od = jnp.zeros((rows, wv), F32)
    for d in range(grp):
        ks = k if d == 0 else pltpu.roll(k, d, 0)
        bs = b if d == 0 else pltpu.roll(b, d, 0)
        vs = v if d == 0 else pltpu.roll(v, d, 0)
        e = jnp.exp(jnp.minimum(b - bs, 0.0))
        z = jnp.where(rowm >= d, q * ks * e, 0.0).astype(BF16)
        od = od + _dot(z, ones_hd) * vs

    mv = lax.broadcasted_iota(jnp.int32, (wv, wk), 0) // dv
    mk = lax.broadcasted_iota(jnp.int32, (wv, wk), 1) // dk
    mbd = jnp.where(mv == mk, 1.0, 0.0).astype(F32)

    def step(i, carry):
        for si in range(n_seq):
            r0 = pl.multiple_of(si * tt + i * grp, grp)
            qg = qt_s[pl.ds(r0, grp), :]
            kg = kt_s[pl.ds(r0, grp), :]
            vg = vt_s[pl.ds(r0, grp), :]
            s = st_s[si]
            oi_s[pl.ds(r0, grp), :] = _dot_nt(qg, s.astype(BF16))
            upd = lax.dot_general(vg, kg, TN_DIMS, preferred_element_type=F32)
            st_s[si] = dec_s[pl.ds(r0, 1), :] * s + mbd * upd
        return carry

    lax.fori_loop(0, tt // grp, step, 0)

    o = oi_s[...] + od
    pv = lax.broadcasted_iota(jnp.int32, (wv, wv), 0) // dv
    pw = lax.broadcasted_iota(jnp.int32, (wv, wv), 1) // dv
    avg = jnp.where(pv == pw, 1.0 / dv, 0.0).astype(BF16)
    o2 = o * o
    o2_hi = o2.astype(BF16)
    ms = _dot(o2_hi, avg) + _dot((o2 - o2_hi.astype(F32)).astype(BF16), avg)
    y = o * lax.rsqrt(ms + NORM_EPS) * gain * (gg * jax.nn.sigmoid(gg))
    o_ref[...] = y.reshape(o_ref.shape)

    @pl.when(t_idx == pl.num_programs(1) - 1)
    def _():
        st_ref[...] = st_s[...]


def _hgrn_body(y_ref, la_ref, lc_ref, gain_ref, s0_ref, o_ref, st_ref, *scratch, w, dk, t_valid):
    n_seq = y_ref.shape[0]
    y = y_ref[...].reshape(-1, y_ref.shape[2])
    q = y[:, 0:w] * (dk ** -0.5)
    z = y[:, w:2 * w]
    v = y[:, 2 * w:3 * w]
    gg = y[:, 3 * w:4 * w]
    ls = _log_sigmoid(z)
    a = la_ref[...] + ls
    cc = lc_ref[...]
    lf = jnp.maximum(a, cc) + _softplus_neg_abs(a - cc)
    k = jnp.exp(la_ref[...] + (ls - z))
    _lin_core(q, k, v, lf, gg, gain_ref[...], s0_ref, o_ref, st_ref, *scratch, dk=dk, dv=dk, t_valid=t_valid,
              n_seq=n_seq)


def _gla_body(y_ref, w2_ref, b2_ref, gain_ref, s0_ref, o_ref, st_ref, *scratch, wk, wv, dk, dv, t_valid):
    n_seq = y_ref.shape[0]
    y = y_ref[...].reshape(-1, y_ref.shape[2])
    q = y[:, 0:wk] * (dk ** -0.5)
    k = y[:, wk:2 * wk]
    v = y[:, 2 * wk:2 * wk + wv]
    lr = y[:, 2 * wk + wv:2 * wk + wv + LANE]
    gg = y[:, 2 * wk + wv + LANE:2 * wk + 2 * wv + LANE]
    u = jnp.dot(lr, w2_ref[...], precision=HI, preferred_element_type=F32) + b2_ref[...]
    lf = _log_sigmoid(u) * (1.0 / GLA_TAU)
    _lin_core(q, k, v, lf, gg, gain_ref[...], s0_ref, o_ref, st_ref, *scratch, dk=dk, dv=dv, t_valid=t_valid,
              n_seq=n_seq)


def _lin_call(body, y, n_seq, tt, vecs, s0):
    bsz, t, cw = y.shape
    wv, wk = s0.shape[1:]
    rows = n_seq * tt
    return pl.pallas_call(
        body,
        grid=(bsz // n_seq, t // tt),
        in_specs=[pl.BlockSpec((n_seq, tt, cw), lambda b, i: (b, i, 0))]
        + [pl.BlockSpec(a.shape, lambda b, i: (0, 0)) for a in vecs]
        + [pl.BlockSpec((n_seq, wv, wk), lambda b, i: (b, 0, 0))],
        out_specs=[pl.BlockSpec((n_seq, tt, wv), lambda b, i: (b, i, 0)),
                   pl.BlockSpec((n_seq, wv, wk), lambda b, i: (b, 0, 0))],
        out_shape=[jax.ShapeDtypeStruct((bsz, t, wv), F32), jax.ShapeDtypeStruct((bsz, wv, wk), F32)],
        scratch_shapes=[pltpu.VMEM((rows, wk), BF16), pltpu.VMEM((rows, wk), BF16), pltpu.VMEM((rows, wv), BF16),
                        pltpu.VMEM((rows, wk), F32), pltpu.VMEM((rows, wv), F32), pltpu.VMEM((n_seq, wv, wk), F32)],
        compiler_params=_params(("parallel", "arbitrary")),
        name="gated_linear",
    )(y, *vecs, s0)


def _state_to_bd(state):
    bsz, nh, dk, dv = state.shape
    eye = jnp.eye(nh, dtype=state.dtype)
    return jnp.einsum('bhkv,hg->bhvgk', state, eye).reshape(bsz, nh * dv, nh * dk)


def _bd_to_state(st, nh):
    bsz, wv, wk = st.shape
    dv, dk = wv // nh, wk // nh
    blocks = st.reshape(bsz, nh, dv, nh, dk)
    idx = jnp.arange(nh)
    return blocks[:, idx, :, idx, :].transpose(1, 0, 3, 2)


def _linear_mixer(body, y, bsz, t, vecs, state0, nh, dk, dv):
    assert dv & (dv - 1) == 0
    tp = -(-t // LIN_GROUP) * LIN_GROUP
    y3 = y.reshape(bsz, t, -1)
    if tp != t:
        y3 = jnp.pad(y3, ((0, 0), (0, tp - t), (0, 0)))
    tt = min(tp, LANE)
    n_seq = math.gcd(bsz, min(8, max(1, 512 // tt)))
    wk, wv = nh * dk, nh * dv
    s0 = jnp.zeros((bsz, wv, wk), F32) if state0 is None else _state_to_bd(state0)
    body = functools.partial(body, t_valid=None if tp == t else t)
    o, st = _lin_call(body, y3, n_seq, tt, vecs, s0)
    return o[:, :t].reshape(bsz * t, wv), _bd_to_state(st, nh)


def _col_update(idx, tiles, m_s, acc_s):
    m_old = m_s[idx]
    m_new = m_old
    for s, _, shift in tiles:
        smax = jnp.max(s, axis=0, keepdims=True)
        m_new = jnp.maximum(m_new, smax if shift is None else smax + shift)
    m_safe = jnp.where(m_new == -jnp.inf, 0.0, m_new)
    pv = None
    for s, vt, shift in tiles:
        p = jnp.exp2(s + ((-m_safe) if shift is None else (shift - m_safe)))
        d = _dot(vt, p.astype(BF16))
        pv = d if pv is None else pv + d
    acc_s[idx] = jnp.exp2(m_old - m_safe) * acc_s[idx] + pv
    m_s[idx] = m_new


def _col_finish(acc, dv):
    den = acc[dv:dv + 1, :]
    return acc[:dv, :] / jnp.where(den > 0, den, 1.0)


def _col_reset(m_s, acc_s):
    m_s[...] = jnp.full(m_s.shape, -jnp.inf, F32)
    acc_s[...] = jnp.zeros(acc_s.shape, F32)


def _online(s, v, m, l, a, v_t=False):
    m_new = jnp.maximum(m, jnp.max(s, axis=1, keepdims=True))
    m_safe = jnp.where(m_new == -jnp.inf, 0.0, m_new)
    p = jnp.exp2(s - m_safe)
    alpha = jnp.exp2(m - m_safe)
    l = alpha * l + jnp.sum(p, axis=1, keepdims=True)
    pv = _dot_nt(p.astype(BF16), v) if v_t else _dot(p.astype(BF16), v)
    return m_new, l, alpha * a + pv


def _finish(l, a):
    return a / jnp.where(l > 0, l, 1.0)


def _softmax_init(rows, width):
    return (jnp.full((rows, 1), -jnp.inf, F32), jnp.zeros((rows, 1), F32), jnp.zeros((rows, width), F32))


def _stage_keys_values(k_ref, v_ref, kb_s, vt_s, *, t, tk, w):
    kb_s[...] = k_ref[...].astype(BF16)
    n_heads = vt_s.shape[0]
    ones = jnp.where(lax.broadcasted_iota(jnp.int32, (ONES_PAD, tk), 0) == 0, 1.0, 0.0).astype(BF16)

    def body(jt, carry):
        vt = v_ref[pl.ds(pl.multiple_of(jt * tk, tk), tk), :].T
        for hh in range(n_heads):
            vt_s[hh, jt, 0:w, :] = vt[hh * w:(hh + 1) * w, :].astype(BF16)
            vt_s[hh, jt, w:w + ONES_PAD, :] = ones
        return carry

    lax.fori_loop(0, t // tk, body, 0)


def _place_rows(x, lo, total):
    r, n = x.shape
    parts = []
    if lo:
        parts.append(jnp.zeros((lo, n), x.dtype))
    parts.append(x)
    if total - lo - r:
        parts.append(jnp.zeros((total - lo - r, n), x.dtype))
    return jnp.concatenate(parts, axis=0) if len(parts) > 1 else x


def _diff_prompt_body(sl_ref, lam_ref, q_ref, k_ref, v_ref, gain_ref, o_ref, kb_s, vt_s, boff_s, m_s, acc_s, s_s,
                      *, t, tq, tk, cd, lam_init):
    hp = pl.program_id(1)
    i = pl.program_id(2)
    w = 2 * cd
    pair = LANE // w

    @pl.when(i == 0)
    def _():
        _stage_keys_values(k_ref, v_ref, kb_s, vt_s, t=t, tk=tk, w=w)

    qt = (q_ref[...] * (cd ** -0.5 * LOG2E)).T
    slope2 = [sl_ref[hp * pair + hh] * LOG2E for hh in range(pair)]
    chains = [(hh, _place_rows(qt[hh * w + mi * cd:hh * w + (mi + 1) * cd, :], hh * w + mi * cd, LANE).astype(BF16))
              for hh in range(pair) for mi in range(2)]
    rel = lax.broadcasted_iota(jnp.int32, (tk, tq), 1) - lax.broadcasted_iota(jnp.int32, (tk, tq), 0)
    rel_f = rel.astype(F32)
    for hh in range(pair):
        boff_s[hh] = rel_f * (-slope2[hh])
    _col_reset(m_s, acc_s)
    ratio = tq // tk

    def key_tiles(jj):
        return [kb_s[pl.ds(pl.multiple_of((jj * ratio + r) * tk, tk), tk), :] for r in range(ratio)]

    def score_span(jj):
        kts = key_tiles(jj)
        for ci, (_, qc) in enumerate(chains):
            for r in range(ratio):
                s_s[jj % 2, ci, r] = _dot(kts[r], qc)

    score_span(0)

    def off_group(jj, carry):
        for ci, (hh, _) in enumerate(chains):
            boff = boff_s[hh]
            _col_update(ci, [(s_s[jj % 2, ci, r] + boff, vt_s[hh, jj * ratio + r],
                              ((i - jj) * tq - r * tk).astype(F32) * (-slope2[hh])) for r in range(ratio)],
                        m_s, acc_s)
        score_span(jj + 1)
        return carry

    lax.fori_loop(0, i, off_group, 0)
    dists = [rel - r * tk for r in range(ratio)]
    for ci, (hh, _) in enumerate(chains):
        _col_update(ci, [(s_s[i % 2, ci, r] + jnp.where(dists[r] >= 0, dists[r].astype(F32) * (-slope2[hh]), -jnp.inf),
                          vt_s[hh, i * ratio + r], None) for r in range(ratio)], m_s, acc_s)

    outs = []
    for hh in range(pair):
        o = _col_finish(acc_s[2 * hh], w) - lam_ref[0] * _col_finish(acc_s[2 * hh + 1], w)
        ms = jnp.mean(o * o, axis=0, keepdims=True)
        outs.append(o * lax.rsqrt(ms + NORM_EPS))
    o_ref[...] = (jnp.concatenate(outs, axis=0) * gain_ref[...] * (1.0 - lam_init)).T


def _diff_prompt(yc, bsz, t, gain, lam, slopes, lam_init, tq, tk):
    w = gain.shape[0]
    nh = yc.shape[1] // (3 * w)
    pair = LANE // w
    nhp = nh // pair
    nq = t // tq
    dva = w + ONES_PAD
    smem = pl.BlockSpec(memory_space=pltpu.SMEM)
    return pl.pallas_call(
        functools.partial(_diff_prompt_body, t=t, tq=tq, tk=tk, cd=w // 2, lam_init=lam_init),
        grid=(bsz, nhp, nq),
        in_specs=[smem, smem,
                  pl.BlockSpec((tq, LANE), lambda b, hp, i: (b * nq + i, hp)),
                  pl.BlockSpec((t, LANE), lambda b, hp, i: (b, nhp + hp)),
                  pl.BlockSpec((t, LANE), lambda b, hp, i: (b, 2 * nhp + hp)),
                  pl.BlockSpec((LANE, 1), lambda b, hp, i: (0, 0))],
        out_specs=pl.BlockSpec((tq, LANE), lambda b, hp, i: (b * nq + i, hp)),
        out_shape=jax.ShapeDtypeStruct((bsz * t, nh * w), F32),
        scratch_shapes=[pltpu.VMEM((t, LANE), BF16), pltpu.VMEM((pair, t // tk, dva, tk), BF16),
                        pltpu.VMEM((pair, tk, tq), F32), pltpu.VMEM((2 * pair, 1, tq), F32),
                        pltpu.VMEM((2 * pair, dva, tq), F32), pltpu.VMEM((2, 2 * pair, tq // tk, tk, tq), F32)],
        compiler_params=_params(("parallel", "parallel", "arbitrary")),
        name="diff_attn_prompt",
    )(slopes, lam, yc, yc, yc, jnp.tile(gain, pair).reshape(LANE, 1))


def _compress(xk_ref, xv_ref, w_ref, nb):
    def body(j, acc):
        ak, av = acc
        xk = xk_ref[pl.ds(j, nb, stride=NSA_BLOCK), :].astype(BF16)
        xv = xv_ref[pl.ds(j, nb, stride=NSA_BLOCK), :].astype(BF16)
        return ak + _dot(xk, w_ref[0, j]), av + _dot(xv, w_ref[1, j])

    zero = jnp.zeros((nb, xk_ref.shape[1]), F32)
    ak, av = lax.fori_loop(0, NSA_BLOCK, body, (zero, zero))
    return jnp.concatenate([ak, av], axis=1)


def _compress_body(xk_ref, xv_ref, w_ref, c_ref, o_ref, *, nb, hd):
    acc = _compress(xk_ref, xv_ref, w_ref, nb) + c_ref[...]
    for p in range(4):
        o_ref[p] = acc[:, p * hd:(p + 1) * hd]


def _compress_prompt(yb, bsz, t, w_bd, cconst, hd):
    nb = t // NSA_BLOCK
    cw = 4 * hd
    hw = cw // 2
    return pl.pallas_call(
        functools.partial(_compress_body, nb=nb, hd=hd),
        grid=(bsz,),
        in_specs=[pl.BlockSpec((t, hw), lambda b: (b, 2)),
                  pl.BlockSpec((t, hw), lambda b: (b, 3)),
                  pl.BlockSpec(w_bd.shape, lambda b: (0, 0, 0, 0)),
                  pl.BlockSpec((1, cw), lambda b: (0, 0))],
        out_specs=pl.BlockSpec((None, 4, nb, hd), lambda b: (b, 0, 0, 0)),
        out_shape=jax.ShapeDtypeStruct((bsz, 4, nb, hd), F32),
        compiler_params=_params(("parallel",)),
        name="nsa_compress_prompt",
    )(yb, yb, w_bd, cconst)


def _rank_rows(score, tq):
    nb = score.shape[0]
    sub = 8
    slabs = [score[v * sub:(v + 1) * sub, :] for v in range(nb // sub)]
    ranks = [jnp.zeros((sub, tq), F32) for _ in slabs]
    sub_i = lax.broadcasted_iota(jnp.int32, (sub, 1), 0)
    for ib in range(nb):
        row = score[ib:ib + 1, :]
        for v, slab in enumerate(slabs):
            if ib < v * sub:
                ahead = jnp.where(row >= slab, 1.0, 0.0)
            elif ib >= (v + 1) * sub:
                ahead = jnp.where(row > slab, 1.0, 0.0)
            else:
                ahead = jnp.where(sub_i > ib - v * sub, jnp.where(row >= slab, 1.0, 0.0),
                                  jnp.where(row > slab, 1.0, 0.0))
            ranks[v] = ranks[v] + ahead
    return jnp.concatenate(ranks, axis=0)


def _nsa_prompt_body(sl_ref, q_ref, cmp_ref, ks_ref, vs_ref, kw_ref, vw_ref, gl_ref, gb_ref, o_ref,
                     ksb_s, kwb_s, vst_s, vwt_s, boff_s, m_s, acc_s, s_s, *, t, tq, tk, nb, hd, k_sel):
    i = pl.program_id(1)
    heads = [(g, n) for g in range(B_KV_HEADS) for n in range(B_GROUP)]
    nh = len(heads)

    @pl.when(i == 0)
    def _():
        _stage_keys_values(ks_ref, vs_ref, ksb_s, vst_s, t=t, tk=tk, w=hd)
        _stage_keys_values(kw_ref, vw_ref, kwb_s, vwt_s, t=t, tk=tk, w=hd)

    qt = (q_ref[...] * (hd ** -0.5 * LOG2E)).T
    qh = [qt[hi * hd:(hi + 1) * hd, :].astype(BF16) for hi in range(nh)]
    qk = [_place_rows(qh[hi], g * hd, LANE) for hi, (g, _) in enumerate(heads)]
    sl2 = [sl_ref[hi] * LOG2E for hi in range(nh)]
    qpos = i * tq + lax.broadcasted_iota(jnp.int32, (1, tq), 1)

    blk = lax.broadcasted_iota(jnp.int32, (nb, 1), 0)
    distc = qpos - (blk * NSA_BLOCK + (NSA_BLOCK - 1))
    distc_f = distc.astype(F32)
    cur = qpos // NSA_BLOCK
    forced = (blk == 0) | (blk == cur) | (blk == cur - 1)
    o_cmp = []
    selneg = []
    for g in range(B_KV_HEADS):
        kc = cmp_ref[g].astype(BF16)
        vc = cmp_ref[B_KV_HEADS + g].astype(BF16)
        imp = None
        for n in range(B_GROUP):
            hi = g * B_GROUP + n
            s = jnp.where(distc >= 0, _dot(kc, qh[hi]) - sl2[hi] * distc_f, -jnp.inf)
            m = jnp.max(s, axis=0, keepdims=True)
            e = jnp.exp2(s - jnp.where(m == -jnp.inf, 0.0, m))
            den = jnp.sum(e, axis=0, keepdims=True)
            p = e / jnp.where(den > 0, den, 1.0)
            o_cmp.append(lax.dot_general(vc, p.astype(BF16), TN_DIMS, preferred_element_type=F32))
            imp = p if imp is None else imp + p
        score = jnp.where(blk > cur, -jnp.inf, jnp.where(forced, jnp.inf, imp))
        selneg.append(jnp.where(_rank_rows(score, tq) < k_sel, 0.0, NEG_BIG).astype(BF16))

    rel = lax.broadcasted_iota(jnp.int32, (tk, tq), 1) - lax.broadcasted_iota(jnp.int32, (tk, tq), 0)
    rel_f = rel.astype(F32)
    for hi in range(nh):
        boff_s[hi] = rel_f * (-sl2[hi])
    ratio = tq // tk
    erow = lax.broadcasted_iota(jnp.int32, (tk, nb), 0)
    eblk = lax.broadcasted_iota(jnp.int32, (tk, nb), 1)

    def block_masks(j):
        expand = jnp.where((erow + j * tk) // NSA_BLOCK == eblk, 1.0, 0.0).astype(BF16)
        return [_dot(expand, sn) for sn in selneg]

    _col_reset(m_s, acc_s)

    def key_tiles(kb_s, jj):
        return [kb_s[pl.ds(pl.multiple_of((jj * ratio + r) * tk, tk), tk), :] for r in range(ratio)]

    def score_span(jj):
        kts = key_tiles(ksb_s, jj)
        mbs = [block_masks(jj * ratio + r) for r in range(ratio)]
        for hi, (g, _) in enumerate(heads):
            for r in range(ratio):
                s_s[jj % 2, hi, r] = _dot(kts[r], qk[hi]) + mbs[r][g]

    score_span(0)

    def sel_off(jj, carry):
        for hi, (g, _) in enumerate(heads):
            boff = boff_s[hi]
            _col_update(hi, [(s_s[jj % 2, hi, r] + boff, vst_s[g, jj * ratio + r],
                              ((i - jj) * tq - r * tk).astype(F32) * (-sl2[hi])) for r in range(ratio)], m_s, acc_s)
        score_span(jj + 1)
        return carry

    lax.fori_loop(0, i, sel_off, 0)
    dists = [rel - r * tk for r in range(ratio)]
    for hi, (g, _) in enumerate(heads):
        _col_update(hi, [(s_s[i % 2, hi, r] + jnp.where(dists[r] >= 0, dists[r].astype(F32) * (-sl2[hi]), -jnp.inf),
                          vst_s[g, i * ratio + r], None) for r in range(ratio)], m_s, acc_s)
    o_sel = [_col_finish(acc_s[hi], hd) for hi in range(nh)]

    _col_reset(m_s, acc_s)
    for ds in range(-(NSA_WINDOW // tq), 1):
        def win_span(ds=ds):
            kts = key_tiles(kwb_s, i + ds)
            dists = [rel - (ds * tq + r * tk) for r in range(ratio)]
            for hi, (g, _) in enumerate(heads):
                _col_update(hi, [(_dot(kts[r], qk[hi])
                                  + jnp.where((dists[r] >= 0) & (dists[r] < NSA_WINDOW),
                                              dists[r].astype(F32) * (-sl2[hi]), -jnp.inf),
                                  vwt_s[g, (i + ds) * ratio + r], None) for r in range(ratio)], m_s, acc_s)

        if ds < 0:
            pl.when(i + ds >= 0)(win_span)
        else:
            win_span()
    o_win = [_col_finish(acc_s[hi], hd) for hi in range(nh)]

    gates = jax.nn.sigmoid(gl_ref[...] + gb_ref[...]).T
    outs = [gates[3 * hi:3 * hi + 1] * o_cmp[hi] + gates[3 * hi + 1:3 * hi + 2] * o_sel[hi]
            + gates[3 * hi + 2:3 * hi + 3] * o_win[hi] for hi in range(nh)]
    o_ref[...] = jnp.concatenate(outs, axis=0).T


def _nsa_prompt(yb, cmp_kv, bsz, t, gate_b, slopes, tq, tk):
    nb, hd = cmp_kv.shape[2], cmp_kv.shape[3]
    qw = B_HEADS * hd
    nq = t // tq
    first = qw // LANE
    dva = hd + ONES_PAD
    kv_spec = lambda c: pl.BlockSpec((t, LANE), lambda b, i: (b, first + c))
    return pl.pallas_call(
        functools.partial(_nsa_prompt_body, t=t, tq=tq, tk=tk, nb=nb, hd=hd, k_sel=min(NSA_TOPK, nb)),
        grid=(bsz, nq),
        in_specs=[pl.BlockSpec(memory_space=pltpu.SMEM),
                  pl.BlockSpec((tq, qw), lambda b, i: (b * nq + i, 0)),
                  pl.BlockSpec((None, 2 * B_KV_HEADS, nb, hd), lambda b, i: (b, 0, 0, 0)),
                  kv_spec(2), kv_spec(3), kv_spec(4), kv_spec(5),
                  pl.BlockSpec((tq, LANE), lambda b, i: (b * nq + i, first + 6)),
                  pl.BlockSpec((1, LANE), lambda b, i: (0, 0))],
        out_specs=pl.BlockSpec((tq, qw), lambda b, i: (b * nq + i, 0)),
        out_shape=jax.ShapeDtypeStruct((bsz * t, qw), F32),
        scratch_shapes=[pltpu.VMEM((t, LANE), BF16), pltpu.VMEM((t, LANE), BF16),
                        pltpu.VMEM((B_KV_HEADS, t // tk, dva, tk), BF16),
                        pltpu.VMEM((B_KV_HEADS, t // tk, dva, tk), BF16),
                        pltpu.VMEM((B_HEADS, tk, tq), F32), pltpu.VMEM((B_HEADS, 1, tq), F32),
                        pltpu.VMEM((B_HEADS, dva, tq), F32), pltpu.VMEM((2, B_HEADS, tq // tk, tk, tq), F32)],
        compiler_params=_params(("parallel", "arbitrary")),
        name="nsa_attn_prompt",
    )(slopes, yb, cmp_kv, yb, yb, yb, yb, yb, gate_b)


def _diff_decode_body(pt_ref, lam_ref, q_ref, new_ref, gain_ref, *refs, n_pp, page, past, nh, cd, lam_init, dec_t):
    pages = refs[:n_pp]
    o_ref = refs[n_pp]
    m_s, l_s, a_s = refs[n_pp + 1:]
    c = pl.program_id(1)
    w = 2 * cd * nh
    rows = 2 * dec_t * nh
    row = lax.broadcasted_iota(jnp.int32, (rows, 1), 0)
    rh = row % nh
    rt = (row % (dec_t * nh)) // nh
    slope2 = jnp.zeros((rows, 1), F32)
    for h in range(nh):
        slope2 = jnp.where(rh == h, 2.0 ** (-2 * (h + 1)) * LOG2E, slope2)
    qb = (q_ref[...] * (cd ** -0.5 * LOG2E)).astype(BF16)

    @pl.when(c == 0)
    def _():
        new = new_ref[...]
        kn = new[:, :w].astype(BF16)
        vn = new[:, w:].astype(BF16)
        col = lax.broadcasted_iota(jnp.int32, (1, new.shape[0]), 1)
        dist = rt - col
        s = jnp.where((dist >= 0) & (col < dec_t), _dot_nt(qb, kn) - slope2 * dist.astype(F32), -jnp.inf)
        m_s[...], l_s[...], a_s[...] = _online(s, vn, *_softmax_init(rows, w))

    kt = jnp.concatenate([pages[p][0:w, :].astype(BF16) for p in range(n_pp)], axis=1)
    vt = jnp.concatenate([pages[p][w:2 * w, :].astype(BF16) for p in range(n_pp)], axis=1)
    col = lax.broadcasted_iota(jnp.int32, (1, n_pp * page), 1)
    dist = (past + rt) - (c * (n_pp * page) + col)
    s = _dot(qb, kt) - slope2 * dist.astype(F32)
    m, l, a = _online(s, vt, m_s[...], l_s[...], a_s[...], v_t=True)
    m_s[...], l_s[...], a_s[...] = m, l, a

    @pl.when(c == pl.num_programs(1) - 1)
    def _():
        full = _finish(l, a)
        own = jnp.zeros((rows, 2 * cd), F32)
        for h in range(nh):
            own = jnp.where(rh == h, full[:, h * 2 * cd:(h + 1) * 2 * cd], own)
        half = rows // 2
        o = own[:half] - lam_ref[0] * own[half:]
        o_ref[...] = _rms(o, gain_ref[...]) * (1.0 - lam_init)


def _diff_decode(q_rows, new_kv, cache_t, layer, page_table, gain, lam, lam_init, dec_t, past):
    bsz, rows, w = q_rows.shape
    n_pages = page_table.shape[1]
    page = cache_t.shape[3]
    n_pp = math.gcd(16, n_pages)
    nh = C_HEADS
    cd = w // (2 * nh)

    def page_spec(p):
        return pl.BlockSpec((None, None, 2 * w, page),
                            lambda b, c, pt: (layer, pt[b * n_pages + c * n_pp + p], 0, 0))

    grid_spec = pltpu.PrefetchScalarGridSpec(
        num_scalar_prefetch=1,
        grid=(bsz, n_pages // n_pp),
        in_specs=[pl.BlockSpec(memory_space=pltpu.SMEM),
                  pl.BlockSpec((None, rows, w), lambda b, c, pt: (b, 0, 0)),
                  pl.BlockSpec((None, NEW_PAD, 2 * w), lambda b, c, pt: (b, 0, 0)),
                  pl.BlockSpec((1, 2 * cd), lambda b, c, pt: (0, 0))]
        + [page_spec(p) for p in range(n_pp)],
        out_specs=pl.BlockSpec((None, rows // 2, 2 * cd), lambda b, c, pt: (b, 0, 0)),
        scratch_shapes=[pltpu.VMEM((rows, 1), F32), pltpu.VMEM((rows, 1), F32), pltpu.VMEM((rows, w), F32)],
    )
    return pl.pallas_call(
        functools.partial(_diff_decode_body, n_pp=n_pp, page=page, past=past, nh=nh, cd=cd,
                          lam_init=lam_init, dec_t=dec_t),
        grid_spec=grid_spec,
        out_shape=jax.ShapeDtypeStruct((bsz, rows // 2, 2 * cd), F32),
        compiler_params=_params(("parallel", "arbitrary")),
        name="diff_attn_decode",
    )(page_table.reshape(-1), lam, q_rows, new_kv, gain.reshape(1, 2 * cd), *([cache_t] * n_pp))


def _nsa_decode_body(pt_ref, qh_ref, qd_ref, wc_ref, wn_ref, sn_ref, wbd_ref, cc_ref, gl_ref, gb_ref, *refs,
                     n_pp, page, past, hd, dec_t, n_pages, k_past, pitch):
    pages = refs[:n_pp]
    o_ref = refs[n_pp]
    x_s, ak_s, av_s, mtok_s, ocw_s, m_s, l_s, a_s = refs[n_pp + 1:]
    ph = pl.program_id(1)
    c = pl.program_id(2)
    last = pl.num_programs(2) - 1
    tp = DEC_ROWS
    rows = B_GROUP * tp
    row = lax.broadcasted_iota(jnp.int32, (rows, 1), 0)
    rt = row % tp
    scale2 = hd ** -0.5 * LOG2E
    qpos = past + rt

    def slope2(g):
        return jnp.where(row < tp, 2.0 ** (-(2 * (g * B_GROUP) + 1)) * LOG2E,
                         2.0 ** (-(2 * (g * B_GROUP + 1) + 1)) * LOG2E)

    def new_scores(qd, kn, sl):
        col = lax.broadcasted_iota(jnp.int32, (1, kn.shape[0]), 1)
        d = rt - col
        return jnp.where((d >= 0) & (col < dec_t), _dot_nt(qd, kn) - sl * d.astype(F32), -jnp.inf)

    @pl.when(ph == 0)
    def _():
        for p in range(n_pp):
            r0 = pl.multiple_of((c * n_pp + p) * pitch, 8)
            x_s[pl.ds(r0, 4 * hd), :] = pages[p][...]

    @pl.when((ph == 0) & (c == last))
    def _():
        unroll = 2

        def compress(cc):
            def body(du, acc):
                for u in range(unroll):
                    dp = du * unroll + u
                    lhs = jnp.concatenate(
                        [jnp.concatenate([x_s[pl.ds(cc * 2 * hd + g * hd + 2 * dp + e, n_pages, stride=pitch), :]
                                          for e in range(2)], axis=1) for g in range(B_KV_HEADS)],
                        axis=0).astype(BF16)
                    acc = acc + _dot(lhs, wbd_ref[cc, dp])
                return acc

            return lax.fori_loop(0, hd // (2 * unroll), body, jnp.zeros((B_KV_HEADS * n_pages, 2 * hd), F32))

        ak_s[...] = compress(0)
        av_s[...] = compress(1)
        pgi = lax.broadcasted_iota(jnp.int32, (1, n_pages), 1)
        lane = lax.broadcasted_iota(jnp.int32, (1, page), 1)
        for g in range(B_KV_HEADS):
            sl = slope2(g)
            kc = (ak_s[g * n_pages:(g + 1) * n_pages, :] + cc_ref[0]).astype(BF16)
            vc = (av_s[g * n_pages:(g + 1) * n_pages, :] + cc_ref[1]).astype(BF16)
            ss = []
            for hf in range(2):
                qh = (qh_ref[g, hf] * scale2).astype(BF16)
                dist = qpos - ((2 * pgi + hf) * NSA_BLOCK + (NSA_BLOCK - 1))
                ss.append(jnp.where(dist >= 0, _dot_nt(qh, kc) - sl * dist.astype(F32), -jnp.inf))
            m = jnp.maximum(jnp.max(ss[0], axis=1, keepdims=True), jnp.max(ss[1], axis=1, keepdims=True))
            m = jnp.where(m == -jnp.inf, 0.0, m)
            es = [jnp.exp2(s - m) for s in ss]
            den = jnp.sum(es[0], axis=1, keepdims=True) + jnp.sum(es[1], axis=1, keepdims=True)
            den = jnp.where(den > 0, den, 1.0)
            ps = [e / den for e in es]
            full = [_dot(p.astype(BF16), vc) for p in ps]
            ocw_s[g, 0] = full[0][:, :hd] + full[1][:, hd:]

            imps = [p[:tp] + p[tp:] for p in ps]
            sc = [jnp.where(pgi == 0, jnp.inf, imps[0]), jnp.where(pgi == n_pages - 1, jnp.inf, imps[1])]
            rank = [jnp.zeros((tp, n_pages), F32), jnp.zeros((tp, n_pages), F32)]
            for pg in range(n_pages):
                for hf in range(2):
                    colv = sc[hf][:, pg:pg + 1]
                    for h2 in range(2):
                        tie = jnp.where(2 * pgi + h2 > 2 * pg + hf, 1.0, 0.0)
                        rank[h2] = rank[h2] + jnp.where(colv > sc[h2], 1.0, jnp.where(colv == sc[h2], tie, 0.0))
            sel = [jnp.where(r < k_past, 1.0, 0.0) for r in rank]
            sel = [jnp.concatenate([s_, s_], axis=0) for s_ in sel]
            for pg in range(n_pages):
                on = jnp.where(lane < NSA_BLOCK, sel[0][:, pg:pg + 1], sel[1][:, pg:pg + 1])
                dist = qpos - (pg * page + lane)
                lo = (pg % n_pp) * page
                mtok_s[g, pg // n_pp, :, lo:lo + page] = jnp.where(on > 0.5, dist.astype(F32) * (-sl), NEG_BIG)

            qd = (qd_ref[g] * scale2).astype(BF16)
            wkt = wc_ref[g * hd:(g + 1) * hd, :].astype(BF16)
            wvt = wc_ref[(B_KV_HEADS + g) * hd:(B_KV_HEADS + g + 1) * hd, :].astype(BF16)
            nwin = wkt.shape[1]
            d1 = (nwin + rt) - lax.broadcasted_iota(jnp.int32, (1, nwin), 1)
            s1 = jnp.where((d1 >= 0) & (d1 < NSA_WINDOW), _dot(qd, wkt) - sl * d1.astype(F32), -jnp.inf)
            wn = wn_ref[...]
            kn = wn[:, g * hd:(g + 1) * hd].astype(BF16)
            vn = wn[:, (B_KV_HEADS + g) * hd:(B_KV_HEADS + g + 1) * hd].astype(BF16)
            s2 = new_scores(qd, kn, sl)
            m = jnp.maximum(jnp.max(s1, axis=1, keepdims=True), jnp.max(s2, axis=1, keepdims=True))
            m = jnp.where(m == -jnp.inf, 0.0, m)
            e1 = jnp.exp2(s1 - m)
            e2 = jnp.exp2(s2 - m)
            den = jnp.sum(e1, axis=1, keepdims=True) + jnp.sum(e2, axis=1, keepdims=True)
            ocw_s[g, 1] = (_dot_nt(e1.astype(BF16), wvt) + _dot(e2.astype(BF16), vn)) / jnp.where(den > 0, den, 1.0)

    @pl.when(ph == 1)
    def _():
        for g in range(B_KV_HEADS):
            sl = slope2(g)
            qd = (qd_ref[g] * scale2).astype(BF16)

            @pl.when(c == 0)
            def _(g=g, sl=sl, qd=qd):
                sn = sn_ref[...]
                kn = sn[:, g * hd:(g + 1) * hd].astype(BF16)
                vn = sn[:, (B_KV_HEADS + g) * hd:(B_KV_HEADS + g + 1) * hd].astype(BF16)
                m_s[g], l_s[g], a_s[g] = _online(new_scores(qd, kn, sl), vn, *_softmax_init(rows, hd))

            kt = jnp.concatenate([pages[p][g * hd:(g + 1) * hd, :].astype(BF16) for p in range(n_pp)], axis=1)
            vt = jnp.concatenate([pages[p][(B_KV_HEADS + g) * hd:(B_KV_HEADS + g + 1) * hd, :].astype(BF16)
                                  for p in range(n_pp)], axis=1)
            s = _dot(qd, kt) + mtok_s[g, c]
            m_s[g], l_s[g], a_s[g] = _online(s, vt, m_s[g], l_s[g], a_s[g], v_t=True)

        @pl.when(c == last)
        def _():
            gates = jax.nn.sigmoid(gl_ref[...] + gb_ref[...])
            for g in range(B_KV_HEADS):
                o_sel = _finish(l_s[g], a_s[g])
                o_cmp = ocw_s[g, 0]
                o_win = ocw_s[g, 1]
                for n in range(B_GROUP):
                    r = slice(n * tp, (n + 1) * tp)
                    base = (g * B_GROUP + n) * 3
                    o_ref[g, n] = (gates[:, base:base + 1] * o_cmp[r] + gates[:, base + 1:base + 2] * o_sel[r]
                                   + gates[:, base + 2:base + 3] * o_win[r])


def _nsa_decode(q_half, q_plain, win_t, win_new, sel_new, cache_t, layer, page_table, w_bdt, cconst2,
                gate_logits, gate_b, dec_t, past, k_past):
    bsz = q_plain.shape[0]
    hd = q_plain.shape[-1]
    rows = q_plain.shape[2]
    n_pages = page_table.shape[1]
    page = cache_t.shape[3]
    nwin = win_t.shape[3]
    n_pp = math.gcd(16, n_pages)
    n_ch = n_pages // n_pp
    pitch = 4 * hd + PAGE_PITCH_PAD

    def page_spec(p):
        return pl.BlockSpec((None, None, 4 * hd, page),
                            lambda b, ph, c, pt: (layer, pt[b * n_pages + c * n_pp + p], ph, 0))

    fix = lambda *shape: pl.BlockSpec(shape, lambda b, ph, c, pt: (0,) * len(shape))
    grid_spec = pltpu.PrefetchScalarGridSpec(
        num_scalar_prefetch=1,
        grid=(bsz, 2, n_ch),
        in_specs=[pl.BlockSpec((None, B_KV_HEADS, 2, rows, 2 * hd), lambda b, ph, c, pt: (b, 0, 0, 0, 0)),
                  pl.BlockSpec((None, B_KV_HEADS, rows, hd), lambda b, ph, c, pt: (b, 0, 0, 0)),
                  pl.BlockSpec((None, None, 4 * hd, nwin), lambda b, ph, c, pt: (layer, b, 0, 0)),
                  pl.BlockSpec((None, NEW_PAD, 4 * hd), lambda b, ph, c, pt: (b, 0, 0)),
                  pl.BlockSpec((None, NEW_PAD, 4 * hd), lambda b, ph, c, pt: (b, 0, 0)),
                  fix(*w_bdt.shape), fix(*cconst2.shape),
                  pl.BlockSpec((None, DEC_ROWS, LANE), lambda b, ph, c, pt: (b, 0, 0)),
                  fix(1, LANE)]
        + [page_spec(p) for p in range(n_pp)],
        out_specs=pl.BlockSpec((None, B_KV_HEADS, B_GROUP, DEC_ROWS, hd), lambda b, ph, c, pt: (b, 0, 0, 0, 0)),
        scratch_shapes=[pltpu.VMEM((n_pages * pitch, page), F32),
                        pltpu.VMEM((B_KV_HEADS * n_pages, 2 * hd), F32),
                        pltpu.VMEM((B_KV_HEADS * n_pages, 2 * hd), F32),
                        pltpu.VMEM((B_KV_HEADS, n_ch, rows, n_pp * page), F32),
                        pltpu.VMEM((B_KV_HEADS, 2, rows, hd), F32),
                        pltpu.VMEM((B_KV_HEADS, rows, 1), F32), pltpu.VMEM((B_KV_HEADS, rows, 1), F32),
                        pltpu.VMEM((B_KV_HEADS, rows, hd), F32)],
    )
    return pl.pallas_call(
        functools.partial(_nsa_decode_body, n_pp=n_pp, page=page, past=past, hd=hd, dec_t=dec_t,
                          n_pages=n_pages, k_past=k_past, pitch=pitch),
        grid_spec=grid_spec,
        out_shape=jax.ShapeDtypeStruct((bsz, B_KV_HEADS, B_GROUP, DEC_ROWS, hd), F32),
        compiler_params=_params(("parallel", "arbitrary", "arbitrary")),
        name="nsa_decode",
    )(page_table.reshape(-1), q_half, q_plain, win_t, win_new, sel_new, w_bdt, cconst2, gate_logits, gate_b,
      *([cache_t] * n_pp))


def _split_points(d_model):
    br = d_model // N_BRANCH
    hd_b = br // B_HEADS
    d_dk = br // (2 * D_HEADS)
    splits = (br, br, br, br, br, 6 * B_KV_HEADS * hd_b, 3 * B_HEADS, br, br, br,
              D_HEADS * d_dk, D_HEADS * d_dk, br, GLA_RANK, br, N_BRANCH * d_model)
    return [0] + [int(p) for p in np.cumsum(splits)]


def _layer_weights(l, d_model, w_in, ffn_w_in, ffn_w_out, w_branch, w_out, nsa_cmp_pe, nsa_cmp_w,
                   gla_gate_w2, gla_gate_b, nsa_gate_b):
    pts = _split_points(d_model)
    w = w_in[l]
    seg = lambda i, j: w[:, pts[i]:pts[j]]
    padl = lambda a: jnp.pad(a, ((0, 0), (0, LANE - a.shape[1])))
    wa = seg(0, 4)
    wb = jnp.concatenate([seg(4, 6), padl(seg(6, 7))], axis=1)
    wc = seg(7, 10)
    wd = jnp.concatenate([seg(10, 13), padl(seg(13, 14)), seg(14, 15)], axis=1)
    hd = nsa_cmp_w.shape[-1]
    eye = jnp.eye(B_KV_HEADS, dtype=F32)
    w_bd = jnp.einsum('cjde,gh->cjgdhe', nsa_cmp_w[l], eye).reshape(
        2, NSA_BLOCK, B_KV_HEADS * hd, B_KV_HEADS * hd)
    w_bdt = jnp.einsum('cjde,ab->cdajbe', nsa_cmp_w[l], eye).reshape(
        2, hd // 2, 2 * 2 * NSA_BLOCK, 2 * hd)
    cvec = jnp.einsum('cjd,cjde->ce', nsa_cmp_pe[l], nsa_cmp_w[l])
    cconst = jnp.repeat(cvec, B_KV_HEADS, axis=0).reshape(1, 4 * hd)
    cconst2 = jnp.concatenate([cvec, cvec], axis=1).reshape(2, 1, 2 * hd)
    w2 = jnp.pad(gla_gate_w2[l], ((0, LANE - GLA_RANK), (0, 0)))
    return dict(
        proj=[a.astype(BF16) for a in (wa, wb, wc, wd)],
        w_merge=seg(15, 16).astype(BF16), w_branch=w_branch[l].astype(BF16), w_out=w_out[l].astype(BF16),
        ffn_in=ffn_w_in[l].astype(BF16), ffn_out=ffn_w_out[l].astype(BF16),
        w_bd=w_bd.astype(BF16), w_bdt=w_bdt.astype(BF16), cconst=cconst, cconst2=cconst2,
        gla_w2=w2, gla_b=gla_gate_b[l].reshape(1, -1),
        gate_b=jnp.pad(nsa_gate_b[l].reshape(1, -1), ((0, 0), (0, LANE - 3 * B_HEADS))),
    )


def _mixer_common(x, lw, g2, vec_a, vec_d, bsz, t, state_a, state_d, br, cache_rows_t=False):
    a_dk = br // A_HEADS
    d_dk = br // (2 * D_HEADS)
    d_dv = br // D_HEADS
    ya, yb, yc, yd, *new_t = _proj(x, g2, lw['proj'], br, (bsz, t) if cache_rows_t else None)
    o_a, st_a = _linear_mixer(functools.partial(_hgrn_body, w=br, dk=a_dk), ya, bsz, t, vec_a, state_a,
                              A_HEADS, a_dk, a_dk)
    o_d, st_d = _linear_mixer(functools.partial(_gla_body, wk=D_HEADS * d_dk, wv=br, dk=d_dk, dv=d_dv),
                              yd, bsz, t, vec_d, state_d, D_HEADS, d_dk, d_dv)
    return yb, yc, o_a, st_a, o_d, st_d, new_t


def _feature_major(cache):
    l, n, t = cache.shape[:3]
    return jnp.transpose(cache, (0, 1, 3, 4, 5, 2)).reshape(l, n, -1, t)


def kernel(x_prompt, x_sample, cache_nsa_kv, cache_nsa_win, cache_diff_kv, state_hgrn, state_gla, page_table,
           norm_gains, ffn_w_in, ffn_w_out, w_in, hgrn_lb_logits, hgrn_norm_gain, nsa_cmp_pe, nsa_cmp_w,
           nsa_gate_b, diff_lambda, diff_norm_gain, gla_gate_w2, gla_gate_b, gla_norm_gain, w_branch, w_out):
    bp, tp, d_model = x_prompt.shape
    bs, ts, _ = x_sample.shape
    depth = w_in.shape[0]
    br = d_model // N_BRANCH
    hd = br // B_HEADS
    cd2 = br // C_HEADS
    page = cache_nsa_kv.shape[2]
    n_pages = page_table.shape[1]
    past = n_pages * page
    nwin = cache_nsa_win.shape[2]
    assert past % NSA_BLOCK == 0 and ts <= DEC_ROWS and tp % NSA_BLOCK == 0
    assert past // NSA_BLOCK >= NSA_TOPK and page == 2 * NSA_BLOCK and nwin == NSA_WINDOW
    k_past = NSA_TOPK - 1
    tq = min(256, tp)
    tk = min(128, tq)

    lb_cum = jnp.cumsum(jax.nn.softmax(hgrn_lb_logits.astype(F32), axis=0), axis=0)
    lower = lb_cum - lb_cum[0]
    slopes = 2.0 ** (-np.arange(1, B_HEADS + C_HEADS + 1, dtype=np.float64))
    sl_b = jnp.asarray(slopes[0::2], F32)
    sl_c = jnp.asarray(slopes[1::2], F32)

    nsa_pool_t = _feature_major(cache_nsa_kv)
    diff_pool_t = _feature_major(cache_diff_kv)
    win_pool_t = _feature_major(cache_nsa_win)

    xp = x_prompt.reshape(bp * tp, d_model)
    xs = x_sample.reshape(bs * ts, d_model)
    outs = {k: [] for k in ('kvp', 'kvs', 'winp', 'wins', 'dkp', 'dks', 'hp', 'hs', 'gp', 'gs')}

    for l in range(depth):
        lw = _layer_weights(l, d_model, w_in, ffn_w_in, ffn_w_out, w_branch, w_out, nsa_cmp_pe, nsa_cmp_w,
                            gla_gate_w2, gla_gate_b, nsa_gate_b)
        g = norm_gains[l]
        lb = lower[l].reshape(1, br)
        vec_a = (jnp.log1p(-lb), jnp.log(lb), hgrn_norm_gain[l].reshape(1, br))
        vec_d = (lw['gla_w2'], lw['gla_b'], gla_norm_gain[l].reshape(1, br))
        lv = diff_lambda[l].astype(F32)
        lam_init = 0.8 - 0.6 * math.exp(-0.3 * l)
        lam = (jnp.exp(jnp.sum(lv[0] * lv[1])) - jnp.exp(jnp.sum(lv[2] * lv[3])) + lam_init).reshape(1)

        xp = _ffn(xp, g[0], g[1], lw['ffn_in'][0], lw['ffn_out'][0])
        yb, yc, o_a, st_a, o_d, st_d, (kv_t, win_t, dk_t) = _mixer_common(
            xp, lw, g[2], vec_a, vec_d, bp, tp, None, None, br, cache_rows_t=True)
        outs['hp'].append(st_a)
        outs['gp'].append(st_d)
        outs['kvp'].append(kv_t)
        outs['winp'].append(win_t[:, :, -min(NSA_WINDOW, tp):])
        outs['dkp'].append(dk_t)

        o_c = _diff_prompt(yc, bp, tp, diff_norm_gain[l], lam, sl_c, lam_init, tq, tk)
        cmp_kv = _compress_prompt(yb, bp, tp, lw['w_bd'], lw['cconst'], hd)
        o_b = _nsa_prompt(yb, cmp_kv, bp, tp, lw['gate_b'], sl_b, tq, tk)

        xp = _merge(xp, g[2], g[3], (o_a, o_b, o_c, o_d), lw['w_merge'], lw['w_branch'], lw['w_out'])
        xp = _ffn(xp, g[4], g[5], lw['ffn_in'][1], lw['ffn_out'][1])

        xs = _ffn(xs, g[0], g[1], lw['ffn_in'][0], lw['ffn_out'][0])
        yb, yc, o_a, st_a, o_d, st_d, _ = _mixer_common(xs, lw, g[2], vec_a, vec_d, bs, ts,
                                                        state_hgrn[l], state_gla[l], br)
        outs['hs'].append(st_a)
        outs['gs'].append(st_d)
        new_kv = yb[:, br:br + 6 * B_KV_HEADS * hd].reshape(bs, ts, 6, B_KV_HEADS * hd)
        outs['kvs'].append(new_kv[:, :, :4].reshape(bs, ts, 4, B_KV_HEADS, hd))
        new_win = new_kv[:, :, 4:].reshape(bs, ts, 2 * B_KV_HEADS * hd)
        outs['wins'].append(jnp.concatenate(
            [cache_nsa_win[l][:, ts:], new_win.reshape(bs, ts, 2, B_KV_HEADS, hd)], axis=1))
        outs['dks'].append(yc[:, br:3 * br].reshape(bs, ts, 2, C_HEADS, cd2))
        pad_new = lambda a: jnp.pad(a, ((0, 0), (0, NEW_PAD - ts), (0, 0)))

        cq = yc[:, 0:br].reshape(bs, ts, C_HEADS, 2, cd2 // 2)
        sel = (jnp.arange(C_HEADS)[:, None, None, None] == jnp.arange(C_HEADS)[None, None, :, None]) & \
              (jnp.arange(2)[None, :, None, None] == jnp.arange(2)[None, None, None, :])
        q_rows = jnp.einsum('bthmd,hmgn->bmthgnd', cq, sel.astype(F32)).reshape(bs, 2 * ts * C_HEADS, br)
        new_c = pad_new(yc[:, br:3 * br].reshape(bs, ts, 2 * br))
        o_c = _diff_decode(q_rows, new_c, diff_pool_t, l, page_table, diff_norm_gain[l], lam, lam_init, ts, past)
        o_c = o_c.reshape(bs * ts, br)

        bq = yb[:, 0:br].reshape(bs, ts, B_KV_HEADS, B_GROUP, hd).transpose(0, 2, 3, 1, 4)
        bq = jnp.pad(bq, ((0, 0), (0, 0), (0, 0), (0, DEC_ROWS - ts), (0, 0)))
        q_plain = bq.reshape(bs, B_KV_HEADS, B_GROUP * DEC_ROWS, hd)
        zero = jnp.zeros_like(q_plain)
        q_half = jnp.stack([jnp.concatenate([q_plain, zero], axis=-1),
                            jnp.concatenate([zero, q_plain], axis=-1)], axis=2)
        gl = yb[:, br + 6 * B_KV_HEADS * hd:].reshape(bs, ts, LANE)
        gl = jnp.pad(gl, ((0, 0), (0, DEC_ROWS - ts), (0, 0)))
        o_b = _nsa_decode(q_half, q_plain, win_pool_t, pad_new(new_win),
                          pad_new(new_kv[:, :, 2:4].reshape(bs, ts, 4 * hd)), nsa_pool_t, l, page_table,
                          lw['w_bdt'], lw['cconst2'], gl, lw['gate_b'], ts, past, k_past)
        o_b = o_b[:, :, :, :ts].transpose(0, 3, 1, 2, 4).reshape(bs * ts, br)

        xs = _merge(xs, g[2], g[3], (o_a, o_b, o_c, o_d), lw['w_merge'], lw['w_branch'], lw['w_out'])
        xs = _ffn(xs, g[4], g[5], lw['ffn_in'][1], lw['ffn_out'][1])

    st = lambda k: jnp.stack(outs[k])

    def token_major(k, a, b):
        x = st(k)
        return x.reshape(x.shape[:2] + (a, b, -1, x.shape[-1])).transpose(0, 1, 5, 2, 3, 4)

    return (xp.reshape(bp, tp, d_model), xs.reshape(bs, ts, d_model),
            token_major('kvp', 4, B_KV_HEADS), st('kvs'), token_major('winp', 2, B_KV_HEADS), st('wins'),
            token_major('dkp', 2, C_HEADS), st('dks'), st('hp'), st('hs'), st('gp'), st('gs'))
```

```python
import functools
import math

import numpy as np
import jax
import jax.numpy as jnp
from jax import lax
from jax.experimental import pallas as pl
from jax.experimental.pallas import tpu as pltpu

F32 = jnp.float32
BF16 = jnp.bfloat16
HI = lax.Precision.HIGHEST

N_BRANCH = 4
A_HEADS = 4
B_HEADS = 4
B_KV_HEADS = 2
B_GROUP = 2
C_HEADS = 4
D_HEADS = 4
NSA_BLOCK = 64
NSA_TOPK = 16
NSA_WINDOW = 512
GLA_RANK = 16
GLA_TAU = 16.0
NORM_EPS = 1e-6
LANE = 128
LIN_GROUP = 16
NEW_PAD = 16
DEC_ROWS = 8
ONES_PAD = 16
PAGE_PITCH_PAD = 8
VMEM_LIMIT = 56 * 1024 * 1024
LOG2E = 1.4426950408889634
NEG_BIG = -1e30

NT_DIMS = (((1,), (1,)), ((), ()))
TN_DIMS = (((0,), (0,)), ((), ()))


def _params(semantics):
    return pltpu.CompilerParams(dimension_semantics=semantics, vmem_limit_bytes=VMEM_LIMIT)


def _rms(x, g):
    return x * lax.rsqrt(jnp.mean(x * x, axis=-1, keepdims=True) + NORM_EPS) * g


def _dot(a, b):
    return jnp.dot(a, b, preferred_element_type=F32)


def _dot_nt(a, b):
    return lax.dot_general(a, b, NT_DIMS, preferred_element_type=F32)


def _row_tile(n, cap):
    t = min(n, cap)
    while n % t or t % 8:
        t -= 1
    return t


def _ffn_body(x_ref, gpre_ref, gpost_ref, wg_ref, wu_ref, wo_ref, o_ref, xn_ref, acc_ref):
    f = pl.program_id(1)

    @pl.when(f == 0)
    def _():
        xn_ref[...] = _rms(x_ref[...], gpre_ref[...]).astype(BF16)
        acc_ref[...] = jnp.zeros_like(acc_ref)

    xn = xn_ref[...]
    gate = _dot(xn, wg_ref[...])
    up = _dot(xn, wu_ref[...])
    act = (gate * jax.nn.sigmoid(gate) * up).astype(BF16)
    acc_ref[...] += _dot(act, wo_ref[...])

    @pl.when(f == pl.num_programs(1) - 1)
    def _():
        o_ref[...] = x_ref[...] + 0.5 * _rms(acc_ref[...], gpost_ref[...])


def _ffn(x, g_pre, g_post, w_in, w_out):
    n, d = x.shape
    dff = w_out.shape[0]
    tf = 256
    nf = dff // tf
    tm = _row_tile(n, 1024)
    return pl.pallas_call(
        _ffn_body,
        grid=(n // tm, nf),
        in_specs=[
            pl.BlockSpec((tm, d), lambda i, f: (i, 0)),
            pl.BlockSpec((1, d), lambda i, f: (0, 0)),
            pl.BlockSpec((1, d), lambda i, f: (0, 0)),
            pl.BlockSpec((d, tf), lambda i, f: (0, f)),
            pl.BlockSpec((d, tf), lambda i, f: (0, nf + f)),
            pl.BlockSpec((tf, d), lambda i, f: (f, 0)),
        ],
        out_specs=pl.BlockSpec((tm, d), lambda i, f: (i, 0)),
        out_shape=jax.ShapeDtypeStruct((n, d), F32),
        scratch_shapes=[pltpu.VMEM((tm, d), BF16), pltpu.VMEM((tm, d), F32)],
        compiler_params=_params(("parallel", "arbitrary")),
        name="ffn",
    )(x, g_pre.reshape(1, d), g_post.reshape(1, d), w_in, w_in, w_out)


def _proj_body(x_ref, g_ref, wa, wb, wc, wd, oa, ob, oc, od, *cache_t, br):
    xn = _rms(x_ref[...], g_ref[...]).astype(BF16)
    ys = [_dot_nt(xn, w[...]) for w in (wa, wb, wc, wd)]
    for y, o in zip(ys, (oa, ob, oc, od)):
        o[...] = y
    if cache_t:
        kv_t, win_t, dk_t = cache_t
        kv_t[...] = ys[1][:, br:3 * br].T
        win_t[...] = ys[1][:, 3 * br:4 * br].T
        dk_t[...] = ys[2][:, br:3 * br].T


def _proj(x, g, ws, br, seqs=None):
    n, d = x.shape
    tm = _row_tile(n, 512)
    out_specs = [pl.BlockSpec((tm, w.shape[0]), lambda i: (i, 0)) for w in ws]
    out_shape = [jax.ShapeDtypeStruct((n, w.shape[0]), F32) for w in ws]
    if seqs is not None:
        bsz, t = seqs
        per = t // tm
        for rows in (2 * br, br, 2 * br):
            out_specs.append(pl.BlockSpec((None, rows, tm), lambda i: (i // per, 0, i % per)))
            out_shape.append(jax.ShapeDtypeStruct((bsz, rows, t), F32))
    return pl.pallas_call(
        functools.partial(_proj_body, br=br),
        grid=(n // tm,),
        in_specs=[pl.BlockSpec((tm, d), lambda i: (i, 0)), pl.BlockSpec((1, d), lambda i: (0, 0))]
        + [pl.BlockSpec(w.shape, lambda i: (0, 0)) for w in ws],
        out_specs=out_specs,
        out_shape=out_shape,
        compiler_params=_params(("parallel",)),
        name="mixer_in_proj",
    )(x, g.reshape(1, d), *ws)


def _merge_body(x_ref, g2_ref, g3_ref, oa, ob, oc, od, wm_ref, wb_ref, wo_ref, out_ref):
    x = x_ref[...]
    d = x.shape[1]
    h = _rms(x, g2_ref[...]).astype(BF16)
    s = None
    for n, br in enumerate((oa, ob, oc, od)):
        gate = jax.nn.sigmoid(_dot_nt(h, wm_ref[n * d:(n + 1) * d, :]))
        term = gate * _dot(br[...].astype(BF16), wb_ref[n])
        s = term if s is None else s + term
    y = _dot(s.astype(BF16), wo_ref[...])
    out_ref[...] = x + _rms(y, g3_ref[...])


def _merge(x, g2, g3, branches, w_merge, w_branch, w_out):
    n, d = x.shape
    br = branches[0].shape[1]
    tm = _row_tile(n, 256)
    row = lambda i: (i, 0)
    fix2 = lambda i: (0, 0)
    return pl.pallas_call(
        _merge_body,
        grid=(n // tm,),
        in_specs=[pl.BlockSpec((tm, d), row), pl.BlockSpec((1, d), fix2), pl.BlockSpec((1, d), fix2)]
        + [pl.BlockSpec((tm, br), row)] * N_BRANCH
        + [pl.BlockSpec(w_merge.shape, fix2), pl.BlockSpec(w_branch.shape, lambda i: (0, 0, 0)),
           pl.BlockSpec(w_out.shape, fix2)],
        out_specs=pl.BlockSpec((tm, d), row),
        out_shape=jax.ShapeDtypeStruct((n, d), F32),
        compiler_params=_params(("parallel",)),
        name="merge",
    )(x, g2.reshape(1, d), g3.reshape(1, d), *branches, w_merge, w_branch, w_out)


def _softplus_neg_abs(z):
    return jnp.log(1.0 + jnp.exp(-jnp.abs(z)))


def _log_sigmoid(z):
    return jnp.minimum(z, 0.0) - _softplus_neg_abs(z)


def _dot_exact_lhs(a, x):
    hi = x.astype(BF16)
    lo = (x - hi.astype(F32)).astype(BF16)
    return _dot(a, hi) + _dot(a, lo)


def _lin_core(q, k, v, lf, gg, gain, s0_ref, o_ref, st_ref, qt_s, kt_s, vt_s, dec_s, oi_s, st_s,
              *, dk, dv, t_valid, n_seq):
    rows, wk = q.shape
    tt = rows // n_seq
    wv = v.shape[1]
    grp = LIN_GROUP
    t_idx = pl.program_id(1)

    @pl.when(t_idx == 0)
    def _():
        st_s[...] = s0_ref[...]

    if t_valid is not None:
        tok = t_idx * tt + lax.broadcasted_iota(jnp.int32, (rows, 1), 0) % tt
        lf = jnp.where(tok < t_valid, lf, 0.0)

    r = lax.broadcasted_iota(jnp.int32, (tt, tt), 0)
    c = lax.broadcasted_iota(jnp.int32, (tt, tt), 1)
    same = (r // grp) == (c // grp)
    tri = jnp.where(same & (c <= r), 1.0, 0.0).astype(BF16)
    ones_g = jnp.where(same, 1.0, 0.0).astype(BF16)
    lfs = [lf[si * tt:(si + 1) * tt] for si in range(n_seq)]
    b = jnp.concatenate([_dot_exact_lhs(tri, x) for x in lfs], axis=0)
    bl = jnp.concatenate([_dot_exact_lhs(ones_g, x) for x in lfs], axis=0)

    qt_s[...] = (q * jnp.exp(b)).astype(BF16)
    kt_s[...] = (k * jnp.exp(bl - b)).astype(BF16)
    vt_s[...] = v.astype(BF16)
    dec_s[...] = jnp.exp(bl)

    hk = lax.broadcasted_iota(jnp.int32, (wk, wv), 0) // dk
    hv = lax.broadcasted_iota(jnp.int32, (wk, wv), 1) // dv
    ones_hd = jnp.where(hk == hv, 1.0, 0.0).astype(BF16)

    rowm = lax.broadcasted_iota(jnp.int32, (rows, 1), 0) % grp
    od = jnp.zeros((rows, wv), F32)
    for d in range(grp):
        ks = k if d == 0 else pltpu.roll(k, d, 0)
        bs = b if d == 0 else pltpu.roll(b, d, 0)
        vs = v if d == 0 else pltpu.roll(v, d, 0)
        e = jnp.exp(jnp.minimum(b - bs, 0.0))
        z = jnp.where(rowm >= d, q * ks * e, 0.0).astype(BF16)
        od = od + _dot(z, ones_hd) * vs

    mv = lax.broadcasted_iota(jnp.int32, (wv, wk), 0) // dv
    mk = lax.broadcasted_iota(jnp.int32, (wv, wk), 1) // dk
    mbd = jnp.where(mv == mk, 1.0, 0.0).astype(F32)

    def step(i, carry):
        for si in range(n_seq):
            r0 = pl.multiple_of(si * tt + i * grp, grp)
            qg = qt_s[pl.ds(r0, grp), :]
            kg = kt_s[pl.ds(r0, grp), :]
            vg = vt_s[pl.ds(r0, grp), :]
            s = st_s[si]
            oi_s[pl.ds(r0, grp), :] = _dot_nt(qg, s.astype(BF16))
            upd = lax.dot_general(vg, kg, TN_DIMS, preferred_element_type=F32)
            st_s[si] = dec_s[pl.ds(r0, 1), :] * s + mbd * upd
        return carry

    lax.fori_loop(0, tt // grp, step, 0)

    o = oi_s[...] + od
    pv = lax.broadcasted_iota(jnp.int32, (wv, wv), 0) // dv
    pw = lax.broadcasted_iota(jnp.int32, (wv, wv), 1) // dv
    avg = jnp.where(pv == pw, 1.0 / dv, 0.0).astype(BF16)
    o2 = o * o
    o2_hi = o2.astype(BF16)
    ms = _dot(o2_hi, avg) + _dot((o2 - o2_hi.astype(F32)).astype(BF16), avg)
    y = o * lax.rsqrt(ms + NORM_EPS) * gain * (gg * jax.nn.sigmoid(gg))
    o_ref[...] = y.reshape(o_ref.shape)

    @pl.when(t_idx == pl.num_programs(1) - 1)
    def _():
        st_ref[...] = st_s[...]


def _hgrn_body(y_ref, la_ref, lc_ref, gain_ref, s0_ref, o_ref, st_ref, *scratch, w, dk, t_valid):
    n_seq = y_ref.shape[0]
    y = y_ref[...].reshape(-1, y_ref.shape[2])
    q = y[:, 0:w] * (dk ** -0.5)
    z = y[:, w:2 * w]
    v = y[:, 2 * w:3 * w]
    gg = y[:, 3 * w:4 * w]
    ls = _log_sigmoid(z)
    a = la_ref[...] + ls
    cc = lc_ref[...]
    lf = jnp.maximum(a, cc) + _softplus_neg_abs(a - cc)
    k = jnp.exp(la_ref[...] + (ls - z))
    _lin_core(q, k, v, lf, gg, gain_ref[...], s0_ref, o_ref, st_ref, *scratch, dk=dk, dv=dk, t_valid=t_valid,
              n_seq=n_seq)


def _gla_body(y_ref, w2_ref, b2_ref, gain_ref, s0_ref, o_ref, st_ref, *scratch, wk, wv, dk, dv, t_valid):
    n_seq = y_ref.shape[0]
    y = y_ref[...].reshape(-1, y_ref.shape[2])
    q = y[:, 0:wk] * (dk ** -0.5)
    k = y[:, wk:2 * wk]
    v = y[:, 2 * wk:2 * wk + wv]
    lr = y[:, 2 * wk + wv:2 * wk + wv + LANE]
    gg = y[:, 2 * wk + wv + LANE:2 * wk + 2 * wv + LANE]
    u = jnp.dot(lr, w2_ref[...], precision=HI, preferred_element_type=F32) + b2_ref[...]
    lf = _log_sigmoid(u) * (1.0 / GLA_TAU)
    _lin_core(q, k, v, lf, gg, gain_ref[...], s0_ref, o_ref, st_ref, *scratch, dk=dk, dv=dv, t_valid=t_valid,
              n_seq=n_seq)


def _lin_call(body, y, n_seq, tt, vecs, s0):
    bsz, t, cw = y.shape
    wv, wk = s0.shape[1:]
    rows = n_seq * tt
    return pl.pallas_call(
        body,
        grid=(bsz // n_seq, t // tt),
        in_specs=[pl.BlockSpec((n_seq, tt, cw), lambda b, i: (b, i, 0))]
        + [pl.BlockSpec(a.shape, lambda b, i: (0, 0)) for a in vecs]
        + [pl.BlockSpec((n_seq, wv, wk), lambda b, i: (b, 0, 0))],
        out_specs=[pl.BlockSpec((n_seq, tt, wv), lambda b, i: (b, i, 0)),
                   pl.BlockSpec((n_seq, wv, wk), lambda b, i: (b, 0, 0))],
        out_shape=[jax.ShapeDtypeStruct((bsz, t, wv), F32), jax.ShapeDtypeStruct((bsz, wv, wk), F32)],
        scratch_shapes=[pltpu.VMEM((rows, wk), BF16), pltpu.VMEM((rows, wk), BF16), pltpu.VMEM((rows, wv), BF16),
                        pltpu.VMEM((rows, wk), F32), pltpu.VMEM((rows, wv), F32), pltpu.VMEM((n_seq, wv, wk), F32)],
        compiler_params=_params(("parallel", "arbitrary")),
        name="gated_linear",
    )(y, *vecs, s0)


def _state_to_bd(state):
    bsz, nh, dk, dv = state.shape
    eye = jnp.eye(nh, dtype=state.dtype)
    return jnp.einsum('bhkv,hg->bhvgk', state, eye).reshape(bsz, nh * dv, nh * dk)


def _bd_to_state(st, nh):
    bsz, wv, wk = st.shape
    dv, dk = wv // nh, wk // nh
    blocks = st.reshape(bsz, nh, dv, nh, dk)
    idx = jnp.arange(nh)
    return blocks[:, idx, :, idx, :].transpose(1, 0, 3, 2)


def _linear_mixer(body, y, bsz, t, vecs, state0, nh, dk, dv):
    assert dv & (dv - 1) == 0
    tp = -(-t // LIN_GROUP) * LIN_GROUP
    y3 = y.reshape(bsz, t, -1)
    if tp != t:
        y3 = jnp.pad(y3, ((0, 0), (0, tp - t), (0, 0)))
    tt = min(tp, LANE)
    n_seq = math.gcd(bsz, min(8, max(1, 512 // tt)))
    wk, wv = nh * dk, nh * dv
    s0 = jnp.zeros((bsz, wv, wk), F32) if state0 is None else _state_to_bd(state0)
    body = functools.partial(body, t_valid=None if tp == t else t)
    o, st = _lin_call(body, y3, n_seq, tt, vecs, s0)
    return o[:, :t].reshape(bsz * t, wv), _bd_to_state(st, nh)


def _col_update(idx, tiles, m_s, acc_s):
    m_old = m_s[idx]
    m_new = m_old
    for s, _, shift in tiles:
        smax = jnp.max(s, axis=0, keepdims=True)
        m_new = jnp.maximum(m_new, smax if shift is None else smax + shift)
    m_safe = jnp.where(m_new == -jnp.inf, 0.0, m_new)
    pv = None
    for s, vt, shift in tiles:
        p = jnp.exp2(s + ((-m_safe) if shift is None else (shift - m_safe)))
        d = _dot(vt, p.astype(BF16))
        pv = d if pv is None else pv + d
    acc_s[idx] = jnp.exp2(m_old - m_safe) * acc_s[idx] + pv
    m_s[idx] = m_new


def _col_finish(acc, dv):
    den = acc[dv:dv + 1, :]
    return acc[:dv, :] / jnp.where(den > 0, den, 1.0)


def _col_reset(m_s, acc_s):
    m_s[...] = jnp.full(m_s.shape, -jnp.inf, F32)
    acc_s[...] = jnp.zeros(acc_s.shape, F32)


def _online(s, v, m, l, a, v_t=False):
    m_new = jnp.maximum(m, jnp.max(s, axis=1, keepdims=True))
    m_safe = jnp.where(m_new == -jnp.inf, 0.0, m_new)
    p = jnp.exp2(s - m_safe)
    alpha = jnp.exp2(m - m_safe)
    l = alpha * l + jnp.sum(p, axis=1, keepdims=True)
    pv = _dot_nt(p.astype(BF16), v) if v_t else _dot(p.astype(BF16), v)
    return m_new, l, alpha * a + pv


def _finish(l, a):
    return a / jnp.where(l > 0, l, 1.0)


def _softmax_init(rows, width):
    return (jnp.full((rows, 1), -jnp.inf, F32), jnp.zeros((rows, 1), F32), jnp.zeros((rows, width), F32))


def _stage_keys_values(k_ref, v_ref, kb_s, vt_s, *, t, tk, w, k_lanes):
    kw = kb_s.shape[2]
    n_heads = vt_s.shape[0]
    ones = jnp.where(lax.broadcasted_iota(jnp.int32, (ONES_PAD, tk), 0) == 0, 1.0, 0.0).astype(BF16)

    def body(jt, carry):
        rows = pl.ds(pl.multiple_of(jt * tk, tk), tk)
        kt = k_ref[rows, :]
        for c, lo in enumerate(k_lanes):
            kb_s[c, rows, :] = kt[:, lo:lo + kw].astype(BF16)
        vt = v_ref[rows, :].T
        for hh in range(n_heads):
            vt_s[hh, jt, 0:w, :] = vt[hh * w:(hh + 1) * w, :].astype(BF16)
            vt_s[hh, jt, w:w + ONES_PAD, :] = ones
        return carry

    lax.fori_loop(0, t // tk, body, 0)


def _diff_prompt_body(sl_ref, lam_ref, q_ref, k_ref, v_ref, gain_ref, o_ref, kb_s, vt_s, boff_s, m_s, acc_s, s_s,
                      *, t, tq, tk, cd, lam_init):
    hp = pl.program_id(1)
    i = pl.program_id(2)
    w = 2 * cd
    pair = LANE // w

    lanes = [hh * w + mi * cd for hh in range(pair) for mi in range(2)]

    @pl.when(i == 0)
    def _():
        _stage_keys_values(k_ref, v_ref, kb_s, vt_s, t=t, tk=tk, w=w, k_lanes=lanes)

    qt = (q_ref[...] * (cd ** -0.5 * LOG2E)).T
    slope2 = [sl_ref[hp * pair + hh] * LOG2E for hh in range(pair)]
    chains = [(ci // 2, qt[lo:lo + cd, :].astype(BF16)) for ci, lo in enumerate(lanes)]
    rel = lax.broadcasted_iota(jnp.int32, (tk, tq), 1) - lax.broadcasted_iota(jnp.int32, (tk, tq), 0)
    rel_f = rel.astype(F32)
    for hh in range(pair):
        boff_s[hh] = rel_f * (-slope2[hh])
    _col_reset(m_s, acc_s)
    ratio = tq // tk

    def score_span(jj):
        for ci, (_, qc) in enumerate(chains):
            for r in range(ratio):
                s_s[jj % 2, ci, r] = _dot(kb_s[ci, pl.ds(pl.multiple_of((jj * ratio + r) * tk, tk), tk), :], qc)

    score_span(0)

    def off_group(jj, carry):
        for ci, (hh, _) in enumerate(chains):
            boff = boff_s[hh]
            _col_update(ci, [(s_s[jj % 2, ci, r] + boff, vt_s[hh, jj * ratio + r],
                              ((i - jj) * tq - r * tk).astype(F32) * (-slope2[hh])) for r in range(ratio)],
                        m_s, acc_s)
        score_span(jj + 1)
        return carry

    lax.fori_loop(0, i, off_group, 0)
    dists = [rel - r * tk for r in range(ratio)]
    for ci, (hh, _) in enumerate(chains):
        _col_update(ci, [(s_s[i % 2, ci, r] + jnp.where(dists[r] >= 0, dists[r].astype(F32) * (-slope2[hh]), -jnp.inf),
                          vt_s[hh, i * ratio + r], None) for r in range(ratio)], m_s, acc_s)

    outs = []
    for hh in range(pair):
        o = _col_finish(acc_s[2 * hh], w) - lam_ref[0] * _col_finish(acc_s[2 * hh + 1], w)
        ms = jnp.mean(o * o, axis=0, keepdims=True)
        outs.append(o * lax.rsqrt(ms + NORM_EPS))
    o_ref[...] = (jnp.concatenate(outs, axis=0) * gain_ref[...] * (1.0 - lam_init)).T


def _diff_prompt(yc, bsz, t, gain, lam, slopes, lam_init, tq, tk):
    w = gain.shape[0]
    nh = yc.shape[1] // (3 * w)
    pair = LANE // w
    nhp = nh // pair
    nq = t // tq
    dva = w + ONES_PAD
    smem = pl.BlockSpec(memory_space=pltpu.SMEM)
    return pl.pallas_call(
        functools.partial(_diff_prompt_body, t=t, tq=tq, tk=tk, cd=w // 2, lam_init=lam_init),
        grid=(bsz, nhp, nq),
        in_specs=[smem, smem,
                  pl.BlockSpec((tq, LANE), lambda b, hp, i: (b * nq + i, hp)),
                  pl.BlockSpec((t, LANE), lambda b, hp, i: (b, nhp + hp)),
                  pl.BlockSpec((t, LANE), lambda b, hp, i: (b, 2 * nhp + hp)),
                  pl.BlockSpec((LANE, 1), lambda b, hp, i: (0, 0))],
        out_specs=pl.BlockSpec((tq, LANE), lambda b, hp, i: (b * nq + i, hp)),
        out_shape=jax.ShapeDtypeStruct((bsz * t, nh * w), F32),
        scratch_shapes=[pltpu.VMEM((2 * pair, t, w // 2), BF16), pltpu.VMEM((pair, t // tk, dva, tk), BF16),
                        pltpu.VMEM((pair, tk, tq), F32), pltpu.VMEM((2 * pair, 1, tq), F32),
                        pltpu.VMEM((2 * pair, dva, tq), F32), pltpu.VMEM((2, 2 * pair, tq // tk, tk, tq), F32)],
        compiler_params=_params(("parallel", "parallel", "arbitrary")),
        name="diff_attn_prompt",
    )(slopes, lam, yc, yc, yc, jnp.tile(gain, pair).reshape(LANE, 1))


def _compress(xk_ref, xv_ref, w_ref, nb):
    def body(j, acc):
        ak, av = acc
        xk = xk_ref[pl.ds(j, nb, stride=NSA_BLOCK), :].astype(BF16)
        xv = xv_ref[pl.ds(j, nb, stride=NSA_BLOCK), :].astype(BF16)
        return ak + _dot(xk, w_ref[0, j]), av + _dot(xv, w_ref[1, j])

    zero = jnp.zeros((nb, xk_ref.shape[1]), F32)
    ak, av = lax.fori_loop(0, NSA_BLOCK, body, (zero, zero))
    return jnp.concatenate([ak, av], axis=1)


def _compress_body(xk_ref, xv_ref, w_ref, c_ref, o_ref, *, nb, hd):
    acc = _compress(xk_ref, xv_ref, w_ref, nb) + c_ref[...]
    for p in range(4):
        o_ref[p] = acc[:, p * hd:(p + 1) * hd]


def _compress_prompt(yb, bsz, t, w_bd, cconst, hd):
    nb = t // NSA_BLOCK
    cw = 4 * hd
    hw = cw // 2
    return pl.pallas_call(
        functools.partial(_compress_body, nb=nb, hd=hd),
        grid=(bsz,),
        in_specs=[pl.BlockSpec((t, hw), lambda b: (b, 2)),
                  pl.BlockSpec((t, hw), lambda b: (b, 3)),
                  pl.BlockSpec(w_bd.shape, lambda b: (0, 0, 0, 0)),
                  pl.BlockSpec((1, cw), lambda b: (0, 0))],
        out_specs=pl.BlockSpec((None, 4, nb, hd), lambda b: (b, 0, 0, 0)),
        out_shape=jax.ShapeDtypeStruct((bsz, 4, nb, hd), F32),
        compiler_params=_params(("parallel",)),
        name="nsa_compress_prompt",
    )(yb, yb, w_bd, cconst)


def _rank_rows(score, tq):
    nb = score.shape[0]
    sub = 8
    slabs = [score[v * sub:(v + 1) * sub, :] for v in range(nb // sub)]
    ranks = [jnp.zeros((sub, tq), F32) for _ in slabs]
    sub_i = lax.broadcasted_iota(jnp.int32, (sub, 1), 0)
    for ib in range(nb):
        row = score[ib:ib + 1, :]
        for v, slab in enumerate(slabs):
            if ib < v * sub:
                ahead = jnp.where(row >= slab, 1.0, 0.0)
            elif ib >= (v + 1) * sub:
                ahead = jnp.where(row > slab, 1.0, 0.0)
            else:
                ahead = jnp.where(sub_i > ib - v * sub, jnp.where(row >= slab, 1.0, 0.0),
                                  jnp.where(row > slab, 1.0, 0.0))
            ranks[v] = ranks[v] + ahead
    return jnp.concatenate(ranks, axis=0)


def _nsa_prompt_body(sl_ref, q_ref, cmp_ref, ks_ref, vs_ref, kw_ref, vw_ref, gl_ref, gb_ref, o_ref,
                     ksb_s, kwb_s, vst_s, vwt_s, boff_s, m_s, acc_s, s_s, *, t, tq, tk, nb, hd, k_sel):
    i = pl.program_id(1)
    heads = [(g, n) for g in range(B_KV_HEADS) for n in range(B_GROUP)]
    nh = len(heads)

    @pl.when(i == 0)
    def _():
        g_lanes = [g * hd for g in range(B_KV_HEADS)]
        _stage_keys_values(ks_ref, vs_ref, ksb_s, vst_s, t=t, tk=tk, w=hd, k_lanes=g_lanes)
        _stage_keys_values(kw_ref, vw_ref, kwb_s, vwt_s, t=t, tk=tk, w=hd, k_lanes=g_lanes)

    qt = (q_ref[...] * (hd ** -0.5 * LOG2E)).T
    qh = [qt[hi * hd:(hi + 1) * hd, :].astype(BF16) for hi in range(nh)]
    sl2 = [sl_ref[hi] * LOG2E for hi in range(nh)]
    qpos = i * tq + lax.broadcasted_iota(jnp.int32, (1, tq), 1)

    blk = lax.broadcasted_iota(jnp.int32, (nb, 1), 0)
    distc = qpos - (blk * NSA_BLOCK + (NSA_BLOCK - 1))
    distc_f = distc.astype(F32)
    cur = qpos // NSA_BLOCK
    forced = (blk == 0) | (blk == cur) | (blk == cur - 1)
    o_cmp = []
    selneg = []
    for g in range(B_KV_HEADS):
        kc = cmp_ref[g].astype(BF16)
        vc = cmp_ref[B_KV_HEADS + g].astype(BF16)
        imp = None
        for n in range(B_GROUP):
            hi = g * B_GROUP + n
            s = jnp.where(distc >= 0, _dot(kc, qh[hi]) - sl2[hi] * distc_f, -jnp.inf)
            m = jnp.max(s, axis=0, keepdims=True)
            e = jnp.exp2(s - jnp.where(m == -jnp.inf, 0.0, m))
            den = jnp.sum(e, axis=0, keepdims=True)
            p = e / jnp.where(den > 0, den, 1.0)
            o_cmp.append(lax.dot_general(vc, p.astype(BF16), TN_DIMS, preferred_element_type=F32))
            imp = p if imp is None else imp + p
        score = jnp.where(blk > cur, -jnp.inf, jnp.where(forced, jnp.inf, imp))
        selneg.append(jnp.where(_rank_rows(score, tq) < k_sel, 0.0, NEG_BIG).astype(BF16))

    rel = lax.broadcasted_iota(jnp.int32, (tk, tq), 1) - lax.broadcasted_iota(jnp.int32, (tk, tq), 0)
    rel_f = rel.astype(F32)
    for hi in range(nh):
        boff_s[hi] = rel_f * (-sl2[hi])
    ratio = tq // tk
    erow = lax.broadcasted_iota(jnp.int32, (tk, nb), 0)
    eblk = lax.broadcasted_iota(jnp.int32, (tk, nb), 1)

    def block_masks(j):
        expand = jnp.where((erow + j * tk) // NSA_BLOCK == eblk, 1.0, 0.0).astype(BF16)
        return [_dot(expand, sn) for sn in selneg]

    _col_reset(m_s, acc_s)

    def key_tiles(kb_s, jj):
        return [[kb_s[g, pl.ds(pl.multiple_of((jj * ratio + r) * tk, tk), tk), :] for r in range(ratio)]
                for g in range(B_KV_HEADS)]

    def score_span(jj):
        kts = key_tiles(ksb_s, jj)
        mbs = [block_masks(jj * ratio + r) for r in range(ratio)]
        for hi, (g, _) in enumerate(heads):
            for r in range(ratio):
                s_s[jj % 2, hi, r] = _dot(kts[g][r], qh[hi]) + mbs[r][g]

    score_span(0)

    def sel_off(jj, carry):
        for hi, (g, _) in enumerate(heads):
            boff = boff_s[hi]
            _col_update(hi, [(s_s[jj % 2, hi, r] + boff, vst_s[g, jj * ratio + r],
                              ((i - jj) * tq - r * tk).astype(F32) * (-sl2[hi])) for r in range(ratio)], m_s, acc_s)
        score_span(jj + 1)
        return carry

    lax.fori_loop(0, i, sel_off, 0)
    dists = [rel - r * tk for r in range(ratio)]
    for hi, (g, _) in enumerate(heads):
        _col_update(hi, [(s_s[i % 2, hi, r] + jnp.where(dists[r] >= 0, dists[r].astype(F32) * (-sl2[hi]), -jnp.inf),
                          vst_s[g, i * ratio + r], None) for r in range(ratio)], m_s, acc_s)
    o_sel = [_col_finish(acc_s[hi], hd) for hi in range(nh)]

    _col_reset(m_s, acc_s)
    for ds in range(-(NSA_WINDOW // tq), 1):
        def win_span(ds=ds):
            kts = key_tiles(kwb_s, i + ds)
            dists = [rel - (ds * tq + r * tk) for r in range(ratio)]
            for hi, (g, _) in enumerate(heads):
                _col_update(hi, [(_dot(kts[g][r], qh[hi])
                                  + jnp.where((dists[r] >= 0) & (dists[r] < NSA_WINDOW),
                                              dists[r].astype(F32) * (-sl2[hi]), -jnp.inf),
                                  vwt_s[g, (i + ds) * ratio + r], None) for r in range(ratio)], m_s, acc_s)

        if ds < 0:
            pl.when(i + ds >= 0)(win_span)
        else:
            win_span()
    o_win = [_col_finish(acc_s[hi], hd) for hi in range(nh)]

    gates = jax.nn.sigmoid(gl_ref[...] + gb_ref[...]).T
    outs = [gates[3 * hi:3 * hi + 1] * o_cmp[hi] + gates[3 * hi + 1:3 * hi + 2] * o_sel[hi]
            + gates[3 * hi + 2:3 * hi + 3] * o_win[hi] for hi in range(nh)]
    o_ref[...] = jnp.concatenate(outs, axis=0).T


def _nsa_prompt(yb, cmp_kv, bsz, t, gate_b, slopes, tq, tk):
    nb, hd = cmp_kv.shape[2], cmp_kv.shape[3]
    qw = B_HEADS * hd
    nq = t // tq
    first = qw // LANE
    dva = hd + ONES_PAD
    kv_spec = lambda c: pl.BlockSpec((t, LANE), lambda b, i: (b, first + c))
    return pl.pallas_call(
        functools.partial(_nsa_prompt_body, t=t, tq=tq, tk=tk, nb=nb, hd=hd, k_sel=min(NSA_TOPK, nb)),
        grid=(bsz, nq),
        in_specs=[pl.BlockSpec(memory_space=pltpu.SMEM),
                  pl.BlockSpec((tq, qw), lambda b, i: (b * nq + i, 0)),
                  pl.BlockSpec((None, 2 * B_KV_HEADS, nb, hd), lambda b, i: (b, 0, 0, 0)),
                  kv_spec(2), kv_spec(3), kv_spec(4), kv_spec(5),
                  pl.BlockSpec((tq, LANE), lambda b, i: (b * nq + i, first + 6)),
                  pl.BlockSpec((1, LANE), lambda b, i: (0, 0))],
        out_specs=pl.BlockSpec((tq, qw), lambda b, i: (b * nq + i, 0)),
        out_shape=jax.ShapeDtypeStruct((bsz * t, qw), F32),
        scratch_shapes=[pltpu.VMEM((B_KV_HEADS, t, hd), BF16), pltpu.VMEM((B_KV_HEADS, t, hd), BF16),
                        pltpu.VMEM((B_KV_HEADS, t // tk, dva, tk), BF16),
                        pltpu.VMEM((B_KV_HEADS, t // tk, dva, tk), BF16),
                        pltpu.VMEM((B_HEADS, tk, tq), F32), pltpu.VMEM((B_HEADS, 1, tq), F32),
                        pltpu.VMEM((B_HEADS, dva, tq), F32), pltpu.VMEM((2, B_HEADS, tq // tk, tk, tq), F32)],
        compiler_params=_params(("parallel", "arbitrary")),
        name="nsa_attn_prompt",
    )(slopes, yb, cmp_kv, yb, yb, yb, yb, yb, gate_b)


def _diff_decode_body(pt_ref, lam_ref, q_ref, new_ref, gain_ref, *refs, n_pp, page, past, nh, cd, lam_init, dec_t):
    pages = refs[:n_pp]
    o_ref = refs[n_pp]
    m_s, l_s, a_s = refs[n_pp + 1:]
    c = pl.program_id(1)
    w = 2 * cd * nh
    rows = 2 * dec_t * nh
    row = lax.broadcasted_iota(jnp.int32, (rows, 1), 0)
    rh = row % nh
    rt = (row % (dec_t * nh)) // nh
    slope2 = jnp.zeros((rows, 1), F32)
    for h in range(nh):
        slope2 = jnp.where(rh == h, 2.0 ** (-2 * (h + 1)) * LOG2E, slope2)
    qb = (q_ref[...] * (cd ** -0.5 * LOG2E)).astype(BF16)

    @pl.when(c == 0)
    def _():
        new = new_ref[...]
        kn = new[:, :w].astype(BF16)
        vn = new[:, w:].astype(BF16)
        col = lax.broadcasted_iota(jnp.int32, (1, new.shape[0]), 1)
        dist = rt - col
        s = jnp.where((dist >= 0) & (col < dec_t), _dot_nt(qb, kn) - slope2 * dist.astype(F32), -jnp.inf)
        m_s[...], l_s[...], a_s[...] = _online(s, vn, *_softmax_init(rows, w))

    kt = jnp.concatenate([pages[p][0:w, :].astype(BF16) for p in range(n_pp)], axis=1)
    vt = jnp.concatenate([pages[p][w:2 * w, :].astype(BF16) for p in range(n_pp)], axis=1)
    col = lax.broadcasted_iota(jnp.int32, (1, n_pp * page), 1)
    dist = (past + rt) - (c * (n_pp * page) + col)
    s = _dot(qb, kt) - slope2 * dist.astype(F32)
    m, l, a = _online(s, vt, m_s[...], l_s[...], a_s[...], v_t=True)
    m_s[...], l_s[...], a_s[...] = m, l, a

    @pl.when(c == pl.num_programs(1) - 1)
    def _():
        full = _finish(l, a)
        own = jnp.zeros((rows, 2 * cd), F32)
        for h in range(nh):
            own = jnp.where(rh == h, full[:, h * 2 * cd:(h + 1) * 2 * cd], own)
        half = rows // 2
        o = own[:half] - lam_ref[0] * own[half:]
        o_ref[...] = _rms(o, gain_ref[...]) * (1.0 - lam_init)


def _diff_decode(q_rows, new_kv, cache_t, layer, page_table, gain, lam, lam_init, dec_t, past):
    bsz, rows, w = q_rows.shape
    n_pages = page_table.shape[1]
    page = cache_t.shape[3]
    n_pp = math.gcd(16, n_pages)
    nh = C_HEADS
    cd = w // (2 * nh)

    def page_spec(p):
        return pl.BlockSpec((None, None, 2 * w, page),
                            lambda b, c, pt: (layer, pt[b * n_pages + c * n_pp + p], 0, 0))

    grid_spec = pltpu.PrefetchScalarGridSpec(
        num_scalar_prefetch=1,
        grid=(bsz, n_pages // n_pp),
        in_specs=[pl.BlockSpec(memory_space=pltpu.SMEM),
                  pl.BlockSpec((None, rows, w), lambda b, c, pt: (b, 0, 0)),
                  pl.BlockSpec((None, NEW_PAD, 2 * w), lambda b, c, pt: (b, 0, 0)),
                  pl.BlockSpec((1, 2 * cd), lambda b, c, pt: (0, 0))]
        + [page_spec(p) for p in range(n_pp)],
        out_specs=pl.BlockSpec((None, rows // 2, 2 * cd), lambda b, c, pt: (b, 0, 0)),
        scratch_shapes=[pltpu.VMEM((rows, 1), F32), pltpu.VMEM((rows, 1), F32), pltpu.VMEM((rows, w), F32)],
    )
    return pl.pallas_call(
        functools.partial(_diff_decode_body, n_pp=n_pp, page=page, past=past, nh=nh, cd=cd,
                          lam_init=lam_init, dec_t=dec_t),
        grid_spec=grid_spec,
        out_shape=jax.ShapeDtypeStruct((bsz, rows // 2, 2 * cd), F32),
        compiler_params=_params(("parallel", "arbitrary")),
        name="diff_attn_decode",
    )(page_table.reshape(-1), lam, q_rows, new_kv, gain.reshape(1, 2 * cd), *([cache_t] * n_pp))


def _nsa_decode_body(pt_ref, qh_ref, qd_ref, wc_ref, wn_ref, sn_ref, wbd_ref, cc_ref, gl_ref, gb_ref, *refs,
                     n_pp, page, past, hd, dec_t, n_pages, k_past, pitch):
    pages = refs[:n_pp]
    o_ref = refs[n_pp]
    x_s, ak_s, av_s, mtok_s, ocw_s, m_s, l_s, a_s = refs[n_pp + 1:]
    ph = pl.program_id(1)
    c = pl.program_id(2)
    last = pl.num_programs(2) - 1
    tp = DEC_ROWS
    rows = B_GROUP * tp
    row = lax.broadcasted_iota(jnp.int32, (rows, 1), 0)
    rt = row % tp
    scale2 = hd ** -0.5 * LOG2E
    qpos = past + rt

    def slope2(g):
        return jnp.where(row < tp, 2.0 ** (-(2 * (g * B_GROUP) + 1)) * LOG2E,
                         2.0 ** (-(2 * (g * B_GROUP + 1) + 1)) * LOG2E)

    def new_scores(qd, kn, sl):
        col = lax.broadcasted_iota(jnp.int32, (1, kn.shape[0]), 1)
        d = rt - col
        return jnp.where((d >= 0) & (col < dec_t), _dot_nt(qd, kn) - sl * d.astype(F32), -jnp.inf)

    @pl.when(ph == 0)
    def _():
        for p in range(n_pp):
            r0 = pl.multiple_of((c * n_pp + p) * pitch, 8)
            x_s[pl.ds(r0, 4 * hd), :] = pages[p][...]

    @pl.when((ph == 0) & (c == last))
    def _():
        unroll = 2

        def body(du, accs):
            accs = list(accs)
            for u in range(unroll):
                dp = du * unroll + u
                for cc in range(2):
                    lhs = jnp.concatenate(
                        [jnp.concatenate([x_s[pl.ds(cc * 2 * hd + g * hd + 2 * dp + e, n_pages, stride=pitch), :]
                                          for e in range(2)], axis=1) for g in range(B_KV_HEADS)],
                        axis=0).astype(BF16)
                    accs[cc] = accs[cc] + _dot(lhs, wbd_ref[cc, dp])
            return tuple(accs)

        zero = jnp.zeros((B_KV_HEADS * n_pages, 2 * hd), F32)
        ak_s[...], av_s[...] = lax.fori_loop(0, hd // (2 * unroll), body, (zero, zero))
        pgi = lax.broadcasted_iota(jnp.int32, (1, n_pages), 1)
        lane = lax.broadcasted_iota(jnp.int32, (1, page), 1)
        for g in range(B_KV_HEADS):
            sl = slope2(g)
            kc = (ak_s[g * n_pages:(g + 1) * n_pages, :] + cc_ref[0]).astype(BF16)
            vc = (av_s[g * n_pages:(g + 1) * n_pages, :] + cc_ref[1]).astype(BF16)
            ss = []
            for hf in range(2):
                qh = (qh_ref[g, hf] * scale2).astype(BF16)
                dist = qpos - ((2 * pgi + hf) * NSA_BLOCK + (NSA_BLOCK - 1))
                ss.append(jnp.where(dist >= 0, _dot_nt(qh, kc) - sl * dist.astype(F32), -jnp.inf))
            m = jnp.maximum(jnp.max(ss[0], axis=1, keepdims=True), jnp.max(ss[1], axis=1, keepdims=True))
            m = jnp.where(m == -jnp.inf, 0.0, m)
            es = [jnp.exp2(s - m) for s in ss]
            den = jnp.sum(es[0], axis=1, keepdims=True) + jnp.sum(es[1], axis=1, keepdims=True)
            den = jnp.where(den > 0, den, 1.0)
            ps = [e / den for e in es]
            full = [_dot(p.astype(BF16), vc) for p in ps]
            ocw_s[g, 0] = full[0][:, :hd] + full[1][:, hd:]

            imps = [p[:tp] + p[tp:] for p in ps]
            sc = [jnp.where(pgi == 0, jnp.inf, imps[0]), jnp.where(pgi == n_pages - 1, jnp.inf, imps[1])]
            rank = [jnp.zeros((tp, n_pages), F32), jnp.zeros((tp, n_pages), F32)]
            for pg in range(n_pages):
                for hf in range(2):
                    colv = sc[hf][:, pg:pg + 1]
                    for h2 in range(2):
                        tie = jnp.where(2 * pgi + h2 > 2 * pg + hf, 1.0, 0.0)
                        rank[h2] = rank[h2] + jnp.where(colv > sc[h2], 1.0, jnp.where(colv == sc[h2], tie, 0.0))
            sel = [jnp.where(r < k_past, 1.0, 0.0) for r in rank]
            sel = [jnp.concatenate([s_, s_], axis=0) for s_ in sel]
            for pg in range(n_pages):
                on = jnp.where(lane < NSA_BLOCK, sel[0][:, pg:pg + 1], sel[1][:, pg:pg + 1])
                dist = qpos - (pg * page + lane)
                lo = (pg % n_pp) * page
                mtok_s[g, pg // n_pp, :, lo:lo + page] = jnp.where(on > 0.5, dist.astype(F32) * (-sl), NEG_BIG)

            qd = (qd_ref[g] * scale2).astype(BF16)
            wkt = wc_ref[g * hd:(g + 1) * hd, :].astype(BF16)
            wvt = wc_ref[(B_KV_HEADS + g) * hd:(B_KV_HEADS + g + 1) * hd, :].astype(BF16)
            nwin = wkt.shape[1]
            d1 = (nwin + rt) - lax.broadcasted_iota(jnp.int32, (1, nwin), 1)
            s1 = jnp.where((d1 >= 0) & (d1 < NSA_WINDOW), _dot(qd, wkt) - sl * d1.astype(F32), -jnp.inf)
            wn = wn_ref[...]
            kn = wn[:, g * hd:(g + 1) * hd].astype(BF16)
            vn = wn[:, (B_KV_HEADS + g) * hd:(B_KV_HEADS + g + 1) * hd].astype(BF16)
            s2 = new_scores(qd, kn, sl)
            m = jnp.maximum(jnp.max(s1, axis=1, keepdims=True), jnp.max(s2, axis=1, keepdims=True))
            m = jnp.where(m == -jnp.inf, 0.0, m)
            e1 = jnp.exp2(s1 - m)
            e2 = jnp.exp2(s2 - m)
            den = jnp.sum(e1, axis=1, keepdims=True) + jnp.sum(e2, axis=1, keepdims=True)
            ocw_s[g, 1] = (_dot_nt(e1.astype(BF16), wvt) + _dot(e2.astype(BF16), vn)) / jnp.where(den > 0, den, 1.0)

    @pl.when(ph == 1)
    def _():
        for g in range(B_KV_HEADS):
            sl = slope2(g)
            qd = (qd_ref[g] * scale2).astype(BF16)

            @pl.when(c == 0)
            def _(g=g, sl=sl, qd=qd):
                sn = sn_ref[...]
                kn = sn[:, g * hd:(g + 1) * hd].astype(BF16)
                vn = sn[:, (B_KV_HEADS + g) * hd:(B_KV_HEADS + g + 1) * hd].astype(BF16)
                m_s[g], l_s[g], a_s[g] = _online(new_scores(qd, kn, sl), vn, *_softmax_init(rows, hd))

            kt = jnp.concatenate([pages[p][g * hd:(g + 1) * hd, :].astype(BF16) for p in range(n_pp)], axis=1)
            vt = jnp.concatenate([pages[p][(B_KV_HEADS + g) * hd:(B_KV_HEADS + g + 1) * hd, :].astype(BF16)
                                  for p in range(n_pp)], axis=1)
            s = _dot(qd, kt) + mtok_s[g, c]
            m_s[g], l_s[g], a_s[g] = _online(s, vt, m_s[g], l_s[g], a_s[g], v_t=True)

        @pl.when(c == last)
        def _():
            gates = jax.nn.sigmoid(gl_ref[...] + gb_ref[...])
            for g in range(B_KV_HEADS):
                o_sel = _finish(l_s[g], a_s[g])
                o_cmp = ocw_s[g, 0]
                o_win = ocw_s[g, 1]
                for n in range(B_GROUP):
                    r = slice(n * tp, (n + 1) * tp)
                    base = (g * B_GROUP + n) * 3
                    o_ref[g, n] = (gates[:, base:base + 1] * o_cmp[r] + gates[:, base + 1:base + 2] * o_sel[r]
                                   + gates[:, base + 2:base + 3] * o_win[r])


def _nsa_decode(q_half, q_plain, win_t, win_new, sel_new, cache_t, layer, page_table, w_bdt, cconst2,
                gate_logits, gate_b, dec_t, past, k_past):
    bsz = q_plain.shape[0]
    hd = q_plain.shape[-1]
    rows = q_plain.shape[2]
    n_pages = page_table.shape[1]
    page = cache_t.shape[3]
    nwin = win_t.shape[3]
    n_pp = math.gcd(16, n_pages)
    n_ch = n_pages // n_pp
    pitch = 4 * hd + PAGE_PITCH_PAD

    def page_spec(p):
        return pl.BlockSpec((None, None, 4 * hd, page),
                            lambda b, ph, c, pt: (layer, pt[b * n_pages + c * n_pp + p], ph, 0))

    fix = lambda *shape: pl.BlockSpec(shape, lambda b, ph, c, pt: (0,) * len(shape))
    grid_spec = pltpu.PrefetchScalarGridSpec(
        num_scalar_prefetch=1,
        grid=(bsz, 2, n_ch),
        in_specs=[pl.BlockSpec((None, B_KV_HEADS, 2, rows, 2 * hd), lambda b, ph, c, pt: (b, 0, 0, 0, 0)),
                  pl.BlockSpec((None, B_KV_HEADS, rows, hd), lambda b, ph, c, pt: (b, 0, 0, 0)),
                  pl.BlockSpec((None, None, 4 * hd, nwin), lambda b, ph, c, pt: (layer, b, 0, 0)),
                  pl.BlockSpec((None, NEW_PAD, 4 * hd), lambda b, ph, c, pt: (b, 0, 0)),
                  pl.BlockSpec((None, NEW_PAD, 4 * hd), lambda b, ph, c, pt: (b, 0, 0)),
                  fix(*w_bdt.shape), fix(*cconst2.shape),
                  pl.BlockSpec((None, DEC_ROWS, LANE), lambda b, ph, c, pt: (b, 0, 0)),
                  fix(1, LANE)]
        + [page_spec(p) for p in range(n_pp)],
        out_specs=pl.BlockSpec((None, B_KV_HEADS, B_GROUP, DEC_ROWS, hd), lambda b, ph, c, pt: (b, 0, 0, 0, 0)),
        scratch_shapes=[pltpu.VMEM((n_pages * pitch, page), F32),
                        pltpu.VMEM((B_KV_HEADS * n_pages, 2 * hd), F32),
                        pltpu.VMEM((B_KV_HEADS * n_pages, 2 * hd), F32),
                        pltpu.VMEM((B_KV_HEADS, n_ch, rows, n_pp * page), F32),
                        pltpu.VMEM((B_KV_HEADS, 2, rows, hd), F32),
                        pltpu.VMEM((B_KV_HEADS, rows, 1), F32), pltpu.VMEM((B_KV_HEADS, rows, 1), F32),
                        pltpu.VMEM((B_KV_HEADS, rows, hd), F32)],
    )
    return pl.pallas_call(
        functools.partial(_nsa_decode_body, n_pp=n_pp, page=page, past=past, hd=hd, dec_t=dec_t,
                          n_pages=n_pages, k_past=k_past, pitch=pitch),
        grid_spec=grid_spec,
        out_shape=jax.ShapeDtypeStruct((bsz, B_KV_HEADS, B_GROUP, DEC_ROWS, hd), F32),
        compiler_params=_params(("parallel", "arbitrary", "arbitrary")),
        name="nsa_decode",
    )(page_table.reshape(-1), q_half, q_plain, win_t, win_new, sel_new, w_bdt, cconst2, gate_logits, gate_b,
      *([cache_t] * n_pp))


def _split_points(d_model):
    br = d_model // N_BRANCH
    hd_b = br // B_HEADS
    d_dk = br // (2 * D_HEADS)
    splits = (br, br, br, br, br, 6 * B_KV_HEADS * hd_b, 3 * B_HEADS, br, br, br,
              D_HEADS * d_dk, D_HEADS * d_dk, br, GLA_RANK, br, N_BRANCH * d_model)
    return [0] + [int(p) for p in np.cumsum(splits)]


def _layer_weights(l, d_model, w_in, ffn_w_in, ffn_w_out, w_branch, w_out, nsa_cmp_pe, nsa_cmp_w,
                   gla_gate_w2, gla_gate_b, nsa_gate_b):
    pts = _split_points(d_model)
    wt = jnp.transpose(w_in, (2, 0, 1))[:, l, :]
    seg = lambda i, j: wt[pts[i]:pts[j]]
    padl = lambda a: jnp.pad(a, ((0, LANE - a.shape[0]), (0, 0)))
    wa = seg(0, 4)
    wb = jnp.concatenate([seg(4, 6), padl(seg(6, 7))], axis=0)
    wc = seg(7, 10)
    wd = jnp.concatenate([seg(10, 13), padl(seg(13, 14)), seg(14, 15)], axis=0)
    hd = nsa_cmp_w.shape[-1]
    eye = jnp.eye(B_KV_HEADS, dtype=F32)
    w_bd = jnp.einsum('cjde,gh->cjgdhe', nsa_cmp_w[l], eye).reshape(
        2, NSA_BLOCK, B_KV_HEADS * hd, B_KV_HEADS * hd)
    w_bdt = jnp.einsum('cjde,ab->cdajbe', nsa_cmp_w[l], eye).reshape(
        2, hd // 2, 2 * 2 * NSA_BLOCK, 2 * hd)
    cvec = jnp.einsum('cjd,cjde->ce', nsa_cmp_pe[l], nsa_cmp_w[l])
    cconst = jnp.repeat(cvec, B_KV_HEADS, axis=0).reshape(1, 4 * hd)
    cconst2 = jnp.concatenate([cvec, cvec], axis=1).reshape(2, 1, 2 * hd)
    w2 = jnp.pad(gla_gate_w2[l], ((0, LANE - GLA_RANK), (0, 0)))
    return dict(
        proj=[a.astype(BF16) for a in (wa, wb, wc, wd)],
        w_merge=seg(15, 16).astype(BF16), w_branch=w_branch[l].astype(BF16), w_out=w_out[l].astype(BF16),
        ffn_in=ffn_w_in[l].astype(BF16), ffn_out=ffn_w_out[l].astype(BF16),
        w_bd=w_bd.astype(BF16), w_bdt=w_bdt.astype(BF16), cconst=cconst, cconst2=cconst2,
        gla_w2=w2, gla_b=gla_gate_b[l].reshape(1, -1),
        gate_b=jnp.pad(nsa_gate_b[l].reshape(1, -1), ((0, 0), (0, LANE - 3 * B_HEADS))),
    )


def _mixer_common(x, lw, g2, vec_a, vec_d, bsz, t, state_a, state_d, br, cache_rows_t=False):
    a_dk = br // A_HEADS
    d_dk = br // (2 * D_HEADS)
    d_dv = br // D_HEADS
    ya, yb, yc, yd, *new_t = _proj(x, g2, lw['proj'], br, (bsz, t) if cache_rows_t else None)
    o_a, st_a = _linear_mixer(functools.partial(_hgrn_body, w=br, dk=a_dk), ya, bsz, t, vec_a, state_a,
                              A_HEADS, a_dk, a_dk)
    o_d, st_d = _linear_mixer(functools.partial(_gla_body, wk=D_HEADS * d_dk, wv=br, dk=d_dk, dv=d_dv),
                              yd, bsz, t, vec_d, state_d, D_HEADS, d_dk, d_dv)
    return yb, yc, o_a, st_a, o_d, st_d, new_t


def _feature_major(cache):
    l, n, t = cache.shape[:3]
    return jnp.transpose(cache, (0, 1, 3, 4, 5, 2)).reshape(l, n, -1, t)


def kernel(x_prompt, x_sample, cache_nsa_kv, cache_nsa_win, cache_diff_kv, state_hgrn, state_gla, page_table,
           norm_gains, ffn_w_in, ffn_w_out, w_in, hgrn_lb_logits, hgrn_norm_gain, nsa_cmp_pe, nsa_cmp_w,
           nsa_gate_b, diff_lambda, diff_norm_gain, gla_gate_w2, gla_gate_b, gla_norm_gain, w_branch, w_out):
    bp, tp, d_model = x_prompt.shape
    bs, ts, _ = x_sample.shape
    depth = w_in.shape[0]
    br = d_model // N_BRANCH
    hd = br // B_HEADS
    cd2 = br // C_HEADS
    page = cache_nsa_kv.shape[2]
    n_pages = page_table.shape[1]
    past = n_pages * page
    nwin = cache_nsa_win.shape[2]
    assert past % NSA_BLOCK == 0 and ts <= DEC_ROWS and tp % NSA_BLOCK == 0
    assert past // NSA_BLOCK >= NSA_TOPK and page == 2 * NSA_BLOCK and nwin == NSA_WINDOW
    k_past = NSA_TOPK - 1
    tq = min(256, tp)
    tk = min(128, tq)

    lb_cum = jnp.cumsum(jax.nn.softmax(hgrn_lb_logits.astype(F32), axis=0), axis=0)
    lower = lb_cum - lb_cum[0]
    slopes = 2.0 ** (-np.arange(1, B_HEADS + C_HEADS + 1, dtype=np.float64))
    sl_b = jnp.asarray(slopes[0::2], F32)
    sl_c = jnp.asarray(slopes[1::2], F32)

    nsa_pool_t = _feature_major(cache_nsa_kv)
    diff_pool_t = _feature_major(cache_diff_kv)
    win_pool_t = _feature_major(cache_nsa_win)

    xp = x_prompt.reshape(bp * tp, d_model)
    xs = x_sample.reshape(bs * ts, d_model)
    outs = {k: [] for k in ('kvp', 'kvs', 'winp', 'wins', 'dkp', 'dks', 'hp', 'hs', 'gp', 'gs')}

    for l in range(depth):
        lw = _layer_weights(l, d_model, w_in, ffn_w_in, ffn_w_out, w_branch, w_out, nsa_cmp_pe, nsa_cmp_w,
                            gla_gate_w2, gla_gate_b, nsa_gate_b)
        g = norm_gains[l]
        lb = lower[l].reshape(1, br)
        vec_a = (jnp.log1p(-lb), jnp.log(lb), hgrn_norm_gain[l].reshape(1, br))
        vec_d = (lw['gla_w2'], lw['gla_b'], gla_norm_gain[l].reshape(1, br))
        lv = diff_lambda[l].astype(F32)
        lam_init = 0.8 - 0.6 * math.exp(-0.3 * l)
        lam = (jnp.exp(jnp.sum(lv[0] * lv[1])) - jnp.exp(jnp.sum(lv[2] * lv[3])) + lam_init).reshape(1)

        xp = _ffn(xp, g[0], g[1], lw['ffn_in'][0], lw['ffn_out'][0])
        yb, yc, o_a, st_a, o_d, st_d, (kv_t, win_t, dk_t) = _mixer_common(
            xp, lw, g[2], vec_a, vec_d, bp, tp, None, None, br, cache_rows_t=True)
        outs['hp'].append(st_a)
        outs['gp'].append(st_d)
        outs['kvp'].append(kv_t)
        outs['winp'].append(win_t[:, :, -min(NSA_WINDOW, tp):])
        outs['dkp'].append(dk_t)

        o_c = _diff_prompt(yc, bp, tp, diff_norm_gain[l], lam, sl_c, lam_init, tq, tk)
        cmp_kv = _compress_prompt(yb, bp, tp, lw['w_bd'], lw['cconst'], hd)
        o_b = _nsa_prompt(yb, cmp_kv, bp, tp, lw['gate_b'], sl_b, tq, tk)

        xp = _merge(xp, g[2], g[3], (o_a, o_b, o_c, o_d), lw['w_merge'], lw['w_branch'], lw['w_out'])
        xp = _ffn(xp, g[4], g[5], lw['ffn_in'][1], lw['ffn_out'][1])

        xs = _ffn(xs, g[0], g[1], lw['ffn_in'][0], lw['ffn_out'][0])
        yb, yc, o_a, st_a, o_d, st_d, _ = _mixer_common(xs, lw, g[2], vec_a, vec_d, bs, ts,
                                                        state_hgrn[l], state_gla[l], br)
        outs['hs'].append(st_a)
        outs['gs'].append(st_d)
        new_kv = yb[:, br:br + 6 * B_KV_HEADS * hd].reshape(bs, ts, 6, B_KV_HEADS * hd)
        outs['kvs'].append(new_kv[:, :, :4].reshape(bs, ts, 4, B_KV_HEADS, hd))
        new_win = new_kv[:, :, 4:].reshape(bs, ts, 2 * B_KV_HEADS * hd)
        outs['wins'].append(jnp.concatenate(
            [cache_nsa_win[l][:, ts:], new_win.reshape(bs, ts, 2, B_KV_HEADS, hd)], axis=1))
        outs['dks'].append(yc[:, br:3 * br].reshape(bs, ts, 2, C_HEADS, cd2))
        pad_new = lambda a: jnp.pad(a, ((0, 0), (0, NEW_PAD - ts), (0, 0)))

        cq = yc[:, 0:br].reshape(bs, ts, C_HEADS, 2, cd2 // 2)
        sel = (jnp.arange(C_HEADS)[:, None, None, None] == jnp.arange(C_HEADS)[None, None, :, None]) & \
              (jnp.arange(2)[None, :, None, None] == jnp.arange(2)[None, None, None, :])
        q_rows = jnp.einsum('bthmd,hmgn->bmthgnd', cq, sel.astype(F32)).reshape(bs, 2 * ts * C_HEADS, br)
        new_c = pad_new(yc[:, br:3 * br].reshape(bs, ts, 2 * br))
        o_c = _diff_decode(q_rows, new_c, diff_pool_t, l, page_table, diff_norm_gain[l], lam, lam_init, ts, past)
        o_c = o_c.reshape(bs * ts, br)

        bq = yb[:, 0:br].reshape(bs, ts, B_KV_HEADS, B_GROUP, hd).transpose(0, 2, 3, 1, 4)
        bq = jnp.pad(bq, ((0, 0), (0, 0), (0, 0), (0, DEC_ROWS - ts), (0, 0)))
        q_plain = bq.reshape(bs, B_KV_HEADS, B_GROUP * DEC_ROWS, hd)
        zero = jnp.zeros_like(q_plain)
        q_half = jnp.stack([jnp.concatenate([q_plain, zero], axis=-1),
                            jnp.concatenate([zero, q_plain], axis=-1)], axis=2)
        gl = yb[:, br + 6 * B_KV_HEADS * hd:].reshape(bs, ts, LANE)
        gl = jnp.pad(gl, ((0, 0), (0, DEC_ROWS - ts), (0, 0)))
        o_b = _nsa_decode(q_half, q_plain, win_pool_t, pad_new(new_win),
                          pad_new(new_kv[:, :, 2:4].reshape(bs, ts, 4 * hd)), nsa_pool_t, l, page_table,
                          lw['w_bdt'], lw['cconst2'], gl, lw['gate_b'], ts, past, k_past)
        o_b = o_b[:, :, :, :ts].transpose(0, 3, 1, 2, 4).reshape(bs * ts, br)

        xs = _merge(xs, g[2], g[3], (o_a, o_b, o_c, o_d), lw['w_merge'], lw['w_branch'], lw['w_out'])
        xs = _ffn(xs, g[4], g[5], lw['ffn_in'][1], lw['ffn_out'][1])

    st = lambda k: jnp.stack(outs[k])

    def token_major(k, a, b):
        x = st(k)
        return x.reshape(x.shape[:2] + (a, b, -1, x.shape[-1])).transpose(0, 1, 5, 2, 3, 4)

    return (xp.reshape(bp, tp, d_model), xs.reshape(bs, ts, d_model),
            token_major('kvp', 4, B_KV_HEADS), st('kvs'), token_major('winp', 2, B_KV_HEADS), st('wins'),
            token_major('dkp', 2, C_HEADS), st('dks'), st('hp'), st('hs'), st('gp'), st('gs'))
```

```python
import functools
import math

import numpy as np
import jax
import jax.numpy as jnp
from jax import lax
from jax.experimental import pallas as pl
from jax.experimental.pallas import tpu as pltpu

F32 = jnp.float32
BF16 = jnp.bfloat16
HI = lax.Precision.HIGHEST

N_BRANCH = 4
A_HEADS = 4
B_HEADS = 4
B_KV_HEADS = 2
B_GROUP = 2
C_HEADS = 4
D_HEADS = 4
NSA_BLOCK = 64
NSA_TOPK = 16
NSA_WINDOW = 512
GLA_RANK = 16
GLA_TAU = 16.0
NORM_EPS = 1e-6
LANE = 128
LIN_GROUP = 16
NEW_PAD = 16
DEC_ROWS = 8
ONES_PAD = 16
PAGE_PITCH_PAD = 8
PAGES_PER_STEP = 32
VMEM_LIMIT = 56 * 1024 * 1024
LOG2E = 1.4426950408889634
NEG_BIG = -1e30

NT_DIMS = (((1,), (1,)), ((), ()))
TN_DIMS = (((0,), (0,)), ((), ()))


def _params(semantics):
    return pltpu.CompilerParams(dimension_semantics=semantics, vmem_limit_bytes=VMEM_LIMIT)


def _rms(x, g):
    return x * lax.rsqrt(jnp.mean(x * x, axis=-1, keepdims=True) + NORM_EPS) * g


def _dot(a, b):
    return jnp.dot(a, b, preferred_element_type=F32)


def _dot_nt(a, b):
    return lax.dot_general(a, b, NT_DIMS, preferred_element_type=F32)


def _row_tile(n, cap):
    t = min(n, cap)
    while n % t or t % 8:
        t -= 1
    return t


def _ffn_body(x_ref, gpre_ref, gpost_ref, wg_ref, wu_ref, wo_ref, o_ref, xn_ref, acc_ref):
    f = pl.program_id(1)

    @pl.when(f == 0)
    def _():
        xn_ref[...] = _rms(x_ref[...], gpre_ref[...]).astype(BF16)
        acc_ref[...] = jnp.zeros_like(acc_ref)

    xn = xn_ref[...]
    gate = _dot(xn, wg_ref[...])
    up = _dot(xn, wu_ref[...])
    act = (gate * jax.nn.sigmoid(gate) * up).astype(BF16)
    acc_ref[...] += _dot(act, wo_ref[...])

    @pl.when(f == pl.num_programs(1) - 1)
    def _():
        o_ref[...] = x_ref[...] + 0.5 * _rms(acc_ref[...], gpost_ref[...])


def _ffn(x, g_pre, g_post, w_in, w_out):
    n, d = x.shape
    dff = w_out.shape[0]
    tf = 256
    nf = dff // tf
    tm = _row_tile(n, 1024)
    return pl.pallas_call(
        _ffn_body,
        grid=(n // tm, nf),
        in_specs=[
            pl.BlockSpec((tm, d), lambda i, f: (i, 0)),
            pl.BlockSpec((1, d), lambda i, f: (0, 0)),
            pl.BlockSpec((1, d), lambda i, f: (0, 0)),
            pl.BlockSpec((d, tf), lambda i, f: (0, f)),
            pl.BlockSpec((d, tf), lambda i, f: (0, nf + f)),
            pl.BlockSpec((tf, d), lambda i, f: (f, 0)),
        ],
        out_specs=pl.BlockSpec((tm, d), lambda i, f: (i, 0)),
        out_shape=jax.ShapeDtypeStruct((n, d), F32),
        scratch_shapes=[pltpu.VMEM((tm, d), BF16), pltpu.VMEM((tm, d), F32)],
        compiler_params=_params(("parallel", "arbitrary")),
        name="ffn",
    )(x, g_pre.reshape(1, d), g_post.reshape(1, d), w_in, w_in, w_out)


def _proj_body(x_ref, g_ref, wa, wb, wc, wd, oa, ob, oc, od, *cache_t, br):
    xn = _rms(x_ref[...], g_ref[...]).astype(BF16)
    ys = [_dot_nt(xn, w[...]) for w in (wa, wb, wc, wd)]
    for y, o in zip(ys, (oa, ob, oc, od)):
        o[...] = y
    if cache_t:
        kv_t, win_t, dk_t = cache_t
        kv_t[...] = ys[1][:, br:3 * br].T
        win_t[...] = ys[1][:, 3 * br:4 * br].T
        dk_t[...] = ys[2][:, br:3 * br].T


def _proj(x, g, ws, br, seqs=None):
    n, d = x.shape
    tm = _row_tile(n, 512)
    out_specs = [pl.BlockSpec((tm, w.shape[0]), lambda i: (i, 0)) for w in ws]
    out_shape = [jax.ShapeDtypeStruct((n, w.shape[0]), F32) for w in ws]
    if seqs is not None:
        bsz, t = seqs
        per = t // tm
        for rows in (2 * br, br, 2 * br):
            out_specs.append(pl.BlockSpec((None, rows, tm), lambda i: (i // per, 0, i % per)))
            out_shape.append(jax.ShapeDtypeStruct((bsz, rows, t), F32))
    return pl.pallas_call(
        functools.partial(_proj_body, br=br),
        grid=(n // tm,),
        in_specs=[pl.BlockSpec((tm, d), lambda i: (i, 0)), pl.BlockSpec((1, d), lambda i: (0, 0))]
        + [pl.BlockSpec(w.shape, lambda i: (0, 0)) for w in ws],
        out_specs=out_specs,
        out_shape=out_shape,
        compiler_params=_params(("parallel",)),
        name="mixer_in_proj",
    )(x, g.reshape(1, d), *ws)


def _merge_body(x_ref, g2_ref, g3_ref, oa, ob, oc, od, wm_ref, wb_ref, wo_ref, out_ref):
    x = x_ref[...]
    d = x.shape[1]
    h = _rms(x, g2_ref[...]).astype(BF16)
    s = None
    for n, br in enumerate((oa, ob, oc, od)):
        gate = jax.nn.sigmoid(_dot_nt(h, wm_ref[n * d:(n + 1) * d, :]))
        term = gate * _dot(br[...].astype(BF16), wb_ref[n])
        s = term if s is None else s + term
    y = _dot(s.astype(BF16), wo_ref[...])
    out_ref[...] = x + _rms(y, g3_ref[...])


def _merge(x, g2, g3, branches, w_merge, w_branch, w_out):
    n, d = x.shape
    br = branches[0].shape[1]
    tm = _row_tile(n, 256)
    row = lambda i: (i, 0)
    fix2 = lambda i: (0, 0)
    return pl.pallas_call(
        _merge_body,
        grid=(n // tm,),
        in_specs=[pl.BlockSpec((tm, d), row), pl.BlockSpec((1, d), fix2), pl.BlockSpec((1, d), fix2)]
        + [pl.BlockSpec((tm, br), row)] * N_BRANCH
        + [pl.BlockSpec(w_merge.shape, fix2), pl.BlockSpec(w_branch.shape, lambda i: (0, 0, 0)),
           pl.BlockSpec(w_out.shape, fix2)],
        out_specs=pl.BlockSpec((tm, d), row),
        out_shape=jax.ShapeDtypeStruct((n, d), F32),
        compiler_params=_params(("parallel",)),
        name="merge",
    )(x, g2.reshape(1, d), g3.reshape(1, d), *branches, w_merge, w_branch, w_out)


def _softplus_neg_abs(z):
    return jnp.log(1.0 + jnp.exp(-jnp.abs(z)))


def _log_sigmoid(z):
    return jnp.minimum(z, 0.0) - _softplus_neg_abs(z)


def _dot_exact_lhs(a, x):
    hi = x.astype(BF16)
    lo = (x - hi.astype(F32)).astype(BF16)
    return _dot(a, hi) + _dot(a, lo)


def _lin_core(q, k, v, lf, gg, gain, s0_ref, o_ref, st_ref, qt_s, kt_s, vt_s, dec_s, oi_s, st_s,
              *, dk, dv, t_valid, n_seq):
    rows, wk = q.shape
    tt = rows // n_seq
    wv = v.shape[1]
    grp = LIN_GROUP
    t_idx = pl.program_id(1)

    @pl.when(t_idx == 0)
    def _():
        st_s[...] = s0_ref[...]

    if t_valid is not None:
        tok = t_idx * tt + lax.broadcasted_iota(jnp.int32, (rows, 1), 0) % tt
        lf = jnp.where(tok < t_valid, lf, 0.0)

    r = lax.broadcasted_iota(jnp.int32, (tt, tt), 0)
    c = lax.broadcasted_iota(jnp.int32, (tt, tt), 1)
    same = (r // grp) == (c // grp)
    tri = jnp.where(same & (c <= r), 1.0, 0.0).astype(BF16)
    ones_g = jnp.where(same, 1.0, 0.0).astype(BF16)
    lfs = [lf[si * tt:(si + 1) * tt] for si in range(n_seq)]
    b = jnp.concatenate([_dot_exact_lhs(tri, x) for x in lfs], axis=0)
    bl = jnp.concatenate([_dot_exact_lhs(ones_g, x) for x in lfs], axis=0)

    qt_s[...] = (q * jnp.exp(b)).astype(BF16)
    kt_s[...] = (k * jnp.exp(bl - b)).astype(BF16)
    vt_s[...] = v.astype(BF16)
    dec_s[...] = jnp.exp(bl)

    hk = lax.broadcasted_iota(jnp.int32, (wk, wv), 0) // dk
    hv = lax.broadcasted_iota(jnp.int32, (wk, wv), 1) // dv
    ones_hd = jnp.where(hk == hv, 1.0, 0.0).astype(BF16)

    rowm = lax.broadcasted_iota(jnp.int32, (rows, 1), 0) % grp
    od = jnp.zeros((rows, wv), F32)
    for d in range(grp):
        ks = k if d == 0 else pltpu.roll(k, d, 0)
        bs = b if d == 0 else pltpu.roll(b, d, 0)
        vs = v if d == 0 else pltpu.roll(v, d, 0)
        e = jnp.exp(jnp.minimum(b - bs, 0.0))
        z = jnp.where(rowm >= d, q * ks * e, 0.0).astype(BF16)
        od = od + _dot(z, ones_hd) * vs

    mv = lax.broadcasted_iota(jnp.int32, (wv, wk), 0) // dv
    mk = lax.broadcasted_iota(jnp.int32, (wv, wk), 1) // dk
    mbd = jnp.where(mv == mk, 1.0, 0.0).astype(F32)

    def step(i, carry):
        for si in range(n_seq):
            r0 = pl.multiple_of(si * tt + i * grp, grp)
            qg = qt_s[pl.ds(r0, grp), :]
            kg = kt_s[pl.ds(r0, grp), :]
            vg = vt_s[pl.ds(r0, grp), :]
            s = st_s[si]
            oi_s[pl.ds(r0, grp), :] = _dot_nt(qg, s.astype(BF16))
            upd = lax.dot_general(vg, kg, TN_DIMS, preferred_element_type=F32)
            st_s[si] = dec_s[pl.ds(r0, 1), :] * s + mbd * upd
        return carry

    lax.fori_loop(0, tt // grp, step, 0)

    o = oi_s[...] + od
    pv = lax.broadcasted_iota(jnp.int32, (wv, wv), 0) // dv
    pw = lax.broadcasted_iota(jnp.int32, (wv, wv), 1) // dv
    avg = jnp.where(pv == pw, 1.0 / dv, 0.0).astype(BF16)
    o2 = o * o
    o2_hi = o2.astype(BF16)
    ms = _dot(o2_hi, avg) + _dot((o2 - o2_hi.astype(F32)).astype(BF16), avg)
    y = o * lax.rsqrt(ms + NORM_EPS) * gain * (gg * jax.nn.sigmoid(gg))
    o_ref[...] = y.reshape(o_ref.shape)

    @pl.when(t_idx == pl.num_programs(1) - 1)
    def _():
        st_ref[...] = st_s[...]


def _hgrn_body(y_ref, la_ref, lc_ref, gain_ref, s0_ref, o_ref, st_ref, *scratch, w, dk, t_valid):
    n_seq = y_ref.shape[0]
    y = y_ref[...].reshape(-1, y_ref.shape[2])
    q = y[:, 0:w] * (dk ** -0.5)
    z = y[:, w:2 * w]
    v = y[:, 2 * w:3 * w]
    gg = y[:, 3 * w:4 * w]
    ls = _log_sigmoid(z)
    a = la_ref[...] + ls
    cc = lc_ref[...]
    lf = jnp.maximum(a, cc) + _softplus_neg_abs(a - cc)
    k = jnp.exp(la_ref[...] + (ls - z))
    _lin_core(q, k, v, lf, gg, gain_ref[...], s0_ref, o_ref, st_ref, *scratch, dk=dk, dv=dk, t_valid=t_valid,
              n_seq=n_seq)


def _gla_body(y_ref, w2_ref, b2_ref, gain_ref, s0_ref, o_ref, st_ref, *scratch, wk, wv, dk, dv, t_valid):
    n_seq = y_ref.shape[0]
    y = y_ref[...].reshape(-1, y_ref.shape[2])
    q = y[:, 0:wk] * (dk ** -0.5)
    k = y[:, wk:2 * wk]
    v = y[:, 2 * wk:2 * wk + wv]
    lr = y[:, 2 * wk + wv:2 * wk + wv + LANE]
    gg = y[:, 2 * wk + wv + LANE:2 * wk + 2 * wv + LANE]
    u = jnp.dot(lr, w2_ref[...], precision=HI, preferred_element_type=F32) + b2_ref[...]
    lf = _log_sigmoid(u) * (1.0 / GLA_TAU)
    _lin_core(q, k, v, lf, gg, gain_ref[...], s0_ref, o_ref, st_ref, *scratch, dk=dk, dv=dv, t_valid=t_valid,
              n_seq=n_seq)


def _lin_call(body, y, n_seq, tt, vecs, s0):
    bsz, t, cw = y.shape
    wv, wk = s0.shape[1:]
    rows = n_seq * tt
    return pl.pallas_call(
        body,
        grid=(bsz // n_seq, t // tt),
        in_specs=[pl.BlockSpec((n_seq, tt, cw), lambda b, i: (b, i, 0))]
        + [pl.BlockSpec(a.shape, lambda b, i: (0, 0)) for a in vecs]
        + [pl.BlockSpec((n_seq, wv, wk), lambda b, i: (b, 0, 0))],
        out_specs=[pl.BlockSpec((n_seq, tt, wv), lambda b, i: (b, i, 0)),
                   pl.BlockSpec((n_seq, wv, wk), lambda b, i: (b, 0, 0))],
        out_shape=[jax.ShapeDtypeStruct((bsz, t, wv), F32), jax.ShapeDtypeStruct((bsz, wv, wk), F32)],
        scratch_shapes=[pltpu.VMEM((rows, wk), BF16), pltpu.VMEM((rows, wk), BF16), pltpu.VMEM((rows, wv), BF16),
                        pltpu.VMEM((rows, wk), F32), pltpu.VMEM((rows, wv), F32), pltpu.VMEM((n_seq, wv, wk), F32)],
        compiler_params=_params(("parallel", "arbitrary")),
        name="gated_linear",
    )(y, *vecs, s0)


def _state_to_bd(state):
    bsz, nh, dk, dv = state.shape
    eye = jnp.eye(nh, dtype=state.dtype)
    return jnp.einsum('bhkv,hg->bhvgk', state, eye).reshape(bsz, nh * dv, nh * dk)


def _bd_to_state(st, nh):
    bsz, wv, wk = st.shape
    dv, dk = wv // nh, wk // nh
    blocks = st.reshape(bsz, nh, dv, nh, dk)
    idx = jnp.arange(nh)
    return blocks[:, idx, :, idx, :].transpose(1, 0, 3, 2)


def _linear_mixer(body, y, bsz, t, vecs, state0, nh, dk, dv):
    assert dv & (dv - 1) == 0
    tp = -(-t // LIN_GROUP) * LIN_GROUP
    y3 = y.reshape(bsz, t, -1)
    if tp != t:
        y3 = jnp.pad(y3, ((0, 0), (0, tp - t), (0, 0)))
    tt = min(tp, LANE)
    n_seq = math.gcd(bsz, min(8, max(1, 512 // tt)))
    wk, wv = nh * dk, nh * dv
    s0 = jnp.zeros((bsz, wv, wk), F32) if state0 is None else _state_to_bd(state0)
    body = functools.partial(body, t_valid=None if tp == t else t)
    o, st = _lin_call(body, y3, n_seq, tt, vecs, s0)
    return o[:, :t].reshape(bsz * t, wv), _bd_to_state(st, nh)


def _col_update(idx, tiles, m_s, acc_s):
    m_old = m_s[idx]
    m_new = m_old
    for s, _, shift in tiles:
        smax = jnp.max(s, axis=0, keepdims=True)
        m_new = jnp.maximum(m_new, smax if shift is None else smax + shift)
    m_safe = jnp.where(m_new == -jnp.inf, 0.0, m_new)
    pv = None
    for s, vt, shift in tiles:
        p = jnp.exp2(s + ((-m_safe) if shift is None else (shift - m_safe)))
        d = _dot(vt, p.astype(BF16))
        pv = d if pv is None else pv + d
    acc_s[idx] = jnp.exp2(m_old - m_safe) * acc_s[idx] + pv
    m_s[idx] = m_new


def _col_finish(acc, dv):
    den = acc[dv:dv + 1, :]
    return acc[:dv, :] / jnp.where(den > 0, den, 1.0)


def _col_reset(m_s, acc_s):
    m_s[...] = jnp.full(m_s.shape, -jnp.inf, F32)
    acc_s[...] = jnp.zeros(acc_s.shape, F32)


def _online(s, v, m, l, a, v_t=False):
    m_new = jnp.maximum(m, jnp.max(s, axis=1, keepdims=True))
    m_safe = jnp.where(m_new == -jnp.inf, 0.0, m_new)
    p = jnp.exp2(s - m_safe)
    alpha = jnp.exp2(m - m_safe)
    l = alpha * l + jnp.sum(p, axis=1, keepdims=True)
    pv = _dot_nt(p.astype(BF16), v) if v_t else _dot(p.astype(BF16), v)
    return m_new, l, alpha * a + pv


def _finish(l, a):
    return a / jnp.where(l > 0, l, 1.0)


def _softmax_init(rows, width):
    return (jnp.full((rows, 1), -jnp.inf, F32), jnp.zeros((rows, 1), F32), jnp.zeros((rows, width), F32))


def _stage_keys_values(k_ref, v_ref, kb_s, vt_s, *, t, tk, w, k_lanes):
    kw = kb_s.shape[2]
    n_heads = vt_s.shape[0]
    ones = jnp.where(lax.broadcasted_iota(jnp.int32, (ONES_PAD, tk), 0) == 0, 1.0, 0.0).astype(BF16)

    def body(jt, carry):
        rows = pl.ds(pl.multiple_of(jt * tk, tk), tk)
        kt = k_ref[rows, :]
        for c, lo in enumerate(k_lanes):
            kb_s[c, rows, :] = kt[:, lo:lo + kw].astype(BF16)
        vt = v_ref[rows, :].T
        for hh in range(n_heads):
            vt_s[hh, jt, 0:w, :] = vt[hh * w:(hh + 1) * w, :].astype(BF16)
            vt_s[hh, jt, w:w + ONES_PAD, :] = ones
        return carry

    lax.fori_loop(0, t // tk, body, 0)


def _diff_prompt_body(sl_ref, lam_ref, q_ref, k_ref, v_ref, gain_ref, o_ref, kb_s, vt_s, boff_s, m_s, acc_s, s_s,
                      *, t, tq, tk, cd, lam_init):
    hp = pl.program_id(1)
    i = pl.program_id(2)
    w = 2 * cd
    pair = LANE // w

    lanes = [hh * w + mi * cd for hh in range(pair) for mi in range(2)]

    @pl.when(i == 0)
    def _():
        _stage_keys_values(k_ref, v_ref, kb_s, vt_s, t=t, tk=tk, w=w, k_lanes=lanes)

    qt = (q_ref[...] * (cd ** -0.5 * LOG2E)).T
    slope2 = [sl_ref[hp * pair + hh] * LOG2E for hh in range(pair)]
    chains = [(ci // 2, qt[lo:lo + cd, :].astype(BF16)) for ci, lo in enumerate(lanes)]
    rel = lax.broadcasted_iota(jnp.int32, (tk, tq), 1) - lax.broadcasted_iota(jnp.int32, (tk, tq), 0)
    rel_f = rel.astype(F32)
    for hh in range(pair):
        boff_s[hh] = rel_f * (-slope2[hh])
    _col_reset(m_s, acc_s)
    ratio = tq // tk

    def score_span(jj):
        for ci, (_, qc) in enumerate(chains):
            for r in range(ratio):
                s_s[jj % 2, ci, r] = _dot(kb_s[ci, pl.ds(pl.multiple_of((jj * ratio + r) * tk, tk), tk), :], qc)

    score_span(0)

    def off_group(jj, carry):
        for ci, (hh, _) in enumerate(chains):
            boff = boff_s[hh]
            _col_update(ci, [(s_s[jj % 2, ci, r] + boff, vt_s[hh, jj * ratio + r],
                              ((i - jj) * tq - r * tk).astype(F32) * (-slope2[hh])) for r in range(ratio)],
                        m_s, acc_s)
        score_span(jj + 1)
        return carry

    lax.fori_loop(0, i, off_group, 0)
    dists = [rel - r * tk for r in range(ratio)]
    for ci, (hh, _) in enumerate(chains):
        _col_update(ci, [(s_s[i % 2, ci, r] + jnp.where(dists[r] >= 0, dists[r].astype(F32) * (-slope2[hh]), -jnp.inf),
                          vt_s[hh, i * ratio + r], None) for r in range(ratio)], m_s, acc_s)

    outs = []
    for hh in range(pair):
        o = _col_finish(acc_s[2 * hh], w) - lam_ref[0] * _col_finish(acc_s[2 * hh + 1], w)
        ms = jnp.mean(o * o, axis=0, keepdims=True)
        outs.append(o * lax.rsqrt(ms + NORM_EPS))
    o_ref[...] = (jnp.concatenate(outs, axis=0) * gain_ref[...] * (1.0 - lam_init)).T


def _diff_prompt(yc, bsz, t, gain, lam, slopes, lam_init, tq, tk):
    w = gain.shape[0]
    nh = yc.shape[1] // (3 * w)
    pair = LANE // w
    nhp = nh // pair
    nq = t // tq
    dva = w + ONES_PAD
    smem = pl.BlockSpec(memory_space=pltpu.SMEM)
    return pl.pallas_call(
        functools.partial(_diff_prompt_body, t=t, tq=tq, tk=tk, cd=w // 2, lam_init=lam_init),
        grid=(bsz, nhp, nq),
        in_specs=[smem, smem,
                  pl.BlockSpec((tq, LANE), lambda b, hp, i: (b * nq + i, hp)),
                  pl.BlockSpec((t, LANE), lambda b, hp, i: (b, nhp + hp)),
                  pl.BlockSpec((t, LANE), lambda b, hp, i: (b, 2 * nhp + hp)),
                  pl.BlockSpec((LANE, 1), lambda b, hp, i: (0, 0))],
        out_specs=pl.BlockSpec((tq, LANE), lambda b, hp, i: (b * nq + i, hp)),
        out_shape=jax.ShapeDtypeStruct((bsz * t, nh * w), F32),
        scratch_shapes=[pltpu.VMEM((2 * pair, t, w // 2), BF16), pltpu.VMEM((pair, t // tk, dva, tk), BF16),
                        pltpu.VMEM((pair, tk, tq), F32), pltpu.VMEM((2 * pair, 1, tq), F32),
                        pltpu.VMEM((2 * pair, dva, tq), F32), pltpu.VMEM((2, 2 * pair, tq // tk, tk, tq), F32)],
        compiler_params=_params(("parallel", "parallel", "arbitrary")),
        name="diff_attn_prompt",
    )(slopes, lam, yc, yc, yc, jnp.tile(gain, pair).reshape(LANE, 1))


def _compress(xk_ref, xv_ref, w_ref, nb):
    def body(j, acc):
        ak, av = acc
        xk = xk_ref[pl.ds(j, nb, stride=NSA_BLOCK), :].astype(BF16)
        xv = xv_ref[pl.ds(j, nb, stride=NSA_BLOCK), :].astype(BF16)
        return ak + _dot(xk, w_ref[0, j]), av + _dot(xv, w_ref[1, j])

    zero = jnp.zeros((nb, xk_ref.shape[1]), F32)
    ak, av = lax.fori_loop(0, NSA_BLOCK, body, (zero, zero))
    return jnp.concatenate([ak, av], axis=1)


def _compress_body(xk_ref, xv_ref, w_ref, c_ref, o_ref, *, nb, hd):
    acc = _compress(xk_ref, xv_ref, w_ref, nb) + c_ref[...]
    for p in range(4):
        o_ref[p] = acc[:, p * hd:(p + 1) * hd]


def _compress_prompt(yb, bsz, t, w_bd, cconst, hd):
    nb = t // NSA_BLOCK
    cw = 4 * hd
    hw = cw // 2
    return pl.pallas_call(
        functools.partial(_compress_body, nb=nb, hd=hd),
        grid=(bsz,),
        in_specs=[pl.BlockSpec((t, hw), lambda b: (b, 2)),
                  pl.BlockSpec((t, hw), lambda b: (b, 3)),
                  pl.BlockSpec(w_bd.shape, lambda b: (0, 0, 0, 0)),
                  pl.BlockSpec((1, cw), lambda b: (0, 0))],
        out_specs=pl.BlockSpec((None, 4, nb, hd), lambda b: (b, 0, 0, 0)),
        out_shape=jax.ShapeDtypeStruct((bsz, 4, nb, hd), F32),
        compiler_params=_params(("parallel",)),
        name="nsa_compress_prompt",
    )(yb, yb, w_bd, cconst)


def _rank_rows(score, tq):
    nb = score.shape[0]
    sub = 8
    slabs = [score[v * sub:(v + 1) * sub, :] for v in range(nb // sub)]
    ranks = [jnp.zeros((sub, tq), F32) for _ in slabs]
    sub_i = lax.broadcasted_iota(jnp.int32, (sub, 1), 0)
    for ib in range(nb):
        row = score[ib:ib + 1, :]
        for v, slab in enumerate(slabs):
            if ib < v * sub:
                ahead = jnp.where(row >= slab, 1.0, 0.0)
            elif ib >= (v + 1) * sub:
                ahead = jnp.where(row > slab, 1.0, 0.0)
            else:
                ahead = jnp.where(sub_i > ib - v * sub, jnp.where(row >= slab, 1.0, 0.0),
                                  jnp.where(row > slab, 1.0, 0.0))
            ranks[v] = ranks[v] + ahead
    return jnp.concatenate(ranks, axis=0)


def _nsa_prompt_body(sl_ref, q_ref, cmp_ref, ks_ref, vs_ref, kw_ref, vw_ref, gl_ref, gb_ref, o_ref,
                     ksb_s, kwb_s, vst_s, vwt_s, boff_s, m_s, acc_s, s_s, *, t, tq, tk, nb, hd, k_sel):
    i = pl.program_id(1)
    heads = [(g, n) for g in range(B_KV_HEADS) for n in range(B_GROUP)]
    nh = len(heads)

    @pl.when(i == 0)
    def _():
        g_lanes = [g * hd for g in range(B_KV_HEADS)]
        _stage_keys_values(ks_ref, vs_ref, ksb_s, vst_s, t=t, tk=tk, w=hd, k_lanes=g_lanes)
        _stage_keys_values(kw_ref, vw_ref, kwb_s, vwt_s, t=t, tk=tk, w=hd, k_lanes=g_lanes)

    qt = (q_ref[...] * (hd ** -0.5 * LOG2E)).T
    qh = [qt[hi * hd:(hi + 1) * hd, :].astype(BF16) for hi in range(nh)]
    sl2 = [sl_ref[hi] * LOG2E for hi in range(nh)]
    qpos = i * tq + lax.broadcasted_iota(jnp.int32, (1, tq), 1)

    blk = lax.broadcasted_iota(jnp.int32, (nb, 1), 0)
    distc = qpos - (blk * NSA_BLOCK + (NSA_BLOCK - 1))
    distc_f = distc.astype(F32)
    cur = qpos // NSA_BLOCK
    forced = (blk == 0) | (blk == cur) | (blk == cur - 1)
    o_cmp = []
    selneg = []
    for g in range(B_KV_HEADS):
        kc = cmp_ref[g].astype(BF16)
        vc = cmp_ref[B_KV_HEADS + g].astype(BF16)
        imp = None
        for n in range(B_GROUP):
            hi = g * B_GROUP + n
            s = jnp.where(distc >= 0, _dot(kc, qh[hi]) - sl2[hi] * distc_f, -jnp.inf)
            m = jnp.max(s, axis=0, keepdims=True)
            e = jnp.exp2(s - jnp.where(m == -jnp.inf, 0.0, m))
            den = jnp.sum(e, axis=0, keepdims=True)
            p = e / jnp.where(den > 0, den, 1.0)
            o_cmp.append(lax.dot_general(vc, p.astype(BF16), TN_DIMS, preferred_element_type=F32))
            imp = p if imp is None else imp + p
        score = jnp.where(blk > cur, -jnp.inf, jnp.where(forced, jnp.inf, imp))
        selneg.append(jnp.where(_rank_rows(score, tq) < k_sel, 0.0, NEG_BIG).astype(BF16))

    rel = lax.broadcasted_iota(jnp.int32, (tk, tq), 1) - lax.broadcasted_iota(jnp.int32, (tk, tq), 0)
    rel_f = rel.astype(F32)
    for hi in range(nh):
        boff_s[hi] = rel_f * (-sl2[hi])
    ratio = tq // tk
    erow = lax.broadcasted_iota(jnp.int32, (tk, nb), 0)
    eblk = lax.broadcasted_iota(jnp.int32, (tk, nb), 1)

    def block_masks(j):
        expand = jnp.where((erow + j * tk) // NSA_BLOCK == eblk, 1.0, 0.0).astype(BF16)
        return [_dot(expand, sn) for sn in selneg]

    _col_reset(m_s, acc_s)

    def key_tiles(kb_s, jj):
        return [[kb_s[g, pl.ds(pl.multiple_of((jj * ratio + r) * tk, tk), tk), :] for r in range(ratio)]
                for g in range(B_KV_HEADS)]

    def score_span(jj):
        kts = key_tiles(ksb_s, jj)
        mbs = [block_masks(jj * ratio + r) for r in range(ratio)]
        for hi, (g, _) in enumerate(heads):
            for r in range(ratio):
                s_s[jj % 2, hi, r] = _dot(kts[g][r], qh[hi]) + mbs[r][g]

    score_span(0)

    def sel_off(jj, carry):
        for hi, (g, _) in enumerate(heads):
            boff = boff_s[hi]
            _col_update(hi, [(s_s[jj % 2, hi, r] + boff, vst_s[g, jj * ratio + r],
                              ((i - jj) * tq - r * tk).astype(F32) * (-sl2[hi])) for r in range(ratio)], m_s, acc_s)
        score_span(jj + 1)
        return carry

    lax.fori_loop(0, i, sel_off, 0)
    dists = [rel - r * tk for r in range(ratio)]
    for hi, (g, _) in enumerate(heads):
        _col_update(hi, [(s_s[i % 2, hi, r] + jnp.where(dists[r] >= 0, dists[r].astype(F32) * (-sl2[hi]), -jnp.inf),
                          vst_s[g, i * ratio + r], None) for r in range(ratio)], m_s, acc_s)
    o_sel = [_col_finish(acc_s[hi], hd) for hi in range(nh)]

    _col_reset(m_s, acc_s)
    for ds in range(-(NSA_WINDOW // tq), 1):
        jj = jnp.maximum(i + ds, 0)
        kts = key_tiles(kwb_s, jj)
        dists = [rel - (ds * tq + r * tk) for r in range(ratio)]
        gone = jnp.where(i + ds >= 0, 0, NSA_WINDOW) if ds < 0 else 0
        oks = [(d >= 0) & (d + gone < NSA_WINDOW) for d in dists]
        for hi, (g, _) in enumerate(heads):
            _col_update(hi, [(_dot(kts[g][r], qh[hi])
                              + jnp.where(oks[r], dists[r].astype(F32) * (-sl2[hi]), -jnp.inf),
                              vwt_s[g, jj * ratio + r], None) for r in range(ratio)], m_s, acc_s)
    o_win = [_col_finish(acc_s[hi], hd) for hi in range(nh)]

    gates = jax.nn.sigmoid(gl_ref[...] + gb_ref[...]).T
    outs = [gates[3 * hi:3 * hi + 1] * o_cmp[hi] + gates[3 * hi + 1:3 * hi + 2] * o_sel[hi]
            + gates[3 * hi + 2:3 * hi + 3] * o_win[hi] for hi in range(nh)]
    o_ref[...] = jnp.concatenate(outs, axis=0).T


def _nsa_prompt(yb, cmp_kv, bsz, t, gate_b, slopes, tq, tk):
    nb, hd = cmp_kv.shape[2], cmp_kv.shape[3]
    qw = B_HEADS * hd
    nq = t // tq
    first = qw // LANE
    dva = hd + ONES_PAD
    kv_spec = lambda c: pl.BlockSpec((t, LANE), lambda b, i: (b, first + c))
    return pl.pallas_call(
        functools.partial(_nsa_prompt_body, t=t, tq=tq, tk=tk, nb=nb, hd=hd, k_sel=min(NSA_TOPK, nb)),
        grid=(bsz, nq),
        in_specs=[pl.BlockSpec(memory_space=pltpu.SMEM),
                  pl.BlockSpec((tq, qw), lambda b, i: (b * nq + i, 0)),
                  pl.BlockSpec((None, 2 * B_KV_HEADS, nb, hd), lambda b, i: (b, 0, 0, 0)),
                  kv_spec(2), kv_spec(3), kv_spec(4), kv_spec(5),
                  pl.BlockSpec((tq, LANE), lambda b, i: (b * nq + i, first + 6)),
                  pl.BlockSpec((1, LANE), lambda b, i: (0, 0))],
        out_specs=pl.BlockSpec((tq, qw), lambda b, i: (b * nq + i, 0)),
        out_shape=jax.ShapeDtypeStruct((bsz * t, qw), F32),
        scratch_shapes=[pltpu.VMEM((B_KV_HEADS, t, hd), BF16), pltpu.VMEM((B_KV_HEADS, t, hd), BF16),
                        pltpu.VMEM((B_KV_HEADS, t // tk, dva, tk), BF16),
                        pltpu.VMEM((B_KV_HEADS, t // tk, dva, tk), BF16),
                        pltpu.VMEM((B_HEADS, tk, tq), F32), pltpu.VMEM((B_HEADS, 1, tq), F32),
                        pltpu.VMEM((B_HEADS, dva, tq), F32), pltpu.VMEM((2, B_HEADS, tq // tk, tk, tq), F32)],
        compiler_params=_params(("parallel", "arbitrary")),
        name="nsa_attn_prompt",
    )(slopes, yb, cmp_kv, yb, yb, yb, yb, yb, gate_b)


def _diff_decode_body(pt_ref, lam_ref, q_ref, new_ref, gain_ref, *refs, n_pp, page, past, nh, cd, lam_init, dec_t):
    pages = refs[:n_pp]
    o_ref = refs[n_pp]
    m_s, l_s, a_s = refs[n_pp + 1:]
    c = pl.program_id(1)
    w = 2 * cd * nh
    rows = 2 * dec_t * nh
    row = lax.broadcasted_iota(jnp.int32, (rows, 1), 0)
    rh = row % nh
    rt = (row % (dec_t * nh)) // nh
    slope2 = jnp.zeros((rows, 1), F32)
    for h in range(nh):
        slope2 = jnp.where(rh == h, 2.0 ** (-2 * (h + 1)) * LOG2E, slope2)
    qb = (q_ref[...] * (cd ** -0.5 * LOG2E)).astype(BF16)

    @pl.when(c == 0)
    def _():
        new = new_ref[...]
        kn = new[:, :w].astype(BF16)
        vn = new[:, w:].astype(BF16)
        col = lax.broadcasted_iota(jnp.int32, (1, new.shape[0]), 1)
        dist = rt - col
        s = jnp.where((dist >= 0) & (col < dec_t), _dot_nt(qb, kn) - slope2 * dist.astype(F32), -jnp.inf)
        m_s[...], l_s[...], a_s[...] = _online(s, vn, *_softmax_init(rows, w))

    kt = jnp.concatenate([pages[p][0:w, :].astype(BF16) for p in range(n_pp)], axis=1)
    vt = jnp.concatenate([pages[p][w:2 * w, :].astype(BF16) for p in range(n_pp)], axis=1)
    col = lax.broadcasted_iota(jnp.int32, (1, n_pp * page), 1)
    dist = (past + rt) - (c * (n_pp * page) + col)
    s = _dot(qb, kt) - slope2 * dist.astype(F32)
    m, l, a = _online(s, vt, m_s[...], l_s[...], a_s[...], v_t=True)
    m_s[...], l_s[...], a_s[...] = m, l, a

    @pl.when(c == pl.num_programs(1) - 1)
    def _():
        full = _finish(l, a)
        own = jnp.zeros((rows, 2 * cd), F32)
        for h in range(nh):
            own = jnp.where(rh == h, full[:, h * 2 * cd:(h + 1) * 2 * cd], own)
        half = rows // 2
        o = own[:half] - lam_ref[0] * own[half:]
        o_ref[...] = _rms(o, gain_ref[...]) * (1.0 - lam_init)


def _diff_decode(q_rows, new_kv, cache_t, layer, page_table, gain, lam, lam_init, dec_t, past):
    bsz, rows, w = q_rows.shape
    n_pages = page_table.shape[1]
    page = cache_t.shape[3]
    n_pp = math.gcd(PAGES_PER_STEP, n_pages)
    nh = C_HEADS
    cd = w // (2 * nh)

    def page_spec(p):
        return pl.BlockSpec((None, None, 2 * w, page),
                            lambda b, c, pt: (layer, pt[b * n_pages + c * n_pp + p], 0, 0))

    grid_spec = pltpu.PrefetchScalarGridSpec(
        num_scalar_prefetch=1,
        grid=(bsz, n_pages // n_pp),
        in_specs=[pl.BlockSpec(memory_space=pltpu.SMEM),
                  pl.BlockSpec((None, rows, w), lambda b, c, pt: (b, 0, 0)),
                  pl.BlockSpec((None, NEW_PAD, 2 * w), lambda b, c, pt: (b, 0, 0)),
                  pl.BlockSpec((1, 2 * cd), lambda b, c, pt: (0, 0))]
        + [page_spec(p) for p in range(n_pp)],
        out_specs=pl.BlockSpec((None, rows // 2, 2 * cd), lambda b, c, pt: (b, 0, 0)),
        scratch_shapes=[pltpu.VMEM((rows, 1), F32), pltpu.VMEM((rows, 1), F32), pltpu.VMEM((rows, w), F32)],
    )
    return pl.pallas_call(
        functools.partial(_diff_decode_body, n_pp=n_pp, page=page, past=past, nh=nh, cd=cd,
                          lam_init=lam_init, dec_t=dec_t),
        grid_spec=grid_spec,
        out_shape=jax.ShapeDtypeStruct((bsz, rows // 2, 2 * cd), F32),
        compiler_params=_params(("parallel", "arbitrary")),
        name="diff_attn_decode",
    )(page_table.reshape(-1), lam, q_rows, new_kv, gain.reshape(1, 2 * cd), *([cache_t] * n_pp))


def _nsa_decode_body(pt_ref, qh_ref, qd_ref, wc_ref, wn_ref, sn_ref, wbd_ref, cc_ref, gl_ref, gb_ref, *refs,
                     n_pp, page, past, hd, dec_t, n_pages, k_past, pitch):
    pages = refs[:n_pp]
    o_ref = refs[n_pp]
    x_s, ak_s, av_s, mtok_s, ocw_s, m_s, l_s, a_s = refs[n_pp + 1:]
    ph = pl.program_id(1)
    c = pl.program_id(2)
    last = pl.num_programs(2) - 1
    tp = DEC_ROWS
    rows = B_GROUP * tp
    row = lax.broadcasted_iota(jnp.int32, (rows, 1), 0)
    rt = row % tp
    scale2 = hd ** -0.5 * LOG2E
    qpos = past + rt

    def slope2(g):
        return jnp.where(row < tp, 2.0 ** (-(2 * (g * B_GROUP) + 1)) * LOG2E,
                         2.0 ** (-(2 * (g * B_GROUP + 1) + 1)) * LOG2E)

    def new_scores(qd, kn, sl):
        col = lax.broadcasted_iota(jnp.int32, (1, kn.shape[0]), 1)
        d = rt - col
        return jnp.where((d >= 0) & (col < dec_t), _dot_nt(qd, kn) - sl * d.astype(F32), -jnp.inf)

    @pl.when(ph == 0)
    def _():
        for p in range(n_pp):
            r0 = pl.multiple_of((c * n_pp + p) * pitch, 8)
            x_s[pl.ds(r0, 4 * hd), :] = pages[p][...]

    @pl.when((ph == 0) & (c == last))
    def _():
        unroll = 2

        def body(du, accs):
            accs = list(accs)
            for u in range(unroll):
                dp = du * unroll + u
                for cc in range(2):
                    lhs = jnp.concatenate(
                        [jnp.concatenate([x_s[pl.ds(cc * 2 * hd + g * hd + 2 * dp + e, n_pages, stride=pitch), :]
                                          for e in range(2)], axis=1) for g in range(B_KV_HEADS)],
                        axis=0).astype(BF16)
                    accs[cc] = accs[cc] + _dot(lhs, wbd_ref[cc, dp])
            return tuple(accs)

        zero = jnp.zeros((B_KV_HEADS * n_pages, 2 * hd), F32)
        ak_s[...], av_s[...] = lax.fori_loop(0, hd // (2 * unroll), body, (zero, zero))
        pgi = lax.broadcasted_iota(jnp.int32, (1, n_pages), 1)
        lane = lax.broadcasted_iota(jnp.int32, (1, page), 1)
        for g in range(B_KV_HEADS):
            sl = slope2(g)
            kc = (ak_s[g * n_pages:(g + 1) * n_pages, :] + cc_ref[0]).astype(BF16)
            vc = (av_s[g * n_pages:(g + 1) * n_pages, :] + cc_ref[1]).astype(BF16)
            ss = []
            for hf in range(2):
                qh = (qh_ref[g, hf] * scale2).astype(BF16)
                dist = qpos - ((2 * pgi + hf) * NSA_BLOCK + (NSA_BLOCK - 1))
                ss.append(jnp.where(dist >= 0, _dot_nt(qh, kc) - sl * dist.astype(F32), -jnp.inf))
            m = jnp.maximum(jnp.max(ss[0], axis=1, keepdims=True), jnp.max(ss[1], axis=1, keepdims=True))
            m = jnp.where(m == -jnp.inf, 0.0, m)
            es = [jnp.exp2(s - m) for s in ss]
            den = jnp.sum(es[0], axis=1, keepdims=True) + jnp.sum(es[1], axis=1, keepdims=True)
            den = jnp.where(den > 0, den, 1.0)
            ps = [e / den for e in es]
            full = [_dot(p.astype(BF16), vc) for p in ps]
            ocw_s[g, 0] = full[0][:, :hd] + full[1][:, hd:]

            imps = [p[:tp] + p[tp:] for p in ps]
            sc = [jnp.where(pgi == 0, jnp.inf, imps[0]), jnp.where(pgi == n_pages - 1, jnp.inf, imps[1])]
            rank = [jnp.zeros((tp, n_pages), F32), jnp.zeros((tp, n_pages), F32)]
            for pg in range(n_pages):
                for hf in range(2):
                    colv = sc[hf][:, pg:pg + 1]
                    for h2 in range(2):
                        tie = jnp.where(2 * pgi + h2 > 2 * pg + hf, 1.0, 0.0)
                        rank[h2] = rank[h2] + jnp.where(colv > sc[h2], 1.0, jnp.where(colv == sc[h2], tie, 0.0))
            sel = [jnp.where(r < k_past, 1.0, 0.0) for r in rank]
            sel = [jnp.concatenate([s_, s_], axis=0) for s_ in sel]
            for pg in range(n_pages):
                on = jnp.where(lane < NSA_BLOCK, sel[0][:, pg:pg + 1], sel[1][:, pg:pg + 1])
                dist = qpos - (pg * page + lane)
                lo = (pg % n_pp) * page
                mtok_s[g, pg // n_pp, :, lo:lo + page] = jnp.where(on > 0.5, dist.astype(F32) * (-sl), NEG_BIG)

            qd = (qd_ref[g] * scale2).astype(BF16)
            wkt = wc_ref[g * hd:(g + 1) * hd, :].astype(BF16)
            wvt = wc_ref[(B_KV_HEADS + g) * hd:(B_KV_HEADS + g + 1) * hd, :].astype(BF16)
            nwin = wkt.shape[1]
            d1 = (nwin + rt) - lax.broadcasted_iota(jnp.int32, (1, nwin), 1)
            s1 = jnp.where((d1 >= 0) & (d1 < NSA_WINDOW), _dot(qd, wkt) - sl * d1.astype(F32), -jnp.inf)
            wn = wn_ref[...]
            kn = wn[:, g * hd:(g + 1) * hd].astype(BF16)
            vn = wn[:, (B_KV_HEADS + g) * hd:(B_KV_HEADS + g + 1) * hd].astype(BF16)
            s2 = new_scores(qd, kn, sl)
            m = jnp.maximum(jnp.max(s1, axis=1, keepdims=True), jnp.max(s2, axis=1, keepdims=True))
            m = jnp.where(m == -jnp.inf, 0.0, m)
            e1 = jnp.exp2(s1 - m)
            e2 = jnp.exp2(s2 - m)
            den = jnp.sum(e1, axis=1, keepdims=True) + jnp.sum(e2, axis=1, keepdims=True)
            ocw_s[g, 1] = (_dot_nt(e1.astype(BF16), wvt) + _dot(e2.astype(BF16), vn)) / jnp.where(den > 0, den, 1.0)

    @pl.when(ph == 1)
    def _():
        for g in range(B_KV_HEADS):
            sl = slope2(g)
            qd = (qd_ref[g] * scale2).astype(BF16)

            @pl.when(c == 0)
            def _(g=g, sl=sl, qd=qd):
                sn = sn_ref[...]
                kn = sn[:, g * hd:(g + 1) * hd].astype(BF16)
                vn = sn[:, (B_KV_HEADS + g) * hd:(B_KV_HEADS + g + 1) * hd].astype(BF16)
                m_s[g], l_s[g], a_s[g] = _online(new_scores(qd, kn, sl), vn, *_softmax_init(rows, hd))

            kt = jnp.concatenate([pages[p][g * hd:(g + 1) * hd, :].astype(BF16) for p in range(n_pp)], axis=1)
            vt = jnp.concatenate([pages[p][(B_KV_HEADS + g) * hd:(B_KV_HEADS + g + 1) * hd, :].astype(BF16)
                                  for p in range(n_pp)], axis=1)
            s = _dot(qd, kt) + mtok_s[g, c]
            m_s[g], l_s[g], a_s[g] = _online(s, vt, m_s[g], l_s[g], a_s[g], v_t=True)

        @pl.when(c == last)
        def _():
            gates = jax.nn.sigmoid(gl_ref[...] + gb_ref[...])
            for g in range(B_KV_HEADS):
                o_sel = _finish(l_s[g], a_s[g])
                o_cmp = ocw_s[g, 0]
                o_win = ocw_s[g, 1]
                for n in range(B_GROUP):
                    r = slice(n * tp, (n + 1) * tp)
                    base = (g * B_GROUP + n) * 3
                    o_ref[g, n] = (gates[:, base:base + 1] * o_cmp[r] + gates[:, base + 1:base + 2] * o_sel[r]
                                   + gates[:, base + 2:base + 3] * o_win[r])


def _nsa_decode(q_half, q_plain, win_t, win_new, sel_new, cache_t, layer, page_table, w_bdt, cconst2,
                gate_logits, gate_b, dec_t, past, k_past):
    bsz = q_plain.shape[0]
    hd = q_plain.shape[-1]
    rows = q_plain.shape[2]
    n_pages = page_table.shape[1]
    page = cache_t.shape[3]
    nwin = win_t.shape[3]
    n_pp = math.gcd(PAGES_PER_STEP, n_pages)
    n_ch = n_pages // n_pp
    pitch = 4 * hd + PAGE_PITCH_PAD

    def page_spec(p):
        return pl.BlockSpec((None, None, 4 * hd, page),
                            lambda b, ph, c, pt: (layer, pt[b * n_pages + c * n_pp + p], ph, 0))

    fix = lambda *shape: pl.BlockSpec(shape, lambda b, ph, c, pt: (0,) * len(shape))
    grid_spec = pltpu.PrefetchScalarGridSpec(
        num_scalar_prefetch=1,
        grid=(bsz, 2, n_ch),
        in_specs=[pl.BlockSpec((None, B_KV_HEADS, 2, rows, 2 * hd), lambda b, ph, c, pt: (b, 0, 0, 0, 0)),
                  pl.BlockSpec((None, B_KV_HEADS, rows, hd), lambda b, ph, c, pt: (b, 0, 0, 0)),
                  pl.BlockSpec((None, None, 4 * hd, nwin), lambda b, ph, c, pt: (layer, b, 0, 0)),
                  pl.BlockSpec((None, NEW_PAD, 4 * hd), lambda b, ph, c, pt: (b, 0, 0)),
                  pl.BlockSpec((None, NEW_PAD, 4 * hd), lambda b, ph, c, pt: (b, 0, 0)),
                  fix(*w_bdt.shape), fix(*cconst2.shape),
                  pl.BlockSpec((None, DEC_ROWS, LANE), lambda b, ph, c, pt: (b, 0, 0)),
                  fix(1, LANE)]
        + [page_spec(p) for p in range(n_pp)],
        out_specs=pl.BlockSpec((None, B_KV_HEADS, B_GROUP, DEC_ROWS, hd), lambda b, ph, c, pt: (b, 0, 0, 0, 0)),
        scratch_shapes=[pltpu.VMEM((n_pages * pitch, page), F32),
                        pltpu.VMEM((B_KV_HEADS * n_pages, 2 * hd), F32),
                        pltpu.VMEM((B_KV_HEADS * n_pages, 2 * hd), F32),
                        pltpu.VMEM((B_KV_HEADS, n_ch, rows, n_pp * page), F32),
                        pltpu.VMEM((B_KV_HEADS, 2, rows, hd), F32),
                        pltpu.VMEM((B_KV_HEADS, rows, 1), F32), pltpu.VMEM((B_KV_HEADS, rows, 1), F32),
                        pltpu.VMEM((B_KV_HEADS, rows, hd), F32)],
    )
    return pl.pallas_call(
        functools.partial(_nsa_decode_body, n_pp=n_pp, page=page, past=past, hd=hd, dec_t=dec_t,
                          n_pages=n_pages, k_past=k_past, pitch=pitch),
        grid_spec=grid_spec,
        out_shape=jax.ShapeDtypeStruct((bsz, B_KV_HEADS, B_GROUP, DEC_ROWS, hd), F32),
        compiler_params=_params(("parallel", "arbitrary", "arbitrary")),
        name="nsa_decode",
    )(page_table.reshape(-1), q_half, q_plain, win_t, win_new, sel_new, w_bdt, cconst2, gate_logits, gate_b,
      *([cache_t] * n_pp))


def _split_points(d_model):
    br = d_model // N_BRANCH
    hd_b = br // B_HEADS
    d_dk = br // (2 * D_HEADS)
    splits = (br, br, br, br, br, 6 * B_KV_HEADS * hd_b, 3 * B_HEADS, br, br, br,
              D_HEADS * d_dk, D_HEADS * d_dk, br, GLA_RANK, br, N_BRANCH * d_model)
    return [0] + [int(p) for p in np.cumsum(splits)]


def _layer_weights(l, d_model, w_in, ffn_w_in, ffn_w_out, w_branch, w_out, nsa_cmp_pe, nsa_cmp_w,
                   gla_gate_w2, gla_gate_b, nsa_gate_b):
    pts = _split_points(d_model)
    wt = jnp.transpose(w_in, (2, 0, 1))[:, l, :]
    seg = lambda i, j: wt[pts[i]:pts[j]]
    padl = lambda a: jnp.pad(a, ((0, LANE - a.shape[0]), (0, 0)))
    wa = seg(0, 4)
    wb = jnp.concatenate([seg(4, 6), padl(seg(6, 7))], axis=0)
    wc = seg(7, 10)
    wd = jnp.concatenate([seg(10, 13), padl(seg(13, 14)), seg(14, 15)], axis=0)
    hd = nsa_cmp_w.shape[-1]
    eye = jnp.eye(B_KV_HEADS, dtype=F32)
    w_bd = jnp.einsum('cjde,gh->cjgdhe', nsa_cmp_w[l], eye).reshape(
        2, NSA_BLOCK, B_KV_HEADS * hd, B_KV_HEADS * hd)
    w_bdt = jnp.einsum('cjde,ab->cdajbe', nsa_cmp_w[l], eye).reshape(
        2, hd // 2, 2 * 2 * NSA_BLOCK, 2 * hd)
    cvec = jnp.einsum('cjd,cjde->ce', nsa_cmp_pe[l], nsa_cmp_w[l])
    cconst = jnp.repeat(cvec, B_KV_HEADS, axis=0).reshape(1, 4 * hd)
    cconst2 = jnp.concatenate([cvec, cvec], axis=1).reshape(2, 1, 2 * hd)
    w2 = jnp.pad(gla_gate_w2[l], ((0, LANE - GLA_RANK), (0, 0)))
    return dict(
        proj=[a.astype(BF16) for a in (wa, wb, wc, wd)],
        w_merge=seg(15, 16).astype(BF16), w_branch=w_branch[l].astype(BF16), w_out=w_out[l].astype(BF16),
        ffn_in=ffn_w_in[l].astype(BF16), ffn_out=ffn_w_out[l].astype(BF16),
        w_bd=w_bd.astype(BF16), w_bdt=w_bdt.astype(BF16), cconst=cconst, cconst2=cconst2,
        gla_w2=w2, gla_b=gla_gate_b[l].reshape(1, -1),
        gate_b=jnp.pad(nsa_gate_b[l].reshape(1, -1), ((0, 0), (0, LANE - 3 * B_HEADS))),
    )


def _mixer_common(x, lw, g2, vec_a, vec_d, bsz, t, state_a, state_d, br, cache_rows_t=False):
    a_dk = br // A_HEADS
    d_dk = br // (2 * D_HEADS)
    d_dv = br // D_HEADS
    ya, yb, yc, yd, *new_t = _proj(x, g2, lw['proj'], br, (bsz, t) if cache_rows_t else None)
    o_a, st_a = _linear_mixer(functools.partial(_hgrn_body, w=br, dk=a_dk), ya, bsz, t, vec_a, state_a,
                              A_HEADS, a_dk, a_dk)
    o_d, st_d = _linear_mixer(functools.partial(_gla_body, wk=D_HEADS * d_dk, wv=br, dk=d_dk, dv=d_dv),
                              yd, bsz, t, vec_d, state_d, D_HEADS, d_dk, d_dv)
    return yb, yc, o_a, st_a, o_d, st_d, new_t


def _feature_major(cache):
    l, n, t = cache.shape[:3]
    return jnp.transpose(cache, (0, 1, 3, 4, 5, 2)).reshape(l, n, -1, t)


def kernel(x_prompt, x_sample, cache_nsa_kv, cache_nsa_win, cache_diff_kv, state_hgrn, state_gla, page_table,
           norm_gains, ffn_w_in, ffn_w_out, w_in, hgrn_lb_logits, hgrn_norm_gain, nsa_cmp_pe, nsa_cmp_w,
           nsa_gate_b, diff_lambda, diff_norm_gain, gla_gate_w2, gla_gate_b, gla_norm_gain, w_branch, w_out):
    bp, tp, d_model = x_prompt.shape
    bs, ts, _ = x_sample.shape
    depth = w_in.shape[0]
    br = d_model // N_BRANCH
    hd = br // B_HEADS
    cd2 = br // C_HEADS
    page = cache_nsa_kv.shape[2]
    n_pages = page_table.shape[1]
    past = n_pages * page
    nwin = cache_nsa_win.shape[2]
    assert past % NSA_BLOCK == 0 and ts <= DEC_ROWS and tp % NSA_BLOCK == 0
    assert past // NSA_BLOCK >= NSA_TOPK and page == 2 * NSA_BLOCK and nwin == NSA_WINDOW
    k_past = NSA_TOPK - 1
    tq = min(256, tp)
    tk = min(128, tq)

    lb_cum = jnp.cumsum(jax.nn.softmax(hgrn_lb_logits.astype(F32), axis=0), axis=0)
    lower = lb_cum - lb_cum[0]
    slopes = 2.0 ** (-np.arange(1, B_HEADS + C_HEADS + 1, dtype=np.float64))
    sl_b = jnp.asarray(slopes[0::2], F32)
    sl_c = jnp.asarray(slopes[1::2], F32)

    nsa_pool_t = _feature_major(cache_nsa_kv)
    diff_pool_t = _feature_major(cache_diff_kv)
    win_pool_t = _feature_major(cache_nsa_win)

    xp = x_prompt.reshape(bp * tp, d_model)
    xs = x_sample.reshape(bs * ts, d_model)
    outs = {k: [] for k in ('kvp', 'kvs', 'winp', 'wins', 'dkp', 'dks', 'hp', 'hs', 'gp', 'gs')}

    for l in range(depth):
        lw = _layer_weights(l, d_model, w_in, ffn_w_in, ffn_w_out, w_branch, w_out, nsa_cmp_pe, nsa_cmp_w,
                            gla_gate_w2, gla_gate_b, nsa_gate_b)
        g = norm_gains[l]
        lb = lower[l].reshape(1, br)
        vec_a = (jnp.log1p(-lb), jnp.log(lb), hgrn_norm_gain[l].reshape(1, br))
        vec_d = (lw['gla_w2'], lw['gla_b'], gla_norm_gain[l].reshape(1, br))
        lv = diff_lambda[l].astype(F32)
        lam_init = 0.8 - 0.6 * math.exp(-0.3 * l)
        lam = (jnp.exp(jnp.sum(lv[0] * lv[1])) - jnp.exp(jnp.sum(lv[2] * lv[3])) + lam_init).reshape(1)

        xp = _ffn(xp, g[0], g[1], lw['ffn_in'][0], lw['ffn_out'][0])
        yb, yc, o_a, st_a, o_d, st_d, (kv_t, win_t, dk_t) = _mixer_common(
            xp, lw, g[2], vec_a, vec_d, bp, tp, None, None, br, cache_rows_t=True)
        outs['hp'].append(st_a)
        outs['gp'].append(st_d)
        outs['kvp'].append(kv_t)
        outs['winp'].append(win_t[:, :, -min(NSA_WINDOW, tp):])
        outs['dkp'].append(dk_t)

        o_c = _diff_prompt(yc, bp, tp, diff_norm_gain[l], lam, sl_c, lam_init, tq, tk)
        cmp_kv = _compress_prompt(yb, bp, tp, lw['w_bd'], lw['cconst'], hd)
        o_b = _nsa_prompt(yb, cmp_kv, bp, tp, lw['gate_b'], sl_b, tq, tk)

        xp = _merge(xp, g[2], g[3], (o_a, o_b, o_c, o_d), lw['w_merge'], lw['w_branch'], lw['w_out'])
        xp = _ffn(xp, g[4], g[5], lw['ffn_in'][1], lw['ffn_out'][1])

        xs = _ffn(xs, g[0], g[1], lw['ffn_in'][0], lw['ffn_out'][0])
        yb, yc, o_a, st_a, o_d, st_d, _ = _mixer_common(xs, lw, g[2], vec_a, vec_d, bs, ts,
                                                        state_hgrn[l], state_gla[l], br)
        outs['hs'].append(st_a)
        outs['gs'].append(st_d)
        new_kv = yb[:, br:br + 6 * B_KV_HEADS * hd].reshape(bs, ts, 6, B_KV_HEADS * hd)
        outs['kvs'].append(new_kv[:, :, :4].reshape(bs, ts, 4, B_KV_HEADS, hd))
        new_win = new_kv[:, :, 4:].reshape(bs, ts, 2 * B_KV_HEADS * hd)
        outs['wins'].append(jnp.concatenate(
            [cache_nsa_win[l][:, ts:], new_win.reshape(bs, ts, 2, B_KV_HEADS, hd)], axis=1))
        outs['dks'].append(yc[:, br:3 * br].reshape(bs, ts, 2, C_HEADS, cd2))
        pad_new = lambda a: jnp.pad(a, ((0, 0), (0, NEW_PAD - ts), (0, 0)))

        cq = yc[:, 0:br].reshape(bs, ts, C_HEADS, 2, cd2 // 2)
        sel = (jnp.arange(C_HEADS)[:, None, None, None] == jnp.arange(C_HEADS)[None, None, :, None]) & \
              (jnp.arange(2)[None, :, None, None] == jnp.arange(2)[None, None, None, :])
        q_rows = jnp.einsum('bthmd,hmgn->bmthgnd', cq, sel.astype(F32)).reshape(bs, 2 * ts * C_HEADS, br)
        new_c = pad_new(yc[:, br:3 * br].reshape(bs, ts, 2 * br))
        o_c = _diff_decode(q_rows, new_c, diff_pool_t, l, page_table, diff_norm_gain[l], lam, lam_init, ts, past)
        o_c = o_c.reshape(bs * ts, br)

        bq = yb[:, 0:br].reshape(bs, ts, B_KV_HEADS, B_GROUP, hd).transpose(0, 2, 3, 1, 4)
        bq = jnp.pad(bq, ((0, 0), (0, 0), (0, 0), (0, DEC_ROWS - ts), (0, 0)))
        q_plain = bq.reshape(bs, B_KV_HEADS, B_GROUP * DEC_ROWS, hd)
        zero = jnp.zeros_like(q_plain)
        q_half = jnp.stack([jnp.concatenate([q_plain, zero], axis=-1),
                            jnp.concatenate([zero, q_plain], axis=-1)], axis=2)
        gl = yb[:, br + 6 * B_KV_HEADS * hd:].reshape(bs, ts, LANE)
        gl = jnp.pad(gl, ((0, 0), (0, DEC_ROWS - ts), (0, 0)))
        o_b = _nsa_decode(q_half, q_plain, win_pool_t, pad_new(new_win),
                          pad_new(new_kv[:, :, 2:4].reshape(bs, ts, 4 * hd)), nsa_pool_t, l, page_table,
                          lw['w_bdt'], lw['cconst2'], gl, lw['gate_b'], ts, past, k_past)
        o_b = o_b[:, :, :, :ts].transpose(0, 3, 1, 2, 4).reshape(bs * ts, br)

        xs = _merge(xs, g[2], g[3], (o_a, o_b, o_c, o_d), lw['w_merge'], lw['w_branch'], lw['w_out'])
        xs = _ffn(xs, g[4], g[5], lw['ffn_in'][1], lw['ffn_out'][1])

    st = lambda k: jnp.stack(outs[k])

    def token_major(k, a, b):
        x = st(k)
        return x.reshape(x.shape[:2] + (a, b, -1, x.shape[-1])).transpose(0, 1, 5, 2, 3, 4)

    return (xp.reshape(bp, tp, d_model), xs.reshape(bs, ts, d_model),
            token_major('kvp', 4, B_KV_HEADS), st('kvs'), token_major('winp', 2, B_KV_HEADS), st('wins'),
            token_major('dkp', 2, C_HEADS), st('dks'), st('hp'), st('hs'), st('gp'), st('gs'))
```

```python
import functools
import math

import numpy as np
import jax
import jax.numpy as jnp
from jax import lax
from jax.experimental import pallas as pl
from jax.experimental.pallas import tpu as pltpu

F32 = jnp.float32
BF16 = jnp.bfloat16
HI = lax.Precision.HIGHEST

N_BRANCH = 4
A_HEADS = 4
B_HEADS = 4
B_KV_HEADS = 2
B_GROUP = 2
C_HEADS = 4
D_HEADS = 4
NSA_BLOCK = 64
NSA_TOPK = 16
NSA_WINDOW = 512
GLA_RANK = 16
GLA_TAU = 16.0
NORM_EPS = 1e-6
LANE = 128
LIN_GROUP = 16
NEW_PAD = 16
DEC_ROWS = 8
ONES_PAD = 16
PAGE_PITCH_PAD = 8
PAGES_PER_STEP = 32
HALF_PAGES_PER_STEP = 64
VMEM_LIMIT = 56 * 1024 * 1024
LOG2E = 1.4426950408889634
NEG_BIG = -1e30

NT_DIMS = (((1,), (1,)), ((), ()))
TN_DIMS = (((0,), (0,)), ((), ()))


def _params(semantics):
    return pltpu.CompilerParams(dimension_semantics=semantics, vmem_limit_bytes=VMEM_LIMIT)


def _rms(x, g):
    return x * lax.rsqrt(jnp.mean(x * x, axis=-1, keepdims=True) + NORM_EPS) * g


def _dot(a, b):
    return jnp.dot(a, b, preferred_element_type=F32)


def _dot_nt(a, b):
    return lax.dot_general(a, b, NT_DIMS, preferred_element_type=F32)


def _row_tile(n, cap):
    t = min(n, cap)
    while n % t or t % 8:
        t -= 1
    return t


def _ffn_body(x_ref, gpre_ref, gpost_ref, wg_ref, wu_ref, wo_ref, o_ref, xn_ref, acc_ref):
    f = pl.program_id(1)

    @pl.when(f == 0)
    def _():
        xn_ref[...] = _rms(x_ref[...], gpre_ref[...]).astype(BF16)
        acc_ref[...] = jnp.zeros_like(acc_ref)

    xn = xn_ref[...]
    gate = _dot(xn, wg_ref[...])
    up = _dot(xn, wu_ref[...])
    act = (gate * jax.nn.sigmoid(gate) * up).astype(BF16)
    acc_ref[...] += _dot(act, wo_ref[...])

    @pl.when(f == pl.num_programs(1) - 1)
    def _():
        o_ref[...] = x_ref[...] + 0.5 * _rms(acc_ref[...], gpost_ref[...])


def _ffn(x, g_pre, g_post, w_in, w_out):
    n, d = x.shape
    dff = w_out.shape[0]
    tf = 256
    nf = dff // tf
    tm = _row_tile(n, 1024)
    return pl.pallas_call(
        _ffn_body,
        grid=(n // tm, nf),
        in_specs=[
            pl.BlockSpec((tm, d), lambda i, f: (i, 0)),
            pl.BlockSpec((1, d), lambda i, f: (0, 0)),
            pl.BlockSpec((1, d), lambda i, f: (0, 0)),
            pl.BlockSpec((d, tf), lambda i, f: (0, f)),
            pl.BlockSpec((d, tf), lambda i, f: (0, nf + f)),
            pl.BlockSpec((tf, d), lambda i, f: (f, 0)),
        ],
        out_specs=pl.BlockSpec((tm, d), lambda i, f: (i, 0)),
        out_shape=jax.ShapeDtypeStruct((n, d), F32),
        scratch_shapes=[pltpu.VMEM((tm, d), BF16), pltpu.VMEM((tm, d), F32)],
        compiler_params=_params(("parallel", "arbitrary")),
        name="ffn",
    )(x, g_pre.reshape(1, d), g_post.reshape(1, d), w_in, w_in, w_out)


def _proj_body(x_ref, g_ref, wa, wb, wc, wd, oa, ob, oc, od, *cache_t, br):
    xn = _rms(x_ref[...], g_ref[...]).astype(BF16)
    ys = [_dot_nt(xn, w[...]) for w in (wa, wb, wc, wd)]
    for y, o in zip(ys, (oa, ob, oc, od)):
        o[...] = y
    if cache_t:
        kv_t, win_t, dk_t = cache_t
        kv_t[...] = ys[1][:, br:3 * br].T
        win_t[...] = ys[1][:, 3 * br:4 * br].T
        dk_t[...] = ys[2][:, br:3 * br].T


def _proj(x, g, ws, br, seqs=None):
    n, d = x.shape
    tm = _row_tile(n, 512)
    out_specs = [pl.BlockSpec((tm, w.shape[0]), lambda i: (i, 0)) for w in ws]
    out_shape = [jax.ShapeDtypeStruct((n, w.shape[0]), F32) for w in ws]
    if seqs is not None:
        bsz, t = seqs
        per = t // tm
        for rows in (2 * br, br, 2 * br):
            out_specs.append(pl.BlockSpec((None, rows, tm), lambda i: (i // per, 0, i % per)))
            out_shape.append(jax.ShapeDtypeStruct((bsz, rows, t), F32))
    return pl.pallas_call(
        functools.partial(_proj_body, br=br),
        grid=(n // tm,),
        in_specs=[pl.BlockSpec((tm, d), lambda i: (i, 0)), pl.BlockSpec((1, d), lambda i: (0, 0))]
        + [pl.BlockSpec(w.shape, lambda i: (0, 0)) for w in ws],
        out_specs=out_specs,
        out_shape=out_shape,
        compiler_params=_params(("parallel",)),
        name="mixer_in_proj",
    )(x, g.reshape(1, d), *ws)


def _merge_body(x_ref, g2_ref, g3_ref, oa, ob, oc, od, wm_ref, wb_ref, wo_ref, out_ref):
    x = x_ref[...]
    d = x.shape[1]
    h = _rms(x, g2_ref[...]).astype(BF16)
    s = None
    for n, br in enumerate((oa, ob, oc, od)):
        gate = jax.nn.sigmoid(_dot_nt(h, wm_ref[n * d:(n + 1) * d, :]))
        term = gate * _dot(br[...].astype(BF16), wb_ref[n])
        s = term if s is None else s + term
    y = _dot(s.astype(BF16), wo_ref[...])
    out_ref[...] = x + _rms(y, g3_ref[...])


def _merge(x, g2, g3, branches, w_merge, w_branch, w_out):
    n, d = x.shape
    br = branches[0].shape[1]
    tm = _row_tile(n, 256)
    row = lambda i: (i, 0)
    fix2 = lambda i: (0, 0)
    return pl.pallas_call(
        _merge_body,
        grid=(n // tm,),
        in_specs=[pl.BlockSpec((tm, d), row), pl.BlockSpec((1, d), fix2), pl.BlockSpec((1, d), fix2)]
        + [pl.BlockSpec((tm, br), row)] * N_BRANCH
        + [pl.BlockSpec(w_merge.shape, fix2), pl.BlockSpec(w_branch.shape, lambda i: (0, 0, 0)),
           pl.BlockSpec(w_out.shape, fix2)],
        out_specs=pl.BlockSpec((tm, d), row),
        out_shape=jax.ShapeDtypeStruct((n, d), F32),
        compiler_params=_params(("parallel",)),
        name="merge",
    )(x, g2.reshape(1, d), g3.reshape(1, d), *branches, w_merge, w_branch, w_out)


def _softplus_neg_abs(z):
    return jnp.log(1.0 + jnp.exp(-jnp.abs(z)))


def _log_sigmoid(z):
    return jnp.minimum(z, 0.0) - _softplus_neg_abs(z)


def _dot_exact_lhs(a, x):
    hi = x.astype(BF16)
    lo = (x - hi.astype(F32)).astype(BF16)
    return _dot(a, hi) + _dot(a, lo)


def _lin_core(q, k, v, lf, gg, gain, s0_ref, o_ref, st_ref, qt_s, kt_s, vt_s, dec_s, oi_s, st_s,
              *, dk, dv, t_valid, n_seq):
    rows, wk = q.shape
    tt = rows // n_seq
    wv = v.shape[1]
    grp = LIN_GROUP
    t_idx = pl.program_id(1)

    @pl.when(t_idx == 0)
    def _():
        st_s[...] = s0_ref[...]

    if t_valid is not None:
        tok = t_idx * tt + lax.broadcasted_iota(jnp.int32, (rows, 1), 0) % tt
        lf = jnp.where(tok < t_valid, lf, 0.0)

    r = lax.broadcasted_iota(jnp.int32, (tt, tt), 0)
    c = lax.broadcasted_iota(jnp.int32, (tt, tt), 1)
    same = (r // grp) == (c // grp)
    tri = jnp.where(same & (c <= r), 1.0, 0.0).astype(BF16)
    ones_g = jnp.where(same, 1.0, 0.0).astype(BF16)
    lfs = [lf[si * tt:(si + 1) * tt] for si in range(n_seq)]
    b = jnp.concatenate([_dot_exact_lhs(tri, x) for x in lfs], axis=0)
    bl = jnp.concatenate([_dot_exact_lhs(ones_g, x) for x in lfs], axis=0)

    qt_s[...] = (q * jnp.exp(b)).astype(BF16)
    kt_s[...] = (k * jnp.exp(bl - b)).astype(BF16)
    vt_s[...] = v.astype(BF16)
    dec_s[...] = jnp.exp(bl)

    hk = lax.broadcasted_iota(jnp.int32, (wk, wv), 0) // dk
    hv = lax.broadcasted_iota(jnp.int32, (wk, wv), 1) // dv
    ones_hd = jnp.where(hk == hv, 1.0, 0.0).astype(BF16)

    rowm = lax.broadcasted_iota(jnp.int32, (rows, 1), 0) % grp
    od = jnp.zeros((rows, wv), F32)
    for d in range(grp):
        ks = k if d == 0 else pltpu.roll(k, d, 0)
        bs = b if d == 0 else pltpu.roll(b, d, 0)
        vs = v if d == 0 else pltpu.roll(v, d, 0)
        e = jnp.exp(jnp.minimum(b - bs, 0.0))
        z = jnp.where(rowm >= d, q * ks * e, 0.0).astype(BF16)
        od = od + _dot(z, ones_hd) * vs

    mv = lax.broadcasted_iota(jnp.int32, (wv, wk), 0) // dv
    mk = lax.broadcasted_iota(jnp.int32, (wv, wk), 1) // dk
    mbd = jnp.where(mv == mk, 1.0, 0.0).astype(F32)

    def step(i, carry):
        for si in range(n_seq):
            r0 = pl.multiple_of(si * tt + i * grp, grp)
            qg = qt_s[pl.ds(r0, grp), :]
            kg = kt_s[pl.ds(r0, grp), :]
            vg = vt_s[pl.ds(r0, grp), :]
            s = st_s[si]
            oi_s[pl.ds(r0, grp), :] = _dot_nt(qg, s.astype(BF16))
            upd = lax.dot_general(vg, kg, TN_DIMS, preferred_element_type=F32)
            st_s[si] = dec_s[pl.ds(r0, 1), :] * s + mbd * upd
        return carry

    lax.fori_loop(0, tt // grp, step, 0)

    o = oi_s[...] + od
    pv = lax.broadcasted_iota(jnp.int32, (wv, wv), 0) // dv
    pw = lax.broadcasted_iota(jnp.int32, (wv, wv), 1) // dv
    avg = jnp.where(pv == pw, 1.0 / dv, 0.0).astype(BF16)
    o2 = o * o
    o2_hi = o2.astype(BF16)
    ms = _dot(o2_hi, avg) + _dot((o2 - o2_hi.astype(F32)).astype(BF16), avg)
    y = o * lax.rsqrt(ms + NORM_EPS) * gain * (gg * jax.nn.sigmoid(gg))
    o_ref[...] = y.reshape(o_ref.shape)

    @pl.when(t_idx == pl.num_programs(1) - 1)
    def _():
        st_ref[...] = st_s[...]


def _hgrn_body(y_ref, la_ref, lc_ref, gain_ref, s0_ref, o_ref, st_ref, *scratch, w, dk, t_valid):
    n_seq = y_ref.shape[0]
    y = y_ref[...].reshape(-1, y_ref.shape[2])
    q = y[:, 0:w] * (dk ** -0.5)
    z = y[:, w:2 * w]
    v = y[:, 2 * w:3 * w]
    gg = y[:, 3 * w:4 * w]
    ls = _log_sigmoid(z)
    a = la_ref[...] + ls
    cc = lc_ref[...]
    lf = jnp.maximum(a, cc) + _softplus_neg_abs(a - cc)
    k = jnp.exp(la_ref[...] + (ls - z))
    _lin_core(q, k, v, lf, gg, gain_ref[...], s0_ref, o_ref, st_ref, *scratch, dk=dk, dv=dk, t_valid=t_valid,
              n_seq=n_seq)


def _gla_body(y_ref, w2_ref, b2_ref, gain_ref, s0_ref, o_ref, st_ref, *scratch, wk, wv, dk, dv, t_valid):
    n_seq = y_ref.shape[0]
    y = y_ref[...].reshape(-1, y_ref.shape[2])
    q = y[:, 0:wk] * (dk ** -0.5)
    k = y[:, wk:2 * wk]
    v = y[:, 2 * wk:2 * wk + wv]
    lr = y[:, 2 * wk + wv:2 * wk + wv + LANE]
    gg = y[:, 2 * wk + wv + LANE:2 * wk + 2 * wv + LANE]
    u = jnp.dot(lr, w2_ref[...], precision=HI, preferred_element_type=F32) + b2_ref[...]
    lf = _log_sigmoid(u) * (1.0 / GLA_TAU)
    _lin_core(q, k, v, lf, gg, gain_ref[...], s0_ref, o_ref, st_ref, *scratch, dk=dk, dv=dv, t_valid=t_valid,
              n_seq=n_seq)


def _lin_call(body, y, n_seq, tt, vecs, s0):
    bsz, t, cw = y.shape
    wv, wk = s0.shape[1:]
    rows = n_seq * tt
    return pl.pallas_call(
        body,
        grid=(bsz // n_seq, t // tt),
        in_specs=[pl.BlockSpec((n_seq, tt, cw), lambda b, i: (b, i, 0))]
        + [pl.BlockSpec(a.shape, lambda b, i: (0, 0)) for a in vecs]
        + [pl.BlockSpec((n_seq, wv, wk), lambda b, i: (b, 0, 0))],
        out_specs=[pl.BlockSpec((n_seq, tt, wv), lambda b, i: (b, i, 0)),
                   pl.BlockSpec((n_seq, wv, wk), lambda b, i: (b, 0, 0))],
        out_shape=[jax.ShapeDtypeStruct((bsz, t, wv), F32), jax.ShapeDtypeStruct((bsz, wv, wk), F32)],
        scratch_shapes=[pltpu.VMEM((rows, wk), BF16), pltpu.VMEM((rows, wk), BF16), pltpu.VMEM((rows, wv), BF16),
                        pltpu.VMEM((rows, wk), F32), pltpu.VMEM((rows, wv), F32), pltpu.VMEM((n_seq, wv, wk), F32)],
        compiler_params=_params(("parallel", "arbitrary")),
        name="gated_linear",
    )(y, *vecs, s0)


def _state_to_bd(state):
    bsz, nh, dk, dv = state.shape
    eye = jnp.eye(nh, dtype=state.dtype)
    return jnp.einsum('bhkv,hg->bhvgk', state, eye).reshape(bsz, nh * dv, nh * dk)


def _bd_to_state(st, nh):
    bsz, wv, wk = st.shape
    dv, dk = wv // nh, wk // nh
    blocks = st.reshape(bsz, nh, dv, nh, dk)
    idx = jnp.arange(nh)
    return blocks[:, idx, :, idx, :].transpose(1, 0, 3, 2)


def _linear_mixer(body, y, bsz, t, vecs, state0, nh, dk, dv):
    assert dv & (dv - 1) == 0
    tp = -(-t // LIN_GROUP) * LIN_GROUP
    y3 = y.reshape(bsz, t, -1)
    if tp != t:
        y3 = jnp.pad(y3, ((0, 0), (0, tp - t), (0, 0)))
    tt = min(tp, LANE)
    n_seq = math.gcd(bsz, min(8, max(1, 512 // tt)))
    wk, wv = nh * dk, nh * dv
    s0 = jnp.zeros((bsz, wv, wk), F32) if state0 is None else _state_to_bd(state0)
    body = functools.partial(body, t_valid=None if tp == t else t)
    o, st = _lin_call(body, y3, n_seq, tt, vecs, s0)
    return o[:, :t].reshape(bsz * t, wv), _bd_to_state(st, nh)


def _col_update(idx, tiles, m_s, acc_s):
    m_old = m_s[idx]
    m_new = m_old
    for s, _, shift in tiles:
        smax = jnp.max(s, axis=0, keepdims=True)
        m_new = jnp.maximum(m_new, smax if shift is None else smax + shift)
    m_safe = jnp.where(m_new == -jnp.inf, 0.0, m_new)
    pv = None
    for s, vt, shift in tiles:
        p = jnp.exp2(s + ((-m_safe) if shift is None else (shift - m_safe)))
        d = _dot(vt, p.astype(BF16))
        pv = d if pv is None else pv + d
    acc_s[idx] = jnp.exp2(m_old - m_safe) * acc_s[idx] + pv
    m_s[idx] = m_new


def _col_finish(acc, dv):
    den = acc[dv:dv + 1, :]
    return acc[:dv, :] / jnp.where(den > 0, den, 1.0)


def _col_reset(m_s, acc_s):
    m_s[...] = jnp.full(m_s.shape, -jnp.inf, F32)
    acc_s[...] = jnp.zeros(acc_s.shape, F32)


def _online(s, v, m, l, a, v_t=False):
    m_new = jnp.maximum(m, jnp.max(s, axis=1, keepdims=True))
    m_safe = jnp.where(m_new == -jnp.inf, 0.0, m_new)
    p = jnp.exp2(s - m_safe)
    alpha = jnp.exp2(m - m_safe)
    l = alpha * l + jnp.sum(p, axis=1, keepdims=True)
    pv = _dot_nt(p.astype(BF16), v) if v_t else _dot(p.astype(BF16), v)
    return m_new, l, alpha * a + pv


def _finish(l, a):
    return a / jnp.where(l > 0, l, 1.0)


def _softmax_init(rows, width):
    return (jnp.full((rows, 1), -jnp.inf, F32), jnp.zeros((rows, 1), F32), jnp.zeros((rows, width), F32))


def _stage_keys_values(k_ref, v_ref, kb_s, vt_s, *, t, tk, w, k_lanes):
    kw = kb_s.shape[2]
    n_heads = vt_s.shape[0]
    ones = jnp.where(lax.broadcasted_iota(jnp.int32, (ONES_PAD, tk), 0) == 0, 1.0, 0.0).astype(BF16)

    def body(jt, carry):
        rows = pl.ds(pl.multiple_of(jt * tk, tk), tk)
        kt = k_ref[rows, :]
        for c, lo in enumerate(k_lanes):
            kb_s[c, rows, :] = kt[:, lo:lo + kw].astype(BF16)
        vt = v_ref[rows, :].T
        for hh in range(n_heads):
            vt_s[hh, jt, 0:w, :] = vt[hh * w:(hh + 1) * w, :].astype(BF16)
            vt_s[hh, jt, w:w + ONES_PAD, :] = ones
        return carry

    lax.fori_loop(0, t // tk, body, 0)


def _diff_prompt_body(sl_ref, lam_ref, q_ref, k_ref, v_ref, gain_ref, o_ref, kb_s, vt_s, boff_s, m_s, acc_s, s_s,
                      *, t, tq, tk, cd, lam_init):
    hp = pl.program_id(1)
    i = pl.program_id(2)
    w = 2 * cd
    pair = LANE // w

    lanes = [hh * w + mi * cd for hh in range(pair) for mi in range(2)]

    @pl.when(i == 0)
    def _():
        _stage_keys_values(k_ref, v_ref, kb_s, vt_s, t=t, tk=tk, w=w, k_lanes=lanes)

    qt = (q_ref[...] * (cd ** -0.5 * LOG2E)).T
    slope2 = [sl_ref[hp * pair + hh] * LOG2E for hh in range(pair)]
    chains = [(ci // 2, qt[lo:lo + cd, :].astype(BF16)) for ci, lo in enumerate(lanes)]
    rel = lax.broadcasted_iota(jnp.int32, (tk, tq), 1) - lax.broadcasted_iota(jnp.int32, (tk, tq), 0)
    rel_f = rel.astype(F32)
    for hh in range(pair):
        boff_s[hh] = rel_f * (-slope2[hh])
    _col_reset(m_s, acc_s)
    ratio = tq // tk

    def score_span(jj):
        for ci, (_, qc) in enumerate(chains):
            for r in range(ratio):
                s_s[jj % 2, ci, r] = _dot(kb_s[ci, pl.ds(pl.multiple_of((jj * ratio + r) * tk, tk), tk), :], qc)

    score_span(0)

    def off_group(jj, carry):
        for ci, (hh, _) in enumerate(chains):
            boff = boff_s[hh]
            _col_update(ci, [(s_s[jj % 2, ci, r] + boff, vt_s[hh, jj * ratio + r],
                              ((i - jj) * tq - r * tk).astype(F32) * (-slope2[hh])) for r in range(ratio)],
                        m_s, acc_s)
        score_span(jj + 1)
        return carry

    lax.fori_loop(0, i, off_group, 0)
    dists = [rel - r * tk for r in range(ratio)]
    for ci, (hh, _) in enumerate(chains):
        _col_update(ci, [(s_s[i % 2, ci, r] + jnp.where(dists[r] >= 0, dists[r].astype(F32) * (-slope2[hh]), -jnp.inf),
                          vt_s[hh, i * ratio + r], None) for r in range(ratio)], m_s, acc_s)

    outs = []
    for hh in range(pair):
        o = _col_finish(acc_s[2 * hh], w) - lam_ref[0] * _col_finish(acc_s[2 * hh + 1], w)
        ms = jnp.mean(o * o, axis=0, keepdims=True)
        outs.append(o * lax.rsqrt(ms + NORM_EPS))
    o_ref[...] = (jnp.concatenate(outs, axis=0) * gain_ref[...] * (1.0 - lam_init)).T


def _diff_prompt(yc, bsz, t, gain, lam, slopes, lam_init, tq, tk):
    w = gain.shape[0]
    nh = yc.shape[1] // (3 * w)
    pair = LANE // w
    nhp = nh // pair
    nq = t // tq
    dva = w + ONES_PAD
    smem = pl.BlockSpec(memory_space=pltpu.SMEM)
    return pl.pallas_call(
        functools.partial(_diff_prompt_body, t=t, tq=tq, tk=tk, cd=w // 2, lam_init=lam_init),
        grid=(bsz, nhp, nq),
        in_specs=[smem, smem,
                  pl.BlockSpec((tq, LANE), lambda b, hp, i: (b * nq + i, hp)),
                  pl.BlockSpec((t, LANE), lambda b, hp, i: (b, nhp + hp)),
                  pl.BlockSpec((t, LANE), lambda b, hp, i: (b, 2 * nhp + hp)),
                  pl.BlockSpec((LANE, 1), lambda b, hp, i: (0, 0))],
        out_specs=pl.BlockSpec((tq, LANE), lambda b, hp, i: (b * nq + i, hp)),
        out_shape=jax.ShapeDtypeStruct((bsz * t, nh * w), F32),
        scratch_shapes=[pltpu.VMEM((2 * pair, t, w // 2), BF16), pltpu.VMEM((pair, t // tk, dva, tk), BF16),
                        pltpu.VMEM((pair, tk, tq), F32), pltpu.VMEM((2 * pair, 1, tq), F32),
                        pltpu.VMEM((2 * pair, dva, tq), F32), pltpu.VMEM((2, 2 * pair, tq // tk, tk, tq), F32)],
        compiler_params=_params(("parallel", "parallel", "arbitrary")),
        name="diff_attn_prompt",
    )(slopes, lam, yc, yc, yc, jnp.tile(gain, pair).reshape(LANE, 1))


def _compress(xk_ref, xv_ref, w_ref, nb):
    def body(j, acc):
        ak, av = acc
        xk = xk_ref[pl.ds(j, nb, stride=NSA_BLOCK), :].astype(BF16)
        xv = xv_ref[pl.ds(j, nb, stride=NSA_BLOCK), :].astype(BF16)
        return ak + _dot(xk, w_ref[0, j]), av + _dot(xv, w_ref[1, j])

    zero = jnp.zeros((nb, xk_ref.shape[1]), F32)
    ak, av = lax.fori_loop(0, NSA_BLOCK, body, (zero, zero))
    return jnp.concatenate([ak, av], axis=1)


def _compress_body(xk_ref, xv_ref, w_ref, c_ref, o_ref, *, nb, hd):
    acc = _compress(xk_ref, xv_ref, w_ref, nb) + c_ref[...]
    for p in range(4):
        o_ref[p] = acc[:, p * hd:(p + 1) * hd]


def _compress_prompt(yb, bsz, t, w_bd, cconst, hd):
    nb = t // NSA_BLOCK
    cw = 4 * hd
    hw = cw // 2
    return pl.pallas_call(
        functools.partial(_compress_body, nb=nb, hd=hd),
        grid=(bsz,),
        in_specs=[pl.BlockSpec((t, hw), lambda b: (b, 2)),
                  pl.BlockSpec((t, hw), lambda b: (b, 3)),
                  pl.BlockSpec(w_bd.shape, lambda b: (0, 0, 0, 0)),
                  pl.BlockSpec((1, cw), lambda b: (0, 0))],
        out_specs=pl.BlockSpec((None, 4, nb, hd), lambda b: (b, 0, 0, 0)),
        out_shape=jax.ShapeDtypeStruct((bsz, 4, nb, hd), F32),
        compiler_params=_params(("parallel",)),
        name="nsa_compress_prompt",
    )(yb, yb, w_bd, cconst)


def _rank_rows(score, tq):
    nb = score.shape[0]
    sub = 8
    slabs = [score[v * sub:(v + 1) * sub, :] for v in range(nb // sub)]
    ranks = [jnp.zeros((sub, tq), F32) for _ in slabs]
    sub_i = lax.broadcasted_iota(jnp.int32, (sub, 1), 0)
    for ib in range(nb):
        row = score[ib:ib + 1, :]
        for v, slab in enumerate(slabs):
            if ib < v * sub:
                ahead = jnp.where(row >= slab, 1.0, 0.0)
            elif ib >= (v + 1) * sub:
                ahead = jnp.where(row > slab, 1.0, 0.0)
            else:
                ahead = jnp.where(sub_i > ib - v * sub, jnp.where(row >= slab, 1.0, 0.0),
                                  jnp.where(row > slab, 1.0, 0.0))
            ranks[v] = ranks[v] + ahead
    return jnp.concatenate(ranks, axis=0)


def _nsa_prompt_body(sl_ref, q_ref, cmp_ref, ks_ref, vs_ref, kw_ref, vw_ref, gl_ref, gb_ref, o_ref,
                     ksb_s, kwb_s, vst_s, vwt_s, boff_s, m_s, acc_s, s_s, *, t, tq, tk, nb, hd, k_sel):
    i = pl.program_id(1)
    heads = [(g, n) for g in range(B_KV_HEADS) for n in range(B_GROUP)]
    nh = len(heads)

    @pl.when(i == 0)
    def _():
        g_lanes = [g * hd for g in range(B_KV_HEADS)]
        _stage_keys_values(ks_ref, vs_ref, ksb_s, vst_s, t=t, tk=tk, w=hd, k_lanes=g_lanes)
        _stage_keys_values(kw_ref, vw_ref, kwb_s, vwt_s, t=t, tk=tk, w=hd, k_lanes=g_lanes)

    qt = (q_ref[...] * (hd ** -0.5 * LOG2E)).T
    qh = [qt[hi * hd:(hi + 1) * hd, :].astype(BF16) for hi in range(nh)]
    sl2 = [sl_ref[hi] * LOG2E for hi in range(nh)]
    qpos = i * tq + lax.broadcasted_iota(jnp.int32, (1, tq), 1)

    blk = lax.broadcasted_iota(jnp.int32, (nb, 1), 0)
    distc = qpos - (blk * NSA_BLOCK + (NSA_BLOCK - 1))
    distc_f = distc.astype(F32)
    cur = qpos // NSA_BLOCK
    forced = (blk == 0) | (blk == cur) | (blk == cur - 1)
    o_cmp = []
    selneg = []
    for g in range(B_KV_HEADS):
        kc = cmp_ref[g].astype(BF16)
        vc = cmp_ref[B_KV_HEADS + g].astype(BF16)
        imp = None
        for n in range(B_GROUP):
            hi = g * B_GROUP + n
            s = jnp.where(distc >= 0, _dot(kc, qh[hi]) - sl2[hi] * distc_f, -jnp.inf)
            m = jnp.max(s, axis=0, keepdims=True)
            e = jnp.exp2(s - jnp.where(m == -jnp.inf, 0.0, m))
            den = jnp.sum(e, axis=0, keepdims=True)
            p = e / jnp.where(den > 0, den, 1.0)
            o_cmp.append(lax.dot_general(vc, p.astype(BF16), TN_DIMS, preferred_element_type=F32))
            imp = p if imp is None else imp + p
        score = jnp.where(blk > cur, -jnp.inf, jnp.where(forced, jnp.inf, imp))
        selneg.append(jnp.where(_rank_rows(score, tq) < k_sel, 0.0, NEG_BIG).astype(BF16))

    rel = lax.broadcasted_iota(jnp.int32, (tk, tq), 1) - lax.broadcasted_iota(jnp.int32, (tk, tq), 0)
    rel_f = rel.astype(F32)
    for hi in range(nh):
        boff_s[hi] = rel_f * (-sl2[hi])
    ratio = tq // tk
    erow = lax.broadcasted_iota(jnp.int32, (tk, nb), 0)
    eblk = lax.broadcasted_iota(jnp.int32, (tk, nb), 1)

    def block_masks(j):
        expand = jnp.where((erow + j * tk) // NSA_BLOCK == eblk, 1.0, 0.0).astype(BF16)
        return [_dot(expand, sn) for sn in selneg]

    _col_reset(m_s, acc_s)

    def key_tiles(kb_s, jj):
        return [[kb_s[g, pl.ds(pl.multiple_of((jj * ratio + r) * tk, tk), tk), :] for r in range(ratio)]
                for g in range(B_KV_HEADS)]

    def score_span(jj):
        kts = key_tiles(ksb_s, jj)
        mbs = [block_masks(jj * ratio + r) for r in range(ratio)]
        for hi, (g, _) in enumerate(heads):
            for r in range(ratio):
                s_s[jj % 2, hi, r] = _dot(kts[g][r], qh[hi]) + mbs[r][g]

    score_span(0)

    def sel_off(jj, carry):
        for hi, (g, _) in enumerate(heads):
            boff = boff_s[hi]
            _col_update(hi, [(s_s[jj % 2, hi, r] + boff, vst_s[g, jj * ratio + r],
                              ((i - jj) * tq - r * tk).astype(F32) * (-sl2[hi])) for r in range(ratio)], m_s, acc_s)
        score_span(jj + 1)
        return carry

    lax.fori_loop(0, i, sel_off, 0)
    dists = [rel - r * tk for r in range(ratio)]
    for hi, (g, _) in enumerate(heads):
        _col_update(hi, [(s_s[i % 2, hi, r] + jnp.where(dists[r] >= 0, dists[r].astype(F32) * (-sl2[hi]), -jnp.inf),
                          vst_s[g, i * ratio + r], None) for r in range(ratio)], m_s, acc_s)
    o_sel = [_col_finish(acc_s[hi], hd) for hi in range(nh)]

    _col_reset(m_s, acc_s)
    for ds in range(-(NSA_WINDOW // tq), 1):
        jj = jnp.maximum(i + ds, 0)
        kts = key_tiles(kwb_s, jj)
        dists = [rel - (ds * tq + r * tk) for r in range(ratio)]
        gone = jnp.where(i + ds >= 0, 0, NSA_WINDOW) if ds < 0 else 0
        oks = [(d >= 0) & (d + gone < NSA_WINDOW) for d in dists]
        for hi, (g, _) in enumerate(heads):
            _col_update(hi, [(_dot(kts[g][r], qh[hi])
                              + jnp.where(oks[r], dists[r].astype(F32) * (-sl2[hi]), -jnp.inf),
                              vwt_s[g, jj * ratio + r], None) for r in range(ratio)], m_s, acc_s)
    o_win = [_col_finish(acc_s[hi], hd) for hi in range(nh)]

    gates = jax.nn.sigmoid(gl_ref[...] + gb_ref[...]).T
    outs = [gates[3 * hi:3 * hi + 1] * o_cmp[hi] + gates[3 * hi + 1:3 * hi + 2] * o_sel[hi]
            + gates[3 * hi + 2:3 * hi + 3] * o_win[hi] for hi in range(nh)]
    o_ref[...] = jnp.concatenate(outs, axis=0).T


def _nsa_prompt(yb, cmp_kv, bsz, t, gate_b, slopes, tq, tk):
    nb, hd = cmp_kv.shape[2], cmp_kv.shape[3]
    qw = B_HEADS * hd
    nq = t // tq
    first = qw // LANE
    dva = hd + ONES_PAD
    kv_spec = lambda c: pl.BlockSpec((t, LANE), lambda b, i: (b, first + c))
    return pl.pallas_call(
        functools.partial(_nsa_prompt_body, t=t, tq=tq, tk=tk, nb=nb, hd=hd, k_sel=min(NSA_TOPK, nb)),
        grid=(bsz, nq),
        in_specs=[pl.BlockSpec(memory_space=pltpu.SMEM),
                  pl.BlockSpec((tq, qw), lambda b, i: (b * nq + i, 0)),
                  pl.BlockSpec((None, 2 * B_KV_HEADS, nb, hd), lambda b, i: (b, 0, 0, 0)),
                  kv_spec(2), kv_spec(3), kv_spec(4), kv_spec(5),
                  pl.BlockSpec((tq, LANE), lambda b, i: (b * nq + i, first + 6)),
                  pl.BlockSpec((1, LANE), lambda b, i: (0, 0))],
        out_specs=pl.BlockSpec((tq, qw), lambda b, i: (b * nq + i, 0)),
        out_shape=jax.ShapeDtypeStruct((bsz * t, qw), F32),
        scratch_shapes=[pltpu.VMEM((B_KV_HEADS, t, hd), BF16), pltpu.VMEM((B_KV_HEADS, t, hd), BF16),
                        pltpu.VMEM((B_KV_HEADS, t // tk, dva, tk), BF16),
                        pltpu.VMEM((B_KV_HEADS, t // tk, dva, tk), BF16),
                        pltpu.VMEM((B_HEADS, tk, tq), F32), pltpu.VMEM((B_HEADS, 1, tq), F32),
                        pltpu.VMEM((B_HEADS, dva, tq), F32), pltpu.VMEM((2, B_HEADS, tq // tk, tk, tq), F32)],
        compiler_params=_params(("parallel", "arbitrary")),
        name="nsa_attn_prompt",
    )(slopes, yb, cmp_kv, yb, yb, yb, yb, yb, gate_b)


def _diff_decode_body(pt_ref, lam_ref, q_ref, new_ref, gain_ref, *refs, n_pp, page, past, nh, cd, lam_init, dec_t):
    pages = refs[:n_pp]
    o_ref = refs[n_pp]
    m_s, l_s, a_s = refs[n_pp + 1:]
    c = pl.program_id(1)
    w = 2 * cd * nh
    rows = 2 * dec_t * nh
    row = lax.broadcasted_iota(jnp.int32, (rows, 1), 0)
    rh = row % nh
    rt = (row % (dec_t * nh)) // nh
    slope2 = jnp.zeros((rows, 1), F32)
    for h in range(nh):
        slope2 = jnp.where(rh == h, 2.0 ** (-2 * (h + 1)) * LOG2E, slope2)
    qb = (q_ref[...] * (cd ** -0.5 * LOG2E)).astype(BF16)

    @pl.when(c == 0)
    def _():
        new = new_ref[...]
        kn = new[:, :w].astype(BF16)
        vn = new[:, w:].astype(BF16)
        col = lax.broadcasted_iota(jnp.int32, (1, new.shape[0]), 1)
        dist = rt - col
        s = jnp.where((dist >= 0) & (col < dec_t), _dot_nt(qb, kn) - slope2 * dist.astype(F32), -jnp.inf)
        m_s[...], l_s[...], a_s[...] = _online(s, vn, *_softmax_init(rows, w))

    kt = jnp.concatenate([pages[p][0:w, :].astype(BF16) for p in range(n_pp)], axis=1)
    vt = jnp.concatenate([pages[p][w:2 * w, :].astype(BF16) for p in range(n_pp)], axis=1)
    col = lax.broadcasted_iota(jnp.int32, (1, n_pp * page), 1)
    dist = (past + rt) - (c * (n_pp * page) + col)
    s = _dot(qb, kt) - slope2 * dist.astype(F32)
    m, l, a = _online(s, vt, m_s[...], l_s[...], a_s[...], v_t=True)
    m_s[...], l_s[...], a_s[...] = m, l, a

    @pl.when(c == pl.num_programs(1) - 1)
    def _():
        full = _finish(l, a)
        own = jnp.zeros((rows, 2 * cd), F32)
        for h in range(nh):
            own = jnp.where(rh == h, full[:, h * 2 * cd:(h + 1) * 2 * cd], own)
        half = rows // 2
        o = own[:half] - lam_ref[0] * own[half:]
        o_ref[...] = _rms(o, gain_ref[...]) * (1.0 - lam_init)


def _diff_decode(q_rows, new_kv, cache_t, layer, page_table, gain, lam, lam_init, dec_t, past):
    bsz, rows, w = q_rows.shape
    n_pages = page_table.shape[1]
    page = cache_t.shape[3]
    n_pp = math.gcd(PAGES_PER_STEP, n_pages)
    nh = C_HEADS
    cd = w // (2 * nh)

    def page_spec(p):
        return pl.BlockSpec((None, None, 2 * w, page),
                            lambda b, c, pt: (layer, pt[b * n_pages + c * n_pp + p], 0, 0))

    grid_spec = pltpu.PrefetchScalarGridSpec(
        num_scalar_prefetch=1,
        grid=(bsz, n_pages // n_pp),
        in_specs=[pl.BlockSpec(memory_space=pltpu.SMEM),
                  pl.BlockSpec((None, rows, w), lambda b, c, pt: (b, 0, 0)),
                  pl.BlockSpec((None, NEW_PAD, 2 * w), lambda b, c, pt: (b, 0, 0)),
                  pl.BlockSpec((1, 2 * cd), lambda b, c, pt: (0, 0))]
        + [page_spec(p) for p in range(n_pp)],
        out_specs=pl.BlockSpec((None, rows // 2, 2 * cd), lambda b, c, pt: (b, 0, 0)),
        scratch_shapes=[pltpu.VMEM((rows, 1), F32), pltpu.VMEM((rows, 1), F32), pltpu.VMEM((rows, w), F32)],
    )
    return pl.pallas_call(
        functools.partial(_diff_decode_body, n_pp=n_pp, page=page, past=past, nh=nh, cd=cd,
                          lam_init=lam_init, dec_t=dec_t),
        grid_spec=grid_spec,
        out_shape=jax.ShapeDtypeStruct((bsz, rows // 2, 2 * cd), F32),
        compiler_params=_params(("parallel", "arbitrary")),
        name="diff_attn_decode",
    )(page_table.reshape(-1), lam, q_rows, new_kv, gain.reshape(1, 2 * cd), *([cache_t] * n_pp))


def _nsa_decode_body(pt_ref, qh_ref, qd_ref, wc_ref, wn_ref, sn_ref, wbd_ref, cc_ref, gl_ref, gb_ref, *refs,
                     n_pp, page, past, hd, dec_t, n_pages, k_past, pitch):
    pages = refs[:n_pp]
    o_ref = refs[n_pp]
    x_s, ak_s, av_s, mtok_s, ocw_s, m_s, l_s, a_s = refs[n_pp + 1:]
    ph = pl.program_id(1)
    c = pl.program_id(2)
    last = pl.num_programs(2) - 1
    tp = DEC_ROWS
    rows = B_GROUP * tp
    row = lax.broadcasted_iota(jnp.int32, (rows, 1), 0)
    rt = row % tp
    scale2 = hd ** -0.5 * LOG2E
    qpos = past + rt

    def slope2(g):
        return jnp.where(row < tp, 2.0 ** (-(2 * (g * B_GROUP) + 1)) * LOG2E,
                         2.0 ** (-(2 * (g * B_GROUP + 1) + 1)) * LOG2E)

    def new_scores(qd, kn, sl):
        col = lax.broadcasted_iota(jnp.int32, (1, kn.shape[0]), 1)
        d = rt - col
        return jnp.where((d >= 0) & (col < dec_t), _dot_nt(qd, kn) - sl * d.astype(F32), -jnp.inf)

    @pl.when(ph == 0)
    def _():
        for p in range(n_pp):
            r0 = pl.multiple_of((c * n_pp + p) * pitch, 8)
            x_s[pl.ds(r0, 4 * hd), :] = pages[p][...]

    @pl.when((ph == 0) & (c == last))
    def _():
        unroll = 2

        def body(du, accs):
            accs = list(accs)
            for u in range(unroll):
                dp = du * unroll + u
                for cc in range(2):
                    lhs = jnp.concatenate(
                        [jnp.concatenate([x_s[pl.ds(cc * 2 * hd + g * hd + 2 * dp + e, n_pages, stride=pitch), :]
                                          for e in range(2)], axis=1) for g in range(B_KV_HEADS)],
                        axis=0).astype(BF16)
                    accs[cc] = accs[cc] + _dot(lhs, wbd_ref[cc, dp])
            return tuple(accs)

        zero = jnp.zeros((B_KV_HEADS * n_pages, 2 * hd), F32)
        ak_s[...], av_s[...] = lax.fori_loop(0, hd // (2 * unroll), body, (zero, zero))
        pgi = lax.broadcasted_iota(jnp.int32, (1, n_pages), 1)
        lane = lax.broadcasted_iota(jnp.int32, (1, page), 1)
        for g in range(B_KV_HEADS):
            sl = slope2(g)
            kc = (ak_s[g * n_pages:(g + 1) * n_pages, :] + cc_ref[0]).astype(BF16)
            vc = (av_s[g * n_pages:(g + 1) * n_pages, :] + cc_ref[1]).astype(BF16)
            ss = []
            for hf in range(2):
                qh = (qh_ref[g, hf] * scale2).astype(BF16)
                dist = qpos - ((2 * pgi + hf) * NSA_BLOCK + (NSA_BLOCK - 1))
                ss.append(jnp.where(dist >= 0, _dot_nt(qh, kc) - sl * dist.astype(F32), -jnp.inf))
            m = jnp.maximum(jnp.max(ss[0], axis=1, keepdims=True), jnp.max(ss[1], axis=1, keepdims=True))
            m = jnp.where(m == -jnp.inf, 0.0, m)
            es = [jnp.exp2(s - m) for s in ss]
            den = jnp.sum(es[0], axis=1, keepdims=True) + jnp.sum(es[1], axis=1, keepdims=True)
            den = jnp.where(den > 0, den, 1.0)
            ps = [e / den for e in es]
            full = [_dot(p.astype(BF16), vc) for p in ps]
            ocw_s[g, 0] = full[0][:, :hd] + full[1][:, hd:]

            imps = [p[:tp] + p[tp:] for p in ps]
            sc = [jnp.where(pgi == 0, jnp.inf, imps[0]), jnp.where(pgi == n_pages - 1, jnp.inf, imps[1])]
            rank = [jnp.zeros((tp, n_pages), F32), jnp.zeros((tp, n_pages), F32)]
            for pg in range(n_pages):
                for hf in range(2):
                    colv = sc[hf][:, pg:pg + 1]
                    for h2 in range(2):
                        tie = jnp.where(2 * pgi + h2 > 2 * pg + hf, 1.0, 0.0)
                        rank[h2] = rank[h2] + jnp.where(colv > sc[h2], 1.0, jnp.where(colv == sc[h2], tie, 0.0))
            sel = [jnp.where(r < k_past, 1.0, 0.0) for r in rank]
            sel = [jnp.concatenate([s_, s_], axis=0) for s_ in sel]
            for pg in range(n_pages):
                on = jnp.where(lane < NSA_BLOCK, sel[0][:, pg:pg + 1], sel[1][:, pg:pg + 1])
                dist = qpos - (pg * page + lane)
                lo = (pg % n_pp) * page
                mtok_s[g, pg // n_pp, :, lo:lo + page] = jnp.where(on > 0.5, dist.astype(F32) * (-sl), NEG_BIG)

            qd = (qd_ref[g] * scale2).astype(BF16)
            wkt = wc_ref[g * hd:(g + 1) * hd, :].astype(BF16)
            wvt = wc_ref[(B_KV_HEADS + g) * hd:(B_KV_HEADS + g + 1) * hd, :].astype(BF16)
            nwin = wkt.shape[1]
            d1 = (nwin + rt) - lax.broadcasted_iota(jnp.int32, (1, nwin), 1)
            s1 = jnp.where((d1 >= 0) & (d1 < NSA_WINDOW), _dot(qd, wkt) - sl * d1.astype(F32), -jnp.inf)
            wn = wn_ref[...]
            kn = wn[:, g * hd:(g + 1) * hd].astype(BF16)
            vn = wn[:, (B_KV_HEADS + g) * hd:(B_KV_HEADS + g + 1) * hd].astype(BF16)
            s2 = new_scores(qd, kn, sl)
            m = jnp.maximum(jnp.max(s1, axis=1, keepdims=True), jnp.max(s2, axis=1, keepdims=True))
            m = jnp.where(m == -jnp.inf, 0.0, m)
            e1 = jnp.exp2(s1 - m)
            e2 = jnp.exp2(s2 - m)
            den = jnp.sum(e1, axis=1, keepdims=True) + jnp.sum(e2, axis=1, keepdims=True)
            ocw_s[g, 1] = (_dot_nt(e1.astype(BF16), wvt) + _dot(e2.astype(BF16), vn)) / jnp.where(den > 0, den, 1.0)

    @pl.when(ph == 1)
    def _():
        for g in range(B_KV_HEADS):
            sl = slope2(g)
            qd = (qd_ref[g] * scale2).astype(BF16)

            @pl.when(c == 0)
            def _(g=g, sl=sl, qd=qd):
                sn = sn_ref[...]
                kn = sn[:, g * hd:(g + 1) * hd].astype(BF16)
                vn = sn[:, (B_KV_HEADS + g) * hd:(B_KV_HEADS + g + 1) * hd].astype(BF16)
                m_s[g], l_s[g], a_s[g] = _online(new_scores(qd, kn, sl), vn, *_softmax_init(rows, hd))

            kt = jnp.concatenate([pages[p][g * hd:(g + 1) * hd, :].astype(BF16) for p in range(n_pp)], axis=1)
            vt = jnp.concatenate([pages[p][(B_KV_HEADS + g) * hd:(B_KV_HEADS + g + 1) * hd, :].astype(BF16)
                                  for p in range(n_pp)], axis=1)
            s = _dot(qd, kt) + mtok_s[g, c]
            m_s[g], l_s[g], a_s[g] = _online(s, vt, m_s[g], l_s[g], a_s[g], v_t=True)

        @pl.when(c == last)
        def _():
            gates = jax.nn.sigmoid(gl_ref[...] + gb_ref[...])
            for g in range(B_KV_HEADS):
                o_sel = _finish(l_s[g], a_s[g])
                o_cmp = ocw_s[g, 0]
                o_win = ocw_s[g, 1]
                for n in range(B_GROUP):
                    r = slice(n * tp, (n + 1) * tp)
                    base = (g * B_GROUP + n) * 3
                    o_ref[g, n] = (gates[:, base:base + 1] * o_cmp[r] + gates[:, base + 1:base + 2] * o_sel[r]
                                   + gates[:, base + 2:base + 3] * o_win[r])


def _nsa_decode(q_half, q_plain, win_t, win_new, sel_new, cache_t, layer, page_table, w_bdt, cconst2,
                gate_logits, gate_b, dec_t, past, k_past):
    bsz = q_plain.shape[0]
    hd = q_plain.shape[-1]
    rows = q_plain.shape[2]
    n_pages = page_table.shape[1]
    page = cache_t.shape[3]
    nwin = win_t.shape[3]
    n_pp = math.gcd(HALF_PAGES_PER_STEP, n_pages)
    n_ch = n_pages // n_pp
    pitch = 4 * hd + PAGE_PITCH_PAD

    def page_spec(p):
        return pl.BlockSpec((None, None, 4 * hd, page),
                            lambda b, ph, c, pt: (layer, pt[b * n_pages + c * n_pp + p], ph, 0))

    fix = lambda *shape: pl.BlockSpec(shape, lambda b, ph, c, pt: (0,) * len(shape))
    grid_spec = pltpu.PrefetchScalarGridSpec(
        num_scalar_prefetch=1,
        grid=(bsz, 2, n_ch),
        in_specs=[pl.BlockSpec((None, B_KV_HEADS, 2, rows, 2 * hd), lambda b, ph, c, pt: (b, 0, 0, 0, 0)),
                  pl.BlockSpec((None, B_KV_HEADS, rows, hd), lambda b, ph, c, pt: (b, 0, 0, 0)),
                  pl.BlockSpec((None, None, 4 * hd, nwin), lambda b, ph, c, pt: (layer, b, 0, 0)),
                  pl.BlockSpec((None, NEW_PAD, 4 * hd), lambda b, ph, c, pt: (b, 0, 0)),
                  pl.BlockSpec((None, NEW_PAD, 4 * hd), lambda b, ph, c, pt: (b, 0, 0)),
                  fix(*w_bdt.shape), fix(*cconst2.shape),
                  pl.BlockSpec((None, DEC_ROWS, LANE), lambda b, ph, c, pt: (b, 0, 0)),
                  fix(1, LANE)]
        + [page_spec(p) for p in range(n_pp)],
        out_specs=pl.BlockSpec((None, B_KV_HEADS, B_GROUP, DEC_ROWS, hd), lambda b, ph, c, pt: (b, 0, 0, 0, 0)),
        scratch_shapes=[pltpu.VMEM((n_pages * pitch, page), F32),
                        pltpu.VMEM((B_KV_HEADS * n_pages, 2 * hd), F32),
                        pltpu.VMEM((B_KV_HEADS * n_pages, 2 * hd), F32),
                        pltpu.VMEM((B_KV_HEADS, n_ch, rows, n_pp * page), F32),
                        pltpu.VMEM((B_KV_HEADS, 2, rows, hd), F32),
                        pltpu.VMEM((B_KV_HEADS, rows, 1), F32), pltpu.VMEM((B_KV_HEADS, rows, 1), F32),
                        pltpu.VMEM((B_KV_HEADS, rows, hd), F32)],
    )
    return pl.pallas_call(
        functools.partial(_nsa_decode_body, n_pp=n_pp, page=page, past=past, hd=hd, dec_t=dec_t,
                          n_pages=n_pages, k_past=k_past, pitch=pitch),
        grid_spec=grid_spec,
        out_shape=jax.ShapeDtypeStruct((bsz, B_KV_HEADS, B_GROUP, DEC_ROWS, hd), F32),
        compiler_params=_params(("parallel", "arbitrary", "arbitrary")),
        name="nsa_decode",
    )(page_table.reshape(-1), q_half, q_plain, win_t, win_new, sel_new, w_bdt, cconst2, gate_logits, gate_b,
      *([cache_t] * n_pp))


def _split_points(d_model):
    br = d_model // N_BRANCH
    hd_b = br // B_HEADS
    d_dk = br // (2 * D_HEADS)
    splits = (br, br, br, br, br, 6 * B_KV_HEADS * hd_b, 3 * B_HEADS, br, br, br,
              D_HEADS * d_dk, D_HEADS * d_dk, br, GLA_RANK, br, N_BRANCH * d_model)
    return [0] + [int(p) for p in np.cumsum(splits)]


def _layer_weights(l, d_model, w_in, ffn_w_in, ffn_w_out, w_branch, w_out, nsa_cmp_pe, nsa_cmp_w,
                   gla_gate_w2, gla_gate_b, nsa_gate_b):
    pts = _split_points(d_model)
    wt = jnp.transpose(w_in, (2, 0, 1))[:, l, :]
    seg = lambda i, j: wt[pts[i]:pts[j]]
    padl = lambda a: jnp.pad(a, ((0, LANE - a.shape[0]), (0, 0)))
    wa = seg(0, 4)
    wb = jnp.concatenate([seg(4, 6), padl(seg(6, 7))], axis=0)
    wc = seg(7, 10)
    wd = jnp.concatenate([seg(10, 13), padl(seg(13, 14)), seg(14, 15)], axis=0)
    hd = nsa_cmp_w.shape[-1]
    eye = jnp.eye(B_KV_HEADS, dtype=F32)
    w_bd = jnp.einsum('cjde,gh->cjgdhe', nsa_cmp_w[l], eye).reshape(
        2, NSA_BLOCK, B_KV_HEADS * hd, B_KV_HEADS * hd)
    w_bdt = jnp.einsum('cjde,ab->cdajbe', nsa_cmp_w[l], eye).reshape(
        2, hd // 2, 2 * 2 * NSA_BLOCK, 2 * hd)
    cvec = jnp.einsum('cjd,cjde->ce', nsa_cmp_pe[l], nsa_cmp_w[l])
    cconst = jnp.repeat(cvec, B_KV_HEADS, axis=0).reshape(1, 4 * hd)
    cconst2 = jnp.concatenate([cvec, cvec], axis=1).reshape(2, 1, 2 * hd)
    w2 = jnp.pad(gla_gate_w2[l], ((0, LANE - GLA_RANK), (0, 0)))
    return dict(
        proj=[a.astype(BF16) for a in (wa, wb, wc, wd)],
        w_merge=seg(15, 16).astype(BF16), w_branch=w_branch[l].astype(BF16), w_out=w_out[l].astype(BF16),
        ffn_in=ffn_w_in[l].astype(BF16), ffn_out=ffn_w_out[l].astype(BF16),
        w_bd=w_bd.astype(BF16), w_bdt=w_bdt.astype(BF16), cconst=cconst, cconst2=cconst2,
        gla_w2=w2, gla_b=gla_gate_b[l].reshape(1, -1),
        gate_b=jnp.pad(nsa_gate_b[l].reshape(1, -1), ((0, 0), (0, LANE - 3 * B_HEADS))),
    )


def _mixer_common(x, lw, g2, vec_a, vec_d, bsz, t, state_a, state_d, br, cache_rows_t=False):
    a_dk = br // A_HEADS
    d_dk = br // (2 * D_HEADS)
    d_dv = br // D_HEADS
    ya, yb, yc, yd, *new_t = _proj(x, g2, lw['proj'], br, (bsz, t) if cache_rows_t else None)
    o_a, st_a = _linear_mixer(functools.partial(_hgrn_body, w=br, dk=a_dk), ya, bsz, t, vec_a, state_a,
                              A_HEADS, a_dk, a_dk)
    o_d, st_d = _linear_mixer(functools.partial(_gla_body, wk=D_HEADS * d_dk, wv=br, dk=d_dk, dv=d_dv),
                              yd, bsz, t, vec_d, state_d, D_HEADS, d_dk, d_dv)
    return yb, yc, o_a, st_a, o_d, st_d, new_t


def _feature_major(cache):
    l, n, t = cache.shape[:3]
    return jnp.transpose(cache, (0, 1, 3, 4, 5, 2)).reshape(l, n, -1, t)


def kernel(x_prompt, x_sample, cache_nsa_kv, cache_nsa_win, cache_diff_kv, state_hgrn, state_gla, page_table,
           norm_gains, ffn_w_in, ffn_w_out, w_in, hgrn_lb_logits, hgrn_norm_gain, nsa_cmp_pe, nsa_cmp_w,
           nsa_gate_b, diff_lambda, diff_norm_gain, gla_gate_w2, gla_gate_b, gla_norm_gain, w_branch, w_out):
    bp, tp, d_model = x_prompt.shape
    bs, ts, _ = x_sample.shape
    depth = w_in.shape[0]
    br = d_model // N_BRANCH
    hd = br // B_HEADS
    cd2 = br // C_HEADS
    page = cache_nsa_kv.shape[2]
    n_pages = page_table.shape[1]
    past = n_pages * page
    nwin = cache_nsa_win.shape[2]
    assert past % NSA_BLOCK == 0 and ts <= DEC_ROWS and tp % NSA_BLOCK == 0
    assert past // NSA_BLOCK >= NSA_TOPK and page == 2 * NSA_BLOCK and nwin == NSA_WINDOW
    k_past = NSA_TOPK - 1
    tq = min(256, tp)
    tk = min(128, tq)

    lb_cum = jnp.cumsum(jax.nn.softmax(hgrn_lb_logits.astype(F32), axis=0), axis=0)
    lower = lb_cum - lb_cum[0]
    slopes = 2.0 ** (-np.arange(1, B_HEADS + C_HEADS + 1, dtype=np.float64))
    sl_b = jnp.asarray(slopes[0::2], F32)
    sl_c = jnp.asarray(slopes[1::2], F32)

    nsa_pool_t = _feature_major(cache_nsa_kv)
    diff_pool_t = _feature_major(cache_diff_kv)
    win_pool_t = _feature_major(cache_nsa_win)

    xp = x_prompt.reshape(bp * tp, d_model)
    xs = x_sample.reshape(bs * ts, d_model)
    outs = {k: [] for k in ('kvp', 'kvs', 'winp', 'wins', 'dkp', 'dks', 'hp', 'hs', 'gp', 'gs')}

    for l in range(depth):
        lw = _layer_weights(l, d_model, w_in, ffn_w_in, ffn_w_out, w_branch, w_out, nsa_cmp_pe, nsa_cmp_w,
                            gla_gate_w2, gla_gate_b, nsa_gate_b)
        g = norm_gains[l]
        lb = lower[l].reshape(1, br)
        vec_a = (jnp.log1p(-lb), jnp.log(lb), hgrn_norm_gain[l].reshape(1, br))
        vec_d = (lw['gla_w2'], lw['gla_b'], gla_norm_gain[l].reshape(1, br))
        lv = diff_lambda[l].astype(F32)
        lam_init = 0.8 - 0.6 * math.exp(-0.3 * l)
        lam = (jnp.exp(jnp.sum(lv[0] * lv[1])) - jnp.exp(jnp.sum(lv[2] * lv[3])) + lam_init).reshape(1)

        xp = _ffn(xp, g[0], g[1], lw['ffn_in'][0], lw['ffn_out'][0])
        yb, yc, o_a, st_a, o_d, st_d, (kv_t, win_t, dk_t) = _mixer_common(
            xp, lw, g[2], vec_a, vec_d, bp, tp, None, None, br, cache_rows_t=True)
        outs['hp'].append(st_a)
        outs['gp'].append(st_d)
        outs['kvp'].append(kv_t)
        outs['winp'].append(win_t[:, :, -min(NSA_WINDOW, tp):])
        outs['dkp'].append(dk_t)

        o_c = _diff_prompt(yc, bp, tp, diff_norm_gain[l], lam, sl_c, lam_init, tq, tk)
        cmp_kv = _compress_prompt(yb, bp, tp, lw['w_bd'], lw['cconst'], hd)
        o_b = _nsa_prompt(yb, cmp_kv, bp, tp, lw['gate_b'], sl_b, tq, tk)

        xp = _merge(xp, g[2], g[3], (o_a, o_b, o_c, o_d), lw['w_merge'], lw['w_branch'], lw['w_out'])
        xp = _ffn(xp, g[4], g[5], lw['ffn_in'][1], lw['ffn_out'][1])

        xs = _ffn(xs, g[0], g[1], lw['ffn_in'][0], lw['ffn_out'][0])
        yb, yc, o_a, st_a, o_d, st_d, _ = _mixer_common(xs, lw, g[2], vec_a, vec_d, bs, ts,
                                                        state_hgrn[l], state_gla[l], br)
        outs['hs'].append(st_a)
        outs['gs'].append(st_d)
        new_kv = yb[:, br:br + 6 * B_KV_HEADS * hd].reshape(bs, ts, 6, B_KV_HEADS * hd)
        outs['kvs'].append(new_kv[:, :, :4].reshape(bs, ts, 4, B_KV_HEADS, hd))
        new_win = new_kv[:, :, 4:].reshape(bs, ts, 2 * B_KV_HEADS * hd)
        outs['wins'].append(jnp.concatenate(
            [cache_nsa_win[l][:, ts:], new_win.reshape(bs, ts, 2, B_KV_HEADS, hd)], axis=1))
        outs['dks'].append(yc[:, br:3 * br].reshape(bs, ts, 2, C_HEADS, cd2))
        pad_new = lambda a: jnp.pad(a, ((0, 0), (0, NEW_PAD - ts), (0, 0)))

        cq = yc[:, 0:br].reshape(bs, ts, C_HEADS, 2, cd2 // 2)
        sel = (jnp.arange(C_HEADS)[:, None, None, None] == jnp.arange(C_HEADS)[None, None, :, None]) & \
              (jnp.arange(2)[None, :, None, None] == jnp.arange(2)[None, None, None, :])
        q_rows = jnp.einsum('bthmd,hmgn->bmthgnd', cq, sel.astype(F32)).reshape(bs, 2 * ts * C_HEADS, br)
        new_c = pad_new(yc[:, br:3 * br].reshape(bs, ts, 2 * br))
        o_c = _diff_decode(q_rows, new_c, diff_pool_t, l, page_table, diff_norm_gain[l], lam, lam_init, ts, past)
        o_c = o_c.reshape(bs * ts, br)

        bq = yb[:, 0:br].reshape(bs, ts, B_KV_HEADS, B_GROUP, hd).transpose(0, 2, 3, 1, 4)
        bq = jnp.pad(bq, ((0, 0), (0, 0), (0, 0), (0, DEC_ROWS - ts), (0, 0)))
        q_plain = bq.reshape(bs, B_KV_HEADS, B_GROUP * DEC_ROWS, hd)
        zero = jnp.zeros_like(q_plain)
        q_half = jnp.stack([jnp.concatenate([q_plain, zero], axis=-1),
                            jnp.concatenate([zero, q_plain], axis=-1)], axis=2)
        gl = yb[:, br + 6 * B_KV_HEADS * hd:].reshape(bs, ts, LANE)
        gl = jnp.pad(gl, ((0, 0), (0, DEC_ROWS - ts), (0, 0)))
        o_b = _nsa_decode(q_half, q_plain, win_pool_t, pad_new(new_win),
                          pad_new(new_kv[:, :, 2:4].reshape(bs, ts, 4 * hd)), nsa_pool_t, l, page_table,
                          lw['w_bdt'], lw['cconst2'], gl, lw['gate_b'], ts, past, k_past)
        o_b = o_b[:, :, :, :ts].transpose(0, 3, 1, 2, 4).reshape(bs * ts, br)

        xs = _merge(xs, g[2], g[3], (o_a, o_b, o_c, o_d), lw['w_merge'], lw['w_branch'], lw['w_out'])
        xs = _ffn(xs, g[4], g[5], lw['ffn_in'][1], lw['ffn_out'][1])

    st = lambda k: jnp.stack(outs[k])

    def token_major(k, a, b):
        x = st(k)
        return x.reshape(x.shape[:2] + (a, b, -1, x.shape[-1])).transpose(0, 1, 5, 2, 3, 4)

    return (xp.reshape(bp, tp, d_model), xs.reshape(bs, ts, d_model),
            token_major('kvp', 4, B_KV_HEADS), st('kvs'), token_major('winp', 2, B_KV_HEADS), st('wins'),
            token_major('dkp', 2, C_HEADS), st('dks'), st('hp'), st('hs'), st('gp'), st('gs'))
```
